```python
import jax
import jax.numpy as jnp
from jax import lax

D_MODEL = 1024
BATCH = 8
SEQ = 2048
DEPTH = 2

CHUNK = 64
N_META = 16
D_CONV = 512
CONV_WIDTH = 31
N_HEADS = 8
Q_LORA = 256
KV_LORA = 128
NOPE_DIM = 64
ROPE_DIM = 32
QK_DIM = NOPE_DIM + ROPE_DIM
V_DIM = 64
ROPE_THETA = 10000.0
Q_BLOCK = 128
N_IN = 2 * D_CONV + Q_LORA + KV_LORA + ROPE_DIM + 2 * D_MODEL
PEER_HEADS = 8
PEER_QDIM = 256
PEER_HALF = PEER_QDIM // 2
N_KEYS = 128
N_EXPERTS = N_KEYS * N_KEYS
PEER_TOPK = 16
PEER_BLOCK = 256

EPS = 1e-6
MASK_VALUE = -1e30
PAD_CHUNK = 2 ** 30

kernel_name = 'hybrid_conv_mla_peer_trunk'


def _rmsnorm(x, g):
    xf = x.astype(jnp.float32)
    y = xf * lax.rsqrt(jnp.mean(xf * xf, axis=-1, keepdims=True) + EPS)
    return (y * g.astype(jnp.float32)).astype(x.dtype)


def _layernorm(x, g, b):
    xf = x.astype(jnp.float32)
    mu = jnp.mean(xf, axis=-1, keepdims=True)
    xc = xf - mu
    y = xc * lax.rsqrt(jnp.mean(xc * xc, axis=-1, keepdims=True) + EPS)
    return (y * g.astype(jnp.float32) + b.astype(jnp.float32)).astype(x.dtype)


def _rope_tables(length):
    pos = jnp.arange(length, dtype=jnp.float32)
    inv = 1.0 / (ROPE_THETA ** (jnp.arange(0, ROPE_DIM, 2, dtype=jnp.float32) / ROPE_DIM))
    ang = pos[:, None] * inv[None, :]
    ang = jnp.concatenate([ang, ang], axis=-1)
    return jnp.cos(ang), jnp.sin(ang)


def _apply_rope(x, cos, sin):
    half = ROPE_DIM // 2
    x1, x2 = x[..., :half], x[..., half:]
    rot = jnp.concatenate([-x2, x1], axis=-1)
    c = cos[None, :, None, :].astype(x.dtype)
    s = sin[None, :, None, :].astype(x.dtype)
    return x * c + rot * s


def _chunk_ids(length):
    p = jnp.arange(length, dtype=jnp.int32)
    return jnp.where(p < N_META, 0, 1 + (p - N_META) // CHUNK).astype(jnp.int32)


def _chunk_causal_attention(q, k, v, chunk):
    B, L, H, dk = q.shape
    dv = v.shape[-1]
    l_pad = -(-L // Q_BLOCK) * Q_BLOCK
    pad = l_pad - L
    padw = ((0, 0), (0, pad), (0, 0), (0, 0))
    q = jnp.pad(q, padw)
    k = jnp.pad(k, padw).astype(jnp.float32)
    v = jnp.pad(v, padw)
    cid = jnp.pad(chunk, (0, pad), constant_values=PAD_CHUNK)
    scale = dk ** -0.5

    def block(i):
        qs = lax.dynamic_slice_in_dim(q, i * Q_BLOCK, Q_BLOCK, axis=1).astype(jnp.float32)
        qc = lax.dynamic_slice_in_dim(cid, i * Q_BLOCK, Q_BLOCK)
        s = jnp.einsum('bqhd,bkhd->bhqk', qs, k) * scale
        mask = cid[None, :] <= qc[:, None]
        s = jnp.where(mask[None, None], s, MASK_VALUE)
        p = jax.nn.softmax(s, axis=-1)
        return jnp.einsum('bhqk,bkhd->bqhd', p, v.astype(jnp.float32)).astype(v.dtype)

    out = lax.map(block, jnp.arange(l_pad // Q_BLOCK))
    out = jnp.transpose(out, (1, 0, 2, 3, 4)).reshape(B, l_pad, H, dv)
    return out[:, :L]


def _causal_depthwise_conv(x, w, b):
    C = x.shape[-1]
    xp = jnp.pad(x, ((0, 0), (CONV_WIDTH - 1, 0), (0, 0)))
    y = lax.conv_general_dilated(xp, w[:, None, :].astype(x.dtype), window_strides=(1,),
                                 padding='VALID', dimension_numbers=('NWC', 'WIO', 'NWC'),
                                 feature_group_count=C)
    return y + b.astype(x.dtype)


def _mixer_block(x, mix_g, w_in, conv_w, conv_b, conv_ln_g, conv_ln_b, w_conv_out,
                 q_a_g, w_uq, kv_a_g, w_ukv, q_norm_g, k_norm_g, w_mla_out, w_out,
                 cos, sin, chunk):
    B, L, _ = x.shape
    h = _rmsnorm(x, mix_g)
    z = h @ w_in
    o0 = 2 * D_CONV
    o1 = o0 + Q_LORA
    o2 = o1 + KV_LORA
    o3 = o2 + ROPE_DIM
    conv_in, c_q, c_kv, k_rope, gate_logits = (z[..., :o0], z[..., o0:o1], z[..., o1:o2],
                                              z[..., o2:o3], z[..., o3:])

    u = conv_in[..., :D_CONV] * jax.nn.sigmoid(conv_in[..., D_CONV:])
    u = _causal_depthwise_conv(u, conv_w, conv_b)
    u = jax.nn.silu(_layernorm(u, conv_ln_g, conv_ln_b))
    y_conv = u @ w_conv_out

    q = (_rmsnorm(c_q, q_a_g) @ w_uq).reshape(B, L, N_HEADS, QK_DIM)
    kv = (_rmsnorm(c_kv, kv_a_g) @ w_ukv).reshape(B, L, N_HEADS, NOPE_DIM + V_DIM)
    k_nope, v = kv[..., :NOPE_DIM], kv[..., NOPE_DIM:]
    k_r = jnp.broadcast_to(k_rope[:, :, None, :], (B, L, N_HEADS, ROPE_DIM))
    k = jnp.concatenate([k_nope, k_r], axis=-1)
    q = _rmsnorm(q, q_norm_g)
    k = _rmsnorm(k, k_norm_g)
    q = jnp.concatenate([q[..., :NOPE_DIM], _apply_rope(q[..., NOPE_DIM:], cos, sin)], axis=-1)
    k = jnp.concatenate([k[..., :NOPE_DIM], _apply_rope(k[..., NOPE_DIM:], cos, sin)], axis=-1)
    o = _chunk_causal_attention(q, k, v, chunk).reshape(B, L, N_HEADS * V_DIM)
    y_mla = o @ w_mla_out

    gates = jax.nn.sigmoid(gate_logits)
    merged = gates[..., :D_MODEL] * y_conv + gates[..., D_MODEL:] * y_mla
    return x + merged @ w_out


def _peer(h, wq, keys, u_table, v_table):
    B, L, D = h.shape
    T = B * L
    hf = h.reshape(T, D)
    q = (hf @ wq).reshape(T, PEER_HEADS, 2, PEER_HALF)
    s = jnp.einsum('thpd,hpnd->thpn', q, keys)
    sv, si = lax.top_k(s, PEER_TOPK)
    cand = (sv[:, :, 0, :, None] + sv[:, :, 1, None, :]).reshape(T, PEER_HEADS, PEER_TOPK * PEER_TOPK)
    cidx = (si[:, :, 0, :, None] * N_KEYS + si[:, :, 1, None, :]).reshape(T, PEER_HEADS, PEER_TOPK * PEER_TOPK)
    top_s, top_pos = lax.top_k(cand, PEER_TOPK)
    eidx = jnp.take_along_axis(cidx, top_pos, axis=-1)
    gw = jax.nn.softmax(top_s.astype(jnp.float32), axis=-1).astype(h.dtype)
    eidx = eidx.reshape(T, PEER_HEADS * PEER_TOPK)
    gw = gw.reshape(T, PEER_HEADS * PEER_TOPK)

    t_pad = -(-T // PEER_BLOCK) * PEER_BLOCK
    pad = t_pad - T
    nb = t_pad // PEER_BLOCK
    xs = jnp.pad(hf, ((0, pad), (0, 0))).reshape(nb, PEER_BLOCK, D)
    ids = jnp.pad(eidx, ((0, pad), (0, 0))).reshape(nb, PEER_BLOCK, PEER_HEADS * PEER_TOPK)
    ws = jnp.pad(gw, ((0, pad), (0, 0))).reshape(nb, PEER_BLOCK, PEER_HEADS * PEER_TOPK)

    def block(args):
        xb, ib, wb = args
        a = jnp.einsum('ted,td->te', u_table[ib], xb)
        act = jax.nn.gelu(a) * wb
        return jnp.einsum('te,ted->td', act, v_table[ib])

    out = lax.map(block, (xs, ids, ws)).reshape(t_pad, D)[:T]
    return out.reshape(B, L, D)


def setup_inputs(seed: int = 0) -> dict:
    key = jax.random.key(seed)
    ks = jax.random.split(key, 24)
    f32 = jnp.float32

    def w(k, shape, fan_in):
        return jax.random.normal(k, shape, f32) * (fan_in ** -0.5)

    def gain(k, shape):
        return 1.0 + 0.02 * jax.random.normal(k, shape, f32)

    def bias(k, shape):
        return 0.02 * jax.random.normal(k, shape, f32)

    return {
        'x': jax.random.normal(ks[0], (BATCH, SEQ, D_MODEL), f32),
        'meta_tokens': jax.random.normal(ks[1], (N_META, D_MODEL), f32),
        'mix_norm_g': gain(ks[2], (DEPTH, D_MODEL)),
        'w_in': w(ks[3], (DEPTH, D_MODEL, N_IN), D_MODEL),
        'conv_w': w(ks[4], (DEPTH, CONV_WIDTH, D_CONV), CONV_WIDTH),
        'conv_b': bias(ks[5], (DEPTH, D_CONV)),
        'conv_ln_g': gain(ks[6], (DEPTH, D_CONV)),
        'conv_ln_b': bias(ks[7], (DEPTH, D_CONV)),
        'w_conv_out': w(ks[8], (DEPTH, D_CONV, D_MODEL), D_CONV),
        'q_a_norm_g': gain(ks[9], (DEPTH, Q_LORA)),
        'w_uq': w(ks[10], (DEPTH, Q_LORA, N_HEADS * QK_DIM), Q_LORA),
        'kv_a_norm_g': gain(ks[11], (DEPTH, KV_LORA)),
        'w_ukv': w(ks[12], (DEPTH, KV_LORA, N_HEADS * (NOPE_DIM + V_DIM)), KV_LORA),
        'q_norm_g': gain(ks[13], (DEPTH, QK_DIM)),
        'k_norm_g': gain(ks[14], (DEPTH, QK_DIM)),
        'w_mla_out': w(ks[15], (DEPTH, N_HEADS * V_DIM, D_MODEL), N_HEADS * V_DIM),
        'w_out': w(ks[16], (DEPTH, D_MODEL, D_MODEL), D_MODEL),
        'ffn_norm_g': gain(ks[17], (DEPTH, D_MODEL)),
        'peer_wq': w(ks[18], (DEPTH, D_MODEL, PEER_HEADS * PEER_QDIM), D_MODEL),
        'peer_keys': w(ks[19], (DEPTH, PEER_HEADS, 2, N_KEYS, PEER_HALF), PEER_HALF),
        'peer_u': w(ks[20], (DEPTH, N_EXPERTS, D_MODEL), D_MODEL),
        'peer_v': w(ks[21], (DEPTH, N_EXPERTS, D_MODEL), D_MODEL),
    }


def reference(x, meta_tokens, mix_norm_g, w_in, conv_w, conv_b, conv_ln_g, conv_ln_b,
              w_conv_out, q_a_norm_g, w_uq, kv_a_norm_g, w_ukv, q_norm_g, k_norm_g,
              w_mla_out, w_out, ffn_norm_g, peer_wq, peer_keys, peer_u, peer_v):
    B = x.shape[0]
    meta = jnp.broadcast_to(meta_tokens[None].astype(x.dtype), (B, N_META, D_MODEL))
    h = jnp.concatenate([meta, x], axis=1)
    L = h.shape[1]
    cos, sin = _rope_tables(L)
    chunk = _chunk_ids(L)
    for l in range(DEPTH):
        h = _mixer_block(h, mix_norm_g[l], w_in[l], conv_w[l], conv_b[l], conv_ln_g[l],
                         conv_ln_b[l], w_conv_out[l], q_a_norm_g[l], w_uq[l], kv_a_norm_g[l],
                         w_ukv[l], q_norm_g[l], k_norm_g[l], w_mla_out[l], w_out[l],
                         cos, sin, chunk)
        h = h + _peer(_rmsnorm(h, ffn_norm_g[l]), peer_wq[l], peer_keys[l], peer_u[l], peer_v[l])
    return h[:, N_META:]
```

```python
import functools
import math

import jax
import jax.numpy as jnp
from jax import lax
from jax.experimental import pallas as pl
from jax.experimental.pallas import tpu as pltpu

F32 = jnp.float32
BF16 = jnp.bfloat16

CHUNK = 64
N_META = 16
D_CONV = 512
CONV_WIDTH = 31
N_HEADS = 8
Q_LORA = 256
KV_LORA = 128
NOPE_DIM = 64
ROPE_DIM = 32
QK_DIM = NOPE_DIM + ROPE_DIM
V_DIM = 64
ROPE_THETA = 10000.0
PEER_HEADS = 8
PEER_HALF = 128
N_KEYS = 128
PEER_TOPK = 16
EPS = 1e-6
MASK_VALUE = -1e30

LANES = 128
SUBLANES = 8
VMEM_LIMIT = 48 * 1024 * 1024

HEAD_PAD = LANES
ROPE_LO = NOPE_DIM
ROPE_HALF = ROPE_DIM // 2
META_PAD = LANES
NEG_INF = float("-inf")


def _cparams(*sem):
    return pltpu.CompilerParams(dimension_semantics=sem, vmem_limit_bytes=VMEM_LIMIT)


def _rms(x, g, n):
    ms = jnp.sum(x * x, axis=-1, keepdims=True) * (1.0 / n)
    return x * lax.rsqrt(ms + EPS) * g


_C_CONV = 0
_C_Q = 2 * D_CONV
_C_KV = _C_Q + Q_LORA
_C_ROPE = _C_KV + KV_LORA
_C_GATE = _C_ROPE + HEAD_PAD


def _inproj_kernel(h_ref, g_ref, w_ref, qg_ref, kvg_ref, u_ref, cq_ref, ckv_ref, kr_ref, gate_ref):
    d = h_ref.shape[-1]
    xn = _rms(h_ref[...], g_ref[...], d).astype(BF16)

    def proj(lo, hi):
        return jnp.dot(xn, w_ref[:, lo:hi], preferred_element_type=F32)

    a = proj(_C_CONV, _C_CONV + D_CONV)
    b = proj(_C_CONV + D_CONV, _C_Q)
    u_ref[...] = a * jax.nn.sigmoid(b)
    cq_ref[...] = _rms(proj(_C_Q, _C_KV), qg_ref[...], Q_LORA).astype(BF16)
    ckv_ref[...] = _rms(proj(_C_KV, _C_ROPE), kvg_ref[...], KV_LORA).astype(BF16)
    kr_ref[...] = proj(_C_ROPE, _C_GATE)
    gate_ref[...] = jax.nn.sigmoid(proj(_C_GATE, _C_GATE + 2 * d)).astype(BF16)


def _inproj(h, g, w, qg, kvg, tb):
    t, d = h.shape
    n = w.shape[1]
    row = lambda c: pl.BlockSpec((tb, c), lambda i: (i, 0))
    full = lambda a: pl.BlockSpec(a.shape, lambda i: (0,) * a.ndim)
    return pl.pallas_call(
        _inproj_kernel,
        grid=(t // tb,),
        in_specs=[row(d), full(g), full(w), full(qg), full(kvg)],
        out_specs=[row(D_CONV), row(Q_LORA), row(KV_LORA), row(HEAD_PAD), row(2 * d)],
        out_shape=[jax.ShapeDtypeStruct((t, D_CONV), F32),
                   jax.ShapeDtypeStruct((t, Q_LORA), BF16),
                   jax.ShapeDtypeStruct((t, KV_LORA), BF16),
                   jax.ShapeDtypeStruct((t, HEAD_PAD), F32),
                   jax.ShapeDtypeStruct((t, 2 * d), BF16)],
        compiler_params=_cparams("parallel"),
        name="inproj",
    )(h, g, w, qg, kvg)


_CONV_PAD = 32
_CONV_TILE = 64


def _conv_kernel(u_ref, pre_ref, cw_ref, cb_ref, lg_ref, lb_ref, wo_ref, gate_ref, o_ref,
                 buf_ref, act_ref, *, rows):
    s = u_ref.shape[1]
    npre = pre_ref.shape[0]
    buf_ref[0:_CONV_PAD - npre, :] = jnp.zeros((_CONV_PAD - npre, D_CONV), F32)
    buf_ref[_CONV_PAD - npre:_CONV_PAD, :] = pre_ref[...]
    buf_ref[_CONV_PAD:, :] = u_ref[0]
    shift = _CONV_PAD - (CONV_WIDTH - 1)

    def tile(i, carry):
        r0 = pl.multiple_of(i * rows, rows)
        acc = jnp.zeros((rows, D_CONV), F32) + cb_ref[...]
        win = buf_ref[pl.ds(r0, rows + _CONV_PAD), :]
        for res in range(SUBLANES):
            taps = [k for k in range(CONV_WIDTH) if (shift + k) % SUBLANES == res]
            shifted = win if res == 0 else win[res:res + rows + _CONV_PAD - SUBLANES, :]
            for k in taps:
                off = (shift + k) // SUBLANES * SUBLANES
                acc = acc + cw_ref[k:k + 1, :] * shifted[off:off + rows, :]
        mu = jnp.mean(acc, axis=-1, keepdims=True)
        xc = acc - mu
        var = jnp.mean(xc * xc, axis=-1, keepdims=True)
        y = xc * lax.rsqrt(var + EPS) * lg_ref[...] + lb_ref[...]
        act_ref[pl.ds(r0, rows), :] = (y * jax.nn.sigmoid(y)).astype(BF16)
        return carry

    lax.fori_loop(0, s // rows, tile, 0)
    y = jnp.dot(act_ref[...], wo_ref[...], preferred_element_type=F32)
    o_ref[0] = (gate_ref[0].astype(F32) * y).astype(BF16)


def _conv_branch(u, pre, cw, cb, lg, lb, wo, gates):
    b, s, _ = u.shape
    d = wo.shape[1]
    rows = min(_CONV_TILE, s)
    full = lambda a: pl.BlockSpec(a.shape, lambda i: (0,) * a.ndim)
    return pl.pallas_call(
        functools.partial(_conv_kernel, rows=rows),
        grid=(b,),
        in_specs=[pl.BlockSpec((1, s, D_CONV), lambda i: (i, 0, 0)), full(pre), full(cw), full(cb),
                  full(lg), full(lb), full(wo), pl.BlockSpec((1, s, d), lambda i: (i, 0, 0))],
        out_specs=pl.BlockSpec((1, s, d), lambda i: (i, 0, 0)),
        out_shape=jax.ShapeDtypeStruct((b, s, d), BF16),
        scratch_shapes=[pltpu.VMEM((_CONV_PAD + s, D_CONV), F32), pltpu.VMEM((s, D_CONV), BF16)],
        compiler_params=_cparams("parallel"),
        name="conv_branch",
    )(u, pre, cw, cb, lg, lb, wo, gates)


def _rope_group(x, cos, sin_signed):
    lane = lax.broadcasted_iota(jnp.int32, x.shape, 1)
    rot = jnp.where(lane < ROPE_LO + ROPE_HALF, pltpu.roll(x, LANES - ROPE_HALF, 1),
                    pltpu.roll(x, ROPE_HALF, 1))
    return x * cos + rot * sin_signed


def _qkv_kernel(cq_ref, ckv_ref, kr_ref, wq_ref, wk_ref, wv_ref, qg_ref, kg_ref, cos_ref, sin_ref,
                q_ref, k_ref, v_ref):
    cos = cos_ref[...]
    sin = sin_ref[...]
    qf = jnp.dot(cq_ref[...], wq_ref[...], preferred_element_type=F32)
    kf = jnp.dot(ckv_ref[...], wk_ref[...], preferred_element_type=F32)
    kr = kr_ref[...]
    scale = QK_DIM ** -0.5
    for hd in range(N_HEADS):
        grp = slice(hd * HEAD_PAD, (hd + 1) * HEAD_PAD)
        qn = _rms(qf[:, grp], qg_ref[...], QK_DIM)
        q_ref[:, grp] = (_rope_group(qn, cos, sin) * scale).astype(BF16)
        kn = _rms(kf[:, grp] + kr, kg_ref[...], QK_DIM)
        k_ref[:, grp] = _rope_group(kn, cos, sin).astype(BF16)
    v_ref[...] = jnp.dot(ckv_ref[...], wv_ref[...], preferred_element_type=F32).astype(BF16)


def _qkv(cq, ckv, kr, wq, wk, wv, qg, kg, cos, sin, tb):
    t = cq.shape[0]
    nrope = cos.shape[0] // tb
    row = lambda c: pl.BlockSpec((tb, c), lambda i: (i, 0))
    full = lambda a: pl.BlockSpec(a.shape, lambda i: (0,) * a.ndim)
    rope = pl.BlockSpec((tb, HEAD_PAD), lambda i: (i % nrope, 0))
    hp = N_HEADS * HEAD_PAD
    return pl.pallas_call(
        _qkv_kernel,
        grid=(t // tb,),
        in_specs=[row(Q_LORA), row(KV_LORA), row(HEAD_PAD), full(wq), full(wk), full(wv),
                  full(qg), full(kg), rope, rope],
        out_specs=[row(hp), row(hp), row(N_HEADS * V_DIM)],
        out_shape=[jax.ShapeDtypeStruct((t, hp), BF16), jax.ShapeDtypeStruct((t, hp), BF16),
                   jax.ShapeDtypeStruct((t, N_HEADS * V_DIM), BF16)],
        compiler_params=_cparams("parallel"),
        name="qkv",
    )(cq, ckv, kr, wq, wk, wv, qg, kg, cos, sin)


def _attn_kernel(*refs, tq, has_real):
    if has_real:
        q_ref, k_ref, v_ref, km_ref, vm_ref, o_ref = refs
    else:
        q_ref, km_ref, vm_ref, o_ref = refs
    qi = pl.program_id(2)
    nt = (((1,), (1,)), ((), ()))
    outs = []
    for hh in range(2):
        grp = slice(hh * HEAD_PAD, (hh + 1) * HEAD_PAD)
        q = q_ref[0, :, grp]
        s = lax.dot_general(q, km_ref[:, grp], nt, preferred_element_type=F32)
        col = lax.broadcasted_iota(jnp.int32, s.shape, 1)
        s = jnp.where(col < N_META, s, MASK_VALUE)
        m = jnp.max(s, axis=-1, keepdims=True)
        p = jnp.exp(s - m)
        l = jnp.sum(p, axis=-1, keepdims=True)
        acc = jnp.dot(p.astype(BF16), vm_ref[...], preferred_element_type=F32)

        if has_real:
            def step(s, v, carry):
                m, l, acc = carry
                m_new = jnp.maximum(m, jnp.max(s, axis=-1, keepdims=True))
                alpha = jnp.exp(m - m_new)
                p = jnp.exp(s - m_new)
                l = alpha * l + jnp.sum(p, axis=-1, keepdims=True)
                acc = alpha * acc + jnp.dot(p.astype(BF16), v, preferred_element_type=F32)
                return m_new, l, acc

            def full_block(kb, carry):
                r0 = pl.multiple_of(kb * tq, tq)
                k = k_ref[0, pl.ds(r0, tq), grp]
                s = lax.dot_general(q, k, nt, preferred_element_type=F32)
                return step(s, v_ref[0, pl.ds(r0, tq), :], carry)

            m, l, acc = lax.fori_loop(0, qi, full_block, (m, l, acc))
            r0 = pl.multiple_of(qi * tq, tq)
            k = k_ref[0, pl.ds(r0, tq), grp]
            s = lax.dot_general(q, k, nt, preferred_element_type=F32)
            row = lax.broadcasted_iota(jnp.int32, s.shape, 0) // CHUNK
            colc = lax.broadcasted_iota(jnp.int32, s.shape, 1) // CHUNK
            s = jnp.where(colc <= row, s, MASK_VALUE)
            m, l, acc = step(s, v_ref[0, pl.ds(r0, tq), :], (m, l, acc))
        outs.append(acc / l)
    lane = lax.broadcasted_iota(jnp.int32, outs[0].shape, 1)
    o_ref[0] = jnp.where(lane < V_DIM, outs[0], outs[1]).astype(BF16)


def _attention(q, k, v, km, vm, tq):
    b, sq, _ = q.shape
    has_real = k is not None
    pair = 2 * HEAD_PAD
    vp = 2 * V_DIM
    qspec = pl.BlockSpec((1, tq, pair), lambda bi, hp, qi: (bi, qi, hp))
    mk = pl.BlockSpec((META_PAD, pair), lambda bi, hp, qi: (0, hp))
    mv = pl.BlockSpec((META_PAD, vp), lambda bi, hp, qi: (0, hp))
    if has_real:
        s = k.shape[1]
        in_specs = [qspec, pl.BlockSpec((1, s, pair), lambda bi, hp, qi: (bi, 0, hp)),
                    pl.BlockSpec((1, s, vp), lambda bi, hp, qi: (bi, 0, hp)), mk, mv]
        args = (q, k, v, km, vm)
    else:
        in_specs = [qspec, mk, mv]
        args = (q, km, vm)
    return pl.pallas_call(
        functools.partial(_attn_kernel, tq=tq, has_real=has_real),
        grid=(b, N_HEADS // 2, sq // tq),
        in_specs=in_specs,
        out_specs=pl.BlockSpec((1, tq, vp), lambda bi, hp, qi: (bi, qi, hp)),
        out_shape=jax.ShapeDtypeStruct((b, sq, N_HEADS * V_DIM), BF16),
        compiler_params=_cparams("parallel", "parallel", "arbitrary"),
        name="attention",
    )(*args)


def _merge_kernel(h_ref, o_ref, gc_ref, g2_ref, wm_ref, wo_ref, out_ref):
    ymla = jnp.dot(o_ref[...], wm_ref[...], preferred_element_type=F32)
    merged = gc_ref[...].astype(F32) + g2_ref[...].astype(F32) * ymla
    out_ref[...] = h_ref[...] + jnp.dot(merged.astype(BF16), wo_ref[...], preferred_element_type=F32)


def _merge(h, o, gc, gates, wm, wo, tb):
    t, d = h.shape
    row = lambda c: pl.BlockSpec((tb, c), lambda i: (i, 0))
    full = lambda a: pl.BlockSpec(a.shape, lambda i: (0,) * a.ndim)
    return pl.pallas_call(
        _merge_kernel,
        grid=(t // tb,),
        in_specs=[row(d), row(N_HEADS * V_DIM), row(d), pl.BlockSpec((tb, d), lambda i: (i, 1)),
                  full(wm), full(wo)],
        out_specs=row(d),
        out_shape=jax.ShapeDtypeStruct((t, d), F32),
        compiler_params=_cparams("parallel"),
        name="merge_out",
    )(h, o, gc, gates, wm, wo)


def _peer_score_kernel(h_ref, g_ref, wq_ref, keys_ref, xt_ref, s_ref):
    d = h_ref.shape[-1]
    xn = _rms(h_ref[...], g_ref[...], d)
    xt = xn.T.astype(BF16)
    xt_ref[...] = xt
    qt = jnp.dot(wq_ref[...], xt, preferred_element_type=F32)
    for g in range(2 * PEER_HEADS):
        qg = qt[g * PEER_HALF:(g + 1) * PEER_HALF, :].astype(BF16)
        s_ref[g] = jnp.dot(keys_ref[g], qg, preferred_element_type=F32)


def _peer_scores(h, g, wqt, keys, tb):
    t, d = h.shape
    full = lambda a: pl.BlockSpec(a.shape, lambda i: (0,) * a.ndim)
    ng = 2 * PEER_HEADS
    return pl.pallas_call(
        _peer_score_kernel,
        grid=(t // tb,),
        in_specs=[pl.BlockSpec((tb, d), lambda i: (i, 0)), full(g), full(wqt), full(keys)],
        out_specs=[pl.BlockSpec((d, tb), lambda i: (0, i)),
                   pl.BlockSpec((ng, N_KEYS, tb), lambda i: (0, 0, i))],
        out_shape=[jax.ShapeDtypeStruct((d, t), BF16), jax.ShapeDtypeStruct((ng, N_KEYS, t), F32)],
        compiler_params=_cparams("parallel"),
        name="peer_scores",
    )(h, g, wqt, keys)


def _extract_topk(vals, pos, sv_ref):
    big = float(vals.shape[0] * vals.shape[0])

    def body(a, carry):
        cur, rank = carry
        m = jnp.max(cur, axis=0, keepdims=True)
        first = jnp.min(jnp.where(cur == m, pos, big), axis=0, keepdims=True)
        hit = pos == first
        if sv_ref is not None:
            sv_ref[pl.ds(a, 1), :] = m
        return jnp.where(hit, NEG_INF, cur), jnp.where(hit, a.astype(F32), rank)

    init = (vals, jnp.full(vals.shape, float(PEER_TOPK), F32))
    return lax.fori_loop(0, PEER_TOPK, body, init)[1]


def _peer_topk_kernel(s_ref, r2_ref, cnt_ref, e1_ref, e2_ref, sv1_ref, sv2_ref):
    lanes = s_ref.shape[-1]
    k = PEER_TOPK
    key_pos = lax.broadcasted_iota(jnp.int32, (N_KEYS, lanes), 0).astype(F32)
    crow = lax.broadcasted_iota(jnp.int32, (k * k, lanes), 0)
    cand_pos = ((crow % k) * k + crow // k).astype(F32)

    def head(hd, carry):
        s1 = s_ref[2 * hd]
        s2 = s_ref[2 * hd + 1]
        r1 = _extract_topk(s1, key_pos, sv1_ref)
        r2 = _extract_topk(s2, key_pos, sv2_ref)
        sv1 = sv1_ref[...]
        sv2 = sv2_ref[...]
        cand = jnp.concatenate([sv1 + sv2[b:b + 1, :] for b in range(k)], axis=0)
        sel = _extract_topk(cand, cand_pos, None) < float(k)
        t1 = jnp.exp(sv1 - sv1[0:1, :])
        t2 = jnp.exp(sv2 - sv2[0:1, :])
        z = jnp.zeros((1, lanes), F32)
        n_a = jnp.zeros((k, lanes), F32)
        for b in range(k):
            sb = sel[b * k:(b + 1) * k, :]
            z = z + jnp.sum(jnp.where(sb, t1 * t2[b:b + 1, :], 0.0), axis=0, keepdims=True)
            n_a = n_a + jnp.where(sb, 1.0, 0.0)
        cnt = jnp.zeros((N_KEYS, lanes), F32)
        for a in range(k):
            cnt = jnp.where(r1 == float(a), n_a[a:a + 1, :], cnt)
        r2_ref[hd] = r2
        cnt_ref[hd] = cnt
        e1_ref[hd] = jnp.exp(s1 - sv1[0:1, :])
        e2_ref[hd] = jnp.exp(s2 - sv2[0:1, :]) / z
        return carry

    lax.fori_loop(0, PEER_HEADS, head, 0)


def _peer_topk(s, lanes):
    ng, nk, t = s.shape
    spec = pl.BlockSpec((PEER_HEADS, nk, lanes), lambda i: (0, 0, i))
    shp = jax.ShapeDtypeStruct((PEER_HEADS, nk, t), F32)
    return pl.pallas_call(
        _peer_topk_kernel,
        grid=(t // lanes,),
        in_specs=[pl.BlockSpec((ng, nk, lanes), lambda i: (0, 0, i))],
        out_specs=[spec, spec, spec, spec],
        out_shape=[shp, shp, shp, shp],
        scratch_shapes=[pltpu.VMEM((PEER_TOPK, lanes), F32), pltpu.VMEM((PEER_TOPK, lanes), F32)],
        compiler_params=_cparams("parallel"),
        name="peer_topk",
    )(s)


def _gelu_tanh(x):
    c = math.sqrt(2.0 / math.pi)
    return 0.5 * x * (1.0 + jnp.tanh(c * (x + 0.044715 * (x * x * x))))


def _peer_dense_kernel(h_ref, xt_ref, u_ref, vt_ref, r2_ref, cnt_ref, e1_ref, e2_ref, o_ref, acc_ref,
                       *, rows_per_step):
    e = pl.program_id(1)
    lanes = xt_ref.shape[-1]

    @pl.when(e == 0)
    def _():
        acc_ref[...] = jnp.zeros_like(acc_ref)

    a = jnp.dot(u_ref[...], xt_ref[...], preferred_element_type=F32)
    ws = []
    for ii in range(rows_per_step):
        i = e * rows_per_step + ii
        w = jnp.zeros((N_KEYS, lanes), F32)
        for hd in range(PEER_HEADS):
            cnt = cnt_ref[hd, pl.ds(i, 1), :]
            e1 = e1_ref[hd, pl.ds(i, 1), :]
            w = w + e1 * jnp.where(r2_ref[hd] < cnt, e2_ref[hd], 0.0)
        ws.append(w)
    act = (_gelu_tanh(a) * jnp.concatenate(ws, axis=0)).astype(BF16)
    acc_ref[...] += jnp.dot(vt_ref[...], act, preferred_element_type=F32)

    @pl.when(e == pl.num_programs(1) - 1)
    def _():
        o_ref[...] = h_ref[...] + acc_ref[...].T


def _peer_dense(h, xt, u, vt, r2, cnt, e1, e2, lanes, rows_per_step):
    t, d = h.shape
    ne = u.shape[0]
    eb = rows_per_step * N_KEYS
    tok = pl.BlockSpec((PEER_HEADS, N_KEYS, lanes), lambda i, e: (0, 0, i))
    return pl.pallas_call(
        functools.partial(_peer_dense_kernel, rows_per_step=rows_per_step),
        grid=(t // lanes, ne // eb),
        in_specs=[pl.BlockSpec((lanes, d), lambda i, e: (i, 0)),
                  pl.BlockSpec((d, lanes), lambda i, e: (0, i)),
                  pl.BlockSpec((eb, d), lambda i, e: (e, 0)),
                  pl.BlockSpec((d, eb), lambda i, e: (0, e)),
                  tok, tok, tok, tok],
        out_specs=pl.BlockSpec((lanes, d), lambda i, e: (i, 0)),
        out_shape=jax.ShapeDtypeStruct((t, d), F32),
        scratch_shapes=[pltpu.VMEM((d, lanes), F32)],
        compiler_params=_cparams("parallel", "arbitrary"),
        name="peer_dense",
    )(h, xt, u, vt, r2, cnt, e1, e2)


def _pad_groups(w, real, n_groups):
    k = w.shape[0]
    w = w.reshape(k, n_groups, real)
    return jnp.pad(w, ((0, 0), (0, 0), (0, HEAD_PAD - real))).reshape(k, n_groups * HEAD_PAD)


def _layer_params(l, w_in, w_conv_out, w_uq, w_ukv, q_norm_g, k_norm_g, w_mla_out, w_out, peer_wq,
                  peer_keys, peer_u, peer_v):
    d = w_in.shape[1]
    wi = w_in[l]
    o0 = 2 * D_CONV
    o1 = o0 + Q_LORA
    o2 = o1 + KV_LORA
    o3 = o2 + ROPE_DIM
    rope_cols = jnp.pad(wi[:, o2:o3], ((0, 0), (ROPE_LO, HEAD_PAD - ROPE_LO - ROPE_DIM)))
    w_in_p = jnp.concatenate([wi[:, :o2], rope_cols, wi[:, o3:]], axis=1).astype(BF16)
    wkv = w_ukv[l].reshape(KV_LORA, N_HEADS, NOPE_DIM + V_DIM)
    wk = _pad_groups(wkv[:, :, :NOPE_DIM].reshape(KV_LORA, N_HEADS * NOPE_DIM), NOPE_DIM, N_HEADS)
    wv = wkv[:, :, NOPE_DIM:].reshape(KV_LORA, N_HEADS * V_DIM)
    padg = lambda g: jnp.pad(g, (0, HEAD_PAD - QK_DIM)).reshape(1, HEAD_PAD)
    return dict(
        w_in=w_in_p,
        w_conv_out=w_conv_out[l].astype(BF16),
        wq=_pad_groups(w_uq[l], QK_DIM, N_HEADS).astype(BF16),
        wk=wk.astype(BF16),
        wv=wv.astype(BF16),
        qg=padg(q_norm_g[l]),
        kg=padg(k_norm_g[l]),
        w_mla_out=w_mla_out[l].astype(BF16),
        w_out=w_out[l].astype(BF16),
        peer_wqt=peer_wq[l].T.astype(BF16),
        peer_keys=peer_keys[l].reshape(2 * PEER_HEADS, N_KEYS, PEER_HALF).astype(BF16),
        peer_u=peer_u[l].astype(BF16),
        peer_vt=peer_v[l].T.astype(BF16),
    )


def _rope_tables(length):
    pos = jnp.arange(length, dtype=F32)
    inv = 1.0 / (ROPE_THETA ** (jnp.arange(0, ROPE_DIM, 2, dtype=F32) / ROPE_DIM))
    ang = pos[:, None] * inv[None, :]
    cos = jnp.concatenate([jnp.cos(ang), jnp.cos(ang)], axis=-1)
    sin = jnp.concatenate([-jnp.sin(ang), jnp.sin(ang)], axis=-1)
    cos = jnp.pad(cos, ((0, 0), (ROPE_LO, 0)), constant_values=1.0)
    cos = jnp.pad(cos, ((0, 0), (0, HEAD_PAD - ROPE_LO - ROPE_DIM)))
    sin = jnp.pad(sin, ((0, 0), (ROPE_LO, HEAD_PAD - ROPE_LO - ROPE_DIM)))
    return cos, sin


def _row(v):
    return v.reshape(1, -1)


def kernel(x, meta_tokens, mix_norm_g, w_in, conv_w, conv_b, conv_ln_g, conv_ln_b, w_conv_out,
           q_a_norm_g, w_uq, kv_a_norm_g, w_ukv, q_norm_g, k_norm_g, w_mla_out, w_out, ffn_norm_g,
           peer_wq, peer_keys, peer_u, peer_v):
    bsz, seq, d = x.shape
    depth = w_in.shape[0]
    t = bsz * seq
    assert meta_tokens.shape[0] == N_META and seq % 256 == 0 and d % LANES == 0
    tb = min(512, seq)
    tq = 256
    topk_lanes = LANES
    dense_lanes = min(512, t)
    rows_per_step = 4

    cos, sin = _rope_tables(N_META + seq)
    h = x.reshape(t, d)
    hm = meta_tokens.astype(x.dtype)
    zero_pre = jnp.zeros((N_META, D_CONV), F32)

    for l in range(depth):
        p = _layer_params(l, w_in, w_conv_out, w_uq, w_ukv, q_norm_g, k_norm_g, w_mla_out, w_out,
                          peer_wq, peer_keys, peer_u, peer_v)
        last = l == depth - 1
        g_mix, qag, kvag = _row(mix_norm_g[l]), _row(q_a_norm_g[l]), _row(kv_a_norm_g[l])
        conv_args = (conv_w[l], _row(conv_b[l]), _row(conv_ln_g[l]), _row(conv_ln_b[l]), p["w_conv_out"])

        um, cqm, ckvm, krm, gatem = _inproj(hm, g_mix, p["w_in"], qag, kvag, N_META)
        qm, km, vm = _qkv(cqm, ckvm, krm, p["wq"], p["wk"], p["wv"], p["qg"], p["kg"],
                          cos[:N_META], sin[:N_META], N_META)
        km_p = jnp.pad(km, ((0, META_PAD - N_META), (0, 0)))
        vm_p = jnp.pad(vm, ((0, META_PAD - N_META), (0, 0)))

        u, cq, ckv, kr, gates = _inproj(h, g_mix, p["w_in"], qag, kvag, tb)
        gc = _conv_branch(u.reshape(bsz, seq, D_CONV), um, *conv_args, gates.reshape(bsz, seq, 2 * d))
        q, k, v = _qkv(cq, ckv, kr, p["wq"], p["wk"], p["wv"], p["qg"], p["kg"],
                       cos[N_META:], sin[N_META:], tb)
        hp = N_HEADS * HEAD_PAD
        o = _attention(q.reshape(bsz, seq, hp), k.reshape(bsz, seq, hp),
                       v.reshape(bsz, seq, N_HEADS * V_DIM), km_p, vm_p, tq)
        h1 = _merge(h, o.reshape(t, N_HEADS * V_DIM), gc.reshape(t, d), gates, p["w_mla_out"],
                    p["w_out"], tb)
        xt, s = _peer_scores(h1, _row(ffn_norm_g[l]), p["peer_wqt"], p["peer_keys"], tb)
        r2, cnt, e1, e2 = _peer_topk(s, topk_lanes)
        h = _peer_dense(h1, xt, p["peer_u"], p["peer_vt"], r2, cnt, e1, e2, dense_lanes, rows_per_step)

        if not last:
            gcm = _conv_branch(um.reshape(1, N_META, D_CONV), zero_pre, *conv_args,
                               gatem.reshape(1, N_META, 2 * d))
            om = _attention(qm.reshape(1, N_META, hp), None, None, km_p, vm_p, N_META)
            hm1 = _merge(hm, om.reshape(N_META, N_HEADS * V_DIM), gcm.reshape(N_META, d), gatem,
                         p["w_mla_out"], p["w_out"], N_META)
            hm1_p = jnp.pad(hm1, ((0, LANES - N_META), (0, 0)))
            xtm, sm = _peer_scores(hm1_p, _row(ffn_norm_g[l]), p["peer_wqt"], p["peer_keys"], LANES)
            r2m, cntm, e1m, e2m = _peer_topk(sm, LANES)
            hm = _peer_dense(hm1_p, xtm, p["peer_u"], p["peer_vt"], r2m, cntm, e1m, e2m, LANES,
                             rows_per_step)[:N_META]

    return h.reshape(bsz, seq, d)
```

```python
import functools
import math

import jax
import jax.numpy as jnp
from jax import lax
from jax.experimental import pallas as pl
from jax.experimental.pallas import tpu as pltpu

F32 = jnp.float32
BF16 = jnp.bfloat16

CHUNK = 64
N_META = 16
D_CONV = 512
CONV_WIDTH = 31
N_HEADS = 8
Q_LORA = 256
KV_LORA = 128
NOPE_DIM = 64
ROPE_DIM = 32
QK_DIM = NOPE_DIM + ROPE_DIM
V_DIM = 64
ROPE_THETA = 10000.0
PEER_HEADS = 8
PEER_HALF = 128
N_KEYS = 128
PEER_TOPK = 16
EPS = 1e-6
MASK_VALUE = -1e30

LANES = 128
SUBLANES = 8
VMEM_LIMIT = 48 * 1024 * 1024

HEAD_PAD = LANES
ROPE_LO = NOPE_DIM
ROPE_HALF = ROPE_DIM // 2
META_PAD = LANES
NEG_INF = float("-inf")


def _cparams(*sem):
    return pltpu.CompilerParams(dimension_semantics=sem, vmem_limit_bytes=VMEM_LIMIT)


def _rms(x, g, n):
    ms = jnp.sum(x * x, axis=-1, keepdims=True) * (1.0 / n)
    return x * lax.rsqrt(ms + EPS) * g


_C_CONV = 0
_C_Q = 2 * D_CONV
_C_KV = _C_Q + Q_LORA
_C_ROPE = _C_KV + KV_LORA
_C_GATE = _C_ROPE + HEAD_PAD


def _inproj_kernel(h_ref, g_ref, w_ref, qg_ref, kvg_ref, u_ref, cq_ref, ckv_ref, kr_ref, gate_ref):
    d = h_ref.shape[-1]
    xn = _rms(h_ref[...], g_ref[...], d).astype(BF16)

    def proj(lo, hi):
        return jnp.dot(xn, w_ref[:, lo:hi], preferred_element_type=F32)

    a = proj(_C_CONV, _C_CONV + D_CONV)
    b = proj(_C_CONV + D_CONV, _C_Q)
    u_ref[...] = a * jax.nn.sigmoid(b)
    cq_ref[...] = _rms(proj(_C_Q, _C_KV), qg_ref[...], Q_LORA).astype(BF16)
    ckv_ref[...] = _rms(proj(_C_KV, _C_ROPE), kvg_ref[...], KV_LORA).astype(BF16)
    kr_ref[...] = proj(_C_ROPE, _C_GATE)
    gate_ref[...] = jax.nn.sigmoid(proj(_C_GATE, _C_GATE + 2 * d)).astype(BF16)


def _inproj(h, g, w, qg, kvg, tb):
    t, d = h.shape
    n = w.shape[1]
    row = lambda c: pl.BlockSpec((tb, c), lambda i: (i, 0))
    full = lambda a: pl.BlockSpec(a.shape, lambda i: (0,) * a.ndim)
    return pl.pallas_call(
        _inproj_kernel,
        grid=(t // tb,),
        in_specs=[row(d), full(g), full(w), full(qg), full(kvg)],
        out_specs=[row(D_CONV), row(Q_LORA), row(KV_LORA), row(HEAD_PAD), row(2 * d)],
        out_shape=[jax.ShapeDtypeStruct((t, D_CONV), F32),
                   jax.ShapeDtypeStruct((t, Q_LORA), BF16),
                   jax.ShapeDtypeStruct((t, KV_LORA), BF16),
                   jax.ShapeDtypeStruct((t, HEAD_PAD), F32),
                   jax.ShapeDtypeStruct((t, 2 * d), BF16)],
        compiler_params=_cparams("parallel"),
        name="inproj",
    )(h, g, w, qg, kvg)


_CONV_PAD = 32
_CONV_TILE = 64


def _conv_kernel(u_ref, pre_ref, cw_ref, cb_ref, lg_ref, lb_ref, wo_ref, gate_ref, o_ref,
                 buf_ref, act_ref, *, rows):
    s = u_ref.shape[1]
    npre = pre_ref.shape[0]
    buf_ref[0:_CONV_PAD - npre, :] = jnp.zeros((_CONV_PAD - npre, D_CONV), F32)
    buf_ref[_CONV_PAD - npre:_CONV_PAD, :] = pre_ref[...]
    buf_ref[_CONV_PAD:, :] = u_ref[0]
    shift = _CONV_PAD - (CONV_WIDTH - 1)

    def tile(i, carry):
        r0 = pl.multiple_of(i * rows, rows)
        acc = jnp.zeros((rows, D_CONV), F32) + cb_ref[...]
        win = buf_ref[pl.ds(r0, rows + _CONV_PAD), :]
        for res in range(SUBLANES):
            taps = [k for k in range(CONV_WIDTH) if (shift + k) % SUBLANES == res]
            shifted = win if res == 0 else win[res:res + rows + _CONV_PAD - SUBLANES, :]
            for k in taps:
                off = (shift + k) // SUBLANES * SUBLANES
                acc = acc + cw_ref[k:k + 1, :] * shifted[off:off + rows, :]
        mu = jnp.mean(acc, axis=-1, keepdims=True)
        xc = acc - mu
        var = jnp.mean(xc * xc, axis=-1, keepdims=True)
        y = xc * lax.rsqrt(var + EPS) * lg_ref[...] + lb_ref[...]
        act_ref[pl.ds(r0, rows), :] = (y * jax.nn.sigmoid(y)).astype(BF16)
        return carry

    lax.fori_loop(0, s // rows, tile, 0)
    y = jnp.dot(act_ref[...], wo_ref[...], preferred_element_type=F32)
    o_ref[0] = (gate_ref[0].astype(F32) * y).astype(BF16)


def _conv_branch(u, pre, cw, cb, lg, lb, wo, gates):
    b, s, _ = u.shape
    d = wo.shape[1]
    rows = min(_CONV_TILE, s)
    full = lambda a: pl.BlockSpec(a.shape, lambda i: (0,) * a.ndim)
    return pl.pallas_call(
        functools.partial(_conv_kernel, rows=rows),
        grid=(b,),
        in_specs=[pl.BlockSpec((1, s, D_CONV), lambda i: (i, 0, 0)), full(pre), full(cw), full(cb),
                  full(lg), full(lb), full(wo), pl.BlockSpec((1, s, d), lambda i: (i, 0, 0))],
        out_specs=pl.BlockSpec((1, s, d), lambda i: (i, 0, 0)),
        out_shape=jax.ShapeDtypeStruct((b, s, d), BF16),
        scratch_shapes=[pltpu.VMEM((_CONV_PAD + s, D_CONV), F32), pltpu.VMEM((s, D_CONV), BF16)],
        compiler_params=_cparams("parallel"),
        name="conv_branch",
    )(u, pre, cw, cb, lg, lb, wo, gates)


def _rope_group(x, cos, sin_signed):
    lane = lax.broadcasted_iota(jnp.int32, x.shape, 1)
    rot = jnp.where(lane < ROPE_LO + ROPE_HALF, pltpu.roll(x, LANES - ROPE_HALF, 1),
                    pltpu.roll(x, ROPE_HALF, 1))
    return x * cos + rot * sin_signed


def _qkv_kernel(cq_ref, ckv_ref, kr_ref, wq_ref, wk_ref, wv_ref, qg_ref, kg_ref, cos_ref, sin_ref,
                q_ref, k_ref, v_ref):
    cos = cos_ref[...]
    sin = sin_ref[...]
    qf = jnp.dot(cq_ref[...], wq_ref[...], preferred_element_type=F32)
    kf = jnp.dot(ckv_ref[...], wk_ref[...], preferred_element_type=F32)
    kr = kr_ref[...]
    scale = QK_DIM ** -0.5
    for hd in range(N_HEADS):
        grp = slice(hd * HEAD_PAD, (hd + 1) * HEAD_PAD)
        qn = _rms(qf[:, grp], qg_ref[...], QK_DIM)
        q_ref[:, grp] = (_rope_group(qn, cos, sin) * scale).astype(BF16)
        kn = _rms(kf[:, grp] + kr, kg_ref[...], QK_DIM)
        k_ref[:, grp] = _rope_group(kn, cos, sin).astype(BF16)
    v_ref[...] = jnp.dot(ckv_ref[...], wv_ref[...], preferred_element_type=F32).astype(BF16)


def _qkv(cq, ckv, kr, wq, wk, wv, qg, kg, cos, sin, tb):
    t = cq.shape[0]
    nrope = cos.shape[0] // tb
    row = lambda c: pl.BlockSpec((tb, c), lambda i: (i, 0))
    full = lambda a: pl.BlockSpec(a.shape, lambda i: (0,) * a.ndim)
    rope = pl.BlockSpec((tb, HEAD_PAD), lambda i: (i % nrope, 0))
    hp = N_HEADS * HEAD_PAD
    return pl.pallas_call(
        _qkv_kernel,
        grid=(t // tb,),
        in_specs=[row(Q_LORA), row(KV_LORA), row(HEAD_PAD), full(wq), full(wk), full(wv),
                  full(qg), full(kg), rope, rope],
        out_specs=[row(hp), row(hp), row(N_HEADS * V_DIM)],
        out_shape=[jax.ShapeDtypeStruct((t, hp), BF16), jax.ShapeDtypeStruct((t, hp), BF16),
                   jax.ShapeDtypeStruct((t, N_HEADS * V_DIM), BF16)],
        compiler_params=_cparams("parallel"),
        name="qkv",
    )(cq, ckv, kr, wq, wk, wv, qg, kg, cos, sin)


def _attn_kernel(*refs, tq, has_real):
    if has_real:
        q_ref, k_ref, v_ref, km_ref, vm_ref, o_ref = refs
    else:
        q_ref, km_ref, vm_ref, o_ref = refs
    qi = pl.program_id(2)
    nt = (((1,), (1,)), ((), ()))
    groups = [slice(hh * HEAD_PAD, (hh + 1) * HEAD_PAD) for hh in range(2)]
    qs = [q_ref[0, :, grp] for grp in groups]

    def step(s, v, carry):
        m, l, acc = carry
        m_new = jnp.maximum(m, jnp.max(s, axis=-1, keepdims=True))
        alpha = jnp.exp(m - m_new)
        p = jnp.exp(s - m_new)
        l = alpha * l + jnp.sum(p, axis=-1, keepdims=True)
        acc = alpha * acc + jnp.dot(p.astype(BF16), v, preferred_element_type=F32)
        return m_new, l, acc

    state = []
    for q, grp in zip(qs, groups):
        s = lax.dot_general(q, km_ref[:, grp], nt, preferred_element_type=F32)
        col = lax.broadcasted_iota(jnp.int32, s.shape, 1)
        s = jnp.where(col < N_META, s, MASK_VALUE)
        m = jnp.max(s, axis=-1, keepdims=True)
        p = jnp.exp(s - m)
        l = jnp.sum(p, axis=-1, keepdims=True)
        state.append((m, l, jnp.dot(p.astype(BF16), vm_ref[...], preferred_element_type=F32)))
    state = tuple(state)

    if has_real:
        def block(r0, carry, masked):
            v = v_ref[0, pl.ds(r0, tq), :]
            out = []
            for q, grp, c in zip(qs, groups, carry):
                s = lax.dot_general(q, k_ref[0, pl.ds(r0, tq), grp], nt, preferred_element_type=F32)
                if masked:
                    row = lax.broadcasted_iota(jnp.int32, s.shape, 0) // CHUNK
                    colc = lax.broadcasted_iota(jnp.int32, s.shape, 1) // CHUNK
                    s = jnp.where(colc <= row, s, MASK_VALUE)
                out.append(step(s, v, c))
            return tuple(out)

        state = lax.fori_loop(0, qi, lambda kb, c: block(pl.multiple_of(kb * tq, tq), c, False), state)
        state = block(pl.multiple_of(qi * tq, tq), state, True)
    outs = [acc / l for (_, l, acc) in state]
    lane = lax.broadcasted_iota(jnp.int32, outs[0].shape, 1)
    o_ref[0] = jnp.where(lane < V_DIM, outs[0], outs[1]).astype(BF16)


def _attention(q, k, v, km, vm, tq):
    b, sq, _ = q.shape
    has_real = k is not None
    pair = 2 * HEAD_PAD
    vp = 2 * V_DIM
    qspec = pl.BlockSpec((1, tq, pair), lambda bi, hp, qi: (bi, qi, hp))
    mk = pl.BlockSpec((META_PAD, pair), lambda bi, hp, qi: (0, hp))
    mv = pl.BlockSpec((META_PAD, vp), lambda bi, hp, qi: (0, hp))
    if has_real:
        s = k.shape[1]
        in_specs = [qspec, pl.BlockSpec((1, s, pair), lambda bi, hp, qi: (bi, 0, hp)),
                    pl.BlockSpec((1, s, vp), lambda bi, hp, qi: (bi, 0, hp)), mk, mv]
        args = (q, k, v, km, vm)
    else:
        in_specs = [qspec, mk, mv]
        args = (q, km, vm)
    return pl.pallas_call(
        functools.partial(_attn_kernel, tq=tq, has_real=has_real),
        grid=(b, N_HEADS // 2, sq // tq),
        in_specs=in_specs,
        out_specs=pl.BlockSpec((1, tq, vp), lambda bi, hp, qi: (bi, qi, hp)),
        out_shape=jax.ShapeDtypeStruct((b, sq, N_HEADS * V_DIM), BF16),
        compiler_params=_cparams("parallel", "parallel", "arbitrary"),
        name="attention",
    )(*args)


def _merge_kernel(h_ref, o_ref, gc_ref, g2_ref, wm_ref, wo_ref, out_ref):
    ymla = jnp.dot(o_ref[...], wm_ref[...], preferred_element_type=F32)
    merged = gc_ref[...].astype(F32) + g2_ref[...].astype(F32) * ymla
    out_ref[...] = h_ref[...] + jnp.dot(merged.astype(BF16), wo_ref[...], preferred_element_type=F32)


def _merge(h, o, gc, gates, wm, wo, tb):
    t, d = h.shape
    row = lambda c: pl.BlockSpec((tb, c), lambda i: (i, 0))
    full = lambda a: pl.BlockSpec(a.shape, lambda i: (0,) * a.ndim)
    return pl.pallas_call(
        _merge_kernel,
        grid=(t // tb,),
        in_specs=[row(d), row(N_HEADS * V_DIM), row(d), pl.BlockSpec((tb, d), lambda i: (i, 1)),
                  full(wm), full(wo)],
        out_specs=row(d),
        out_shape=jax.ShapeDtypeStruct((t, d), F32),
        compiler_params=_cparams("parallel"),
        name="merge_out",
    )(h, o, gc, gates, wm, wo)


def _peer_score_kernel(h_ref, g_ref, wq_ref, keys_ref, xt_ref, s_ref):
    d = h_ref.shape[-1]
    xn = _rms(h_ref[...], g_ref[...], d)
    xt = xn.T.astype(BF16)
    xt_ref[...] = xt
    qt = jnp.dot(wq_ref[...], xt, preferred_element_type=F32)
    for g in range(2 * PEER_HEADS):
        qg = qt[g * PEER_HALF:(g + 1) * PEER_HALF, :].astype(BF16)
        s_ref[g] = jnp.dot(keys_ref[g], qg, preferred_element_type=F32)


def _peer_scores(h, g, wqt, keys, tb):
    t, d = h.shape
    full = lambda a: pl.BlockSpec(a.shape, lambda i: (0,) * a.ndim)
    ng = 2 * PEER_HEADS
    return pl.pallas_call(
        _peer_score_kernel,
        grid=(t // tb,),
        in_specs=[pl.BlockSpec((tb, d), lambda i: (i, 0)), full(g), full(wqt), full(keys)],
        out_specs=[pl.BlockSpec((d, tb), lambda i: (0, i)),
                   pl.BlockSpec((ng, N_KEYS, tb), lambda i: (0, 0, i))],
        out_shape=[jax.ShapeDtypeStruct((d, t), BF16), jax.ShapeDtypeStruct((ng, N_KEYS, t), F32)],
        compiler_params=_cparams("parallel"),
        name="peer_scores",
    )(h, g, wqt, keys)


def _extract_topk(vals, pos, sv_ref):
    big = float(vals.shape[0] * vals.shape[0])

    def body(a, carry):
        cur, rank = carry
        m = jnp.max(cur, axis=0, keepdims=True)
        first = jnp.min(jnp.where(cur == m, pos, big), axis=0, keepdims=True)
        hit = pos == first
        if sv_ref is not None:
            sv_ref[pl.ds(a, 1), :] = m
        return jnp.where(hit, NEG_INF, cur), jnp.where(hit, a.astype(F32), rank)

    init = (vals, jnp.full(vals.shape, float(PEER_TOPK), F32))
    return lax.fori_loop(0, PEER_TOPK, body, init)[1]


def _peer_topk_kernel(s_ref, r2_ref, cnt_ref, e1_ref, e2_ref, sv1_ref, sv2_ref):
    lanes = s_ref.shape[-1]
    k = PEER_TOPK
    key_pos = lax.broadcasted_iota(jnp.int32, (N_KEYS, lanes), 0).astype(F32)
    crow = lax.broadcasted_iota(jnp.int32, (k * k, lanes), 0)
    cand_pos = ((crow % k) * k + crow // k).astype(F32)

    def head(hd, carry):
        s1 = s_ref[2 * hd]
        s2 = s_ref[2 * hd + 1]
        r1 = _extract_topk(s1, key_pos, sv1_ref)
        r2 = _extract_topk(s2, key_pos, sv2_ref)
        sv1 = sv1_ref[...]
        sv2 = sv2_ref[...]
        cand = jnp.concatenate([sv1 + sv2[b:b + 1, :] for b in range(k)], axis=0)
        sel = _extract_topk(cand, cand_pos, None) < float(k)
        t1 = jnp.exp(sv1 - sv1[0:1, :])
        t2 = jnp.exp(sv2 - sv2[0:1, :])
        z = jnp.zeros((1, lanes), F32)
        n_a = jnp.zeros((k, lanes), F32)
        for b in range(k):
            sb = sel[b * k:(b + 1) * k, :]
            z = z + jnp.sum(jnp.where(sb, t1 * t2[b:b + 1, :], 0.0), axis=0, keepdims=True)
            n_a = n_a + jnp.where(sb, 1.0, 0.0)
        cnt = jnp.zeros((N_KEYS, lanes), F32)
        for a in range(k):
            cnt = jnp.where(r1 == float(a), n_a[a:a + 1, :], cnt)
        r2_ref[hd] = r2.astype(BF16)
        cnt_ref[hd] = cnt
        e1_ref[hd] = jnp.exp(s1 - sv1[0:1, :])
        e2_ref[hd] = (jnp.exp(s2 - sv2[0:1, :]) / z).astype(BF16)
        return carry

    lax.fori_loop(0, PEER_HEADS, head, 0)


def _peer_topk(s, lanes):
    ng, nk, t = s.shape
    spec = pl.BlockSpec((PEER_HEADS, nk, lanes), lambda i: (0, 0, i))
    shp = lambda dt: jax.ShapeDtypeStruct((PEER_HEADS, nk, t), dt)
    return pl.pallas_call(
        _peer_topk_kernel,
        grid=(t // lanes,),
        in_specs=[pl.BlockSpec((ng, nk, lanes), lambda i: (0, 0, i))],
        out_specs=[spec, spec, spec, spec],
        out_shape=[shp(BF16), shp(F32), shp(F32), shp(BF16)],
        scratch_shapes=[pltpu.VMEM((PEER_TOPK, lanes), F32), pltpu.VMEM((PEER_TOPK, lanes), F32)],
        compiler_params=_cparams("parallel"),
        name="peer_topk",
    )(s)


def _gelu_tanh(x):
    c = math.sqrt(2.0 / math.pi)
    return 0.5 * x * (1.0 + jnp.tanh(c * (x + 0.044715 * (x * x * x))))


def _peer_dense_kernel(h_ref, xt_ref, u_ref, vt_ref, r2_ref, cnt_ref, e1_ref, e2_ref, o_ref, acc_ref,
                       *, rows_per_step):
    e = pl.program_id(1)
    lanes = xt_ref.shape[-1]

    @pl.when(e == 0)
    def _():
        acc_ref[...] = jnp.zeros_like(acc_ref)

    a = jnp.dot(u_ref[...], xt_ref[...], preferred_element_type=F32)
    zero = jnp.zeros((), BF16)
    acts = []
    for ii in range(rows_per_step):
        i = e * rows_per_step + ii
        w = None
        for hd in range(PEER_HEADS):
            cnt = jnp.broadcast_to(cnt_ref[hd, pl.ds(i, 1), :].astype(BF16), (N_KEYS, lanes))
            e1 = jnp.broadcast_to(e1_ref[hd, pl.ds(i, 1), :].astype(BF16), (N_KEYS, lanes))
            term = e1 * jnp.where(r2_ref[hd] < cnt, e2_ref[hd], zero)
            w = term if w is None else w + term
        acts.append(_gelu_tanh(a[ii * N_KEYS:(ii + 1) * N_KEYS, :]).astype(BF16) * w)
    act = jnp.concatenate(acts, axis=0)
    acc_ref[...] += jnp.dot(vt_ref[...], act, preferred_element_type=F32)

    @pl.when(e == pl.num_programs(1) - 1)
    def _():
        o_ref[...] = h_ref[...] + acc_ref[...].T


def _peer_dense(h, xt, u, vt, r2, cnt, e1, e2, lanes, rows_per_step):
    t, d = h.shape
    ne = u.shape[0]
    eb = rows_per_step * N_KEYS
    tok = pl.BlockSpec((PEER_HEADS, N_KEYS, lanes), lambda i, e: (0, 0, i))
    return pl.pallas_call(
        functools.partial(_peer_dense_kernel, rows_per_step=rows_per_step),
        grid=(t // lanes, ne // eb),
        in_specs=[pl.BlockSpec((lanes, d), lambda i, e: (i, 0)),
                  pl.BlockSpec((d, lanes), lambda i, e: (0, i)),
                  pl.BlockSpec((eb, d), lambda i, e: (e, 0)),
                  pl.BlockSpec((d, eb), lambda i, e: (0, e)),
                  tok, tok, tok, tok],
        out_specs=pl.BlockSpec((lanes, d), lambda i, e: (i, 0)),
        out_shape=jax.ShapeDtypeStruct((t, d), F32),
        scratch_shapes=[pltpu.VMEM((d, lanes), F32)],
        compiler_params=_cparams("parallel", "arbitrary"),
        name="peer_dense",
    )(h, xt, u, vt, r2, cnt, e1, e2)


def _pad_groups(w, real, n_groups):
    k = w.shape[0]
    w = w.reshape(k, n_groups, real)
    return jnp.pad(w, ((0, 0), (0, 0), (0, HEAD_PAD - real))).reshape(k, n_groups * HEAD_PAD)


def _layer_params(l, w_in, w_conv_out, w_uq, w_ukv, q_norm_g, k_norm_g, w_mla_out, w_out, peer_wq,
                  peer_keys, peer_u, peer_v):
    d = w_in.shape[1]
    wi = w_in[l]
    o0 = 2 * D_CONV
    o1 = o0 + Q_LORA
    o2 = o1 + KV_LORA
    o3 = o2 + ROPE_DIM
    rope_cols = jnp.pad(wi[:, o2:o3], ((0, 0), (ROPE_LO, HEAD_PAD - ROPE_LO - ROPE_DIM)))
    w_in_p = jnp.concatenate([wi[:, :o2], rope_cols, wi[:, o3:]], axis=1).astype(BF16)
    wkv = w_ukv[l].reshape(KV_LORA, N_HEADS, NOPE_DIM + V_DIM)
    wk = _pad_groups(wkv[:, :, :NOPE_DIM].reshape(KV_LORA, N_HEADS * NOPE_DIM), NOPE_DIM, N_HEADS)
    wv = wkv[:, :, NOPE_DIM:].reshape(KV_LORA, N_HEADS * V_DIM)
    padg = lambda g: jnp.pad(g, (0, HEAD_PAD - QK_DIM)).reshape(1, HEAD_PAD)
    return dict(
        w_in=w_in_p,
        w_conv_out=w_conv_out[l].astype(BF16),
        wq=_pad_groups(w_uq[l], QK_DIM, N_HEADS).astype(BF16),
        wk=wk.astype(BF16),
        wv=wv.astype(BF16),
        qg=padg(q_norm_g[l]),
        kg=padg(k_norm_g[l]),
        w_mla_out=w_mla_out[l].astype(BF16),
        w_out=w_out[l].astype(BF16),
        peer_wqt=peer_wq[l].T.astype(BF16),
        peer_keys=peer_keys[l].reshape(2 * PEER_HEADS, N_KEYS, PEER_HALF).astype(BF16),
        peer_u=peer_u[l].astype(BF16),
        peer_vt=peer_v[l].T.astype(BF16),
    )


def _rope_tables(length):
    pos = jnp.arange(length, dtype=F32)
    inv = 1.0 / (ROPE_THETA ** (jnp.arange(0, ROPE_DIM, 2, dtype=F32) / ROPE_DIM))
    ang = pos[:, None] * inv[None, :]
    cos = jnp.concatenate([jnp.cos(ang), jnp.cos(ang)], axis=-1)
    sin = jnp.concatenate([-jnp.sin(ang), jnp.sin(ang)], axis=-1)
    cos = jnp.pad(cos, ((0, 0), (ROPE_LO, 0)), constant_values=1.0)
    cos = jnp.pad(cos, ((0, 0), (0, HEAD_PAD - ROPE_LO - ROPE_DIM)))
    sin = jnp.pad(sin, ((0, 0), (ROPE_LO, HEAD_PAD - ROPE_LO - ROPE_DIM)))
    return cos, sin


def _row(v):
    return v.reshape(1, -1)


def kernel(x, meta_tokens, mix_norm_g, w_in, conv_w, conv_b, conv_ln_g, conv_ln_b, w_conv_out,
           q_a_norm_g, w_uq, kv_a_norm_g, w_ukv, q_norm_g, k_norm_g, w_mla_out, w_out, ffn_norm_g,
           peer_wq, peer_keys, peer_u, peer_v):
    bsz, seq, d = x.shape
    depth = w_in.shape[0]
    t = bsz * seq
    assert meta_tokens.shape[0] == N_META and seq % 256 == 0 and d % LANES == 0
    tb = min(512, seq)
    tq = min(512, seq)
    topk_lanes = LANES
    dense_lanes = min(512, t)
    rows_per_step = 16

    cos, sin = _rope_tables(N_META + seq)
    h = x.reshape(t, d)
    hm = meta_tokens.astype(x.dtype)
    zero_pre = jnp.zeros((N_META, D_CONV), F32)

    for l in range(depth):
        p = _layer_params(l, w_in, w_conv_out, w_uq, w_ukv, q_norm_g, k_norm_g, w_mla_out, w_out,
                          peer_wq, peer_keys, peer_u, peer_v)
        last = l == depth - 1
        g_mix, qag, kvag = _row(mix_norm_g[l]), _row(q_a_norm_g[l]), _row(kv_a_norm_g[l])
        conv_args = (conv_w[l], _row(conv_b[l]), _row(conv_ln_g[l]), _row(conv_ln_b[l]), p["w_conv_out"])

        um, cqm, ckvm, krm, gatem = _inproj(hm, g_mix, p["w_in"], qag, kvag, N_META)
        qm, km, vm = _qkv(cqm, ckvm, krm, p["wq"], p["wk"], p["wv"], p["qg"], p["kg"],
                          cos[:N_META], sin[:N_META], N_META)
        km_p = jnp.pad(km, ((0, META_PAD - N_META), (0, 0)))
        vm_p = jnp.pad(vm, ((0, META_PAD - N_META), (0, 0)))

        u, cq, ckv, kr, gates = _inproj(h, g_mix, p["w_in"], qag, kvag, tb)
        gc = _conv_branch(u.reshape(bsz, seq, D_CONV), um, *conv_args, gates.reshape(bsz, seq, 2 * d))
        q, k, v = _qkv(cq, ckv, kr, p["wq"], p["wk"], p["wv"], p["qg"], p["kg"],
                       cos[N_META:], sin[N_META:], tb)
        hp = N_HEADS * HEAD_PAD
        o = _attention(q.reshape(bsz, seq, hp), k.reshape(bsz, seq, hp),
                       v.reshape(bsz, seq, N_HEADS * V_DIM), km_p, vm_p, tq)
        h1 = _merge(h, o.reshape(t, N_HEADS * V_DIM), gc.reshape(t, d), gates, p["w_mla_out"],
                    p["w_out"], tb)
        xt, s = _peer_scores(h1, _row(ffn_norm_g[l]), p["peer_wqt"], p["peer_keys"], tb)
        r2, cnt, e1, e2 = _peer_topk(s, topk_lanes)
        h = _peer_dense(h1, xt, p["peer_u"], p["peer_vt"], r2, cnt, e1, e2, dense_lanes, rows_per_step)

        if not last:
            gcm = _conv_branch(um.reshape(1, N_META, D_CONV), zero_pre, *conv_args,
                               gatem.reshape(1, N_META, 2 * d))
            om = _attention(qm.reshape(1, N_META, hp), None, None, km_p, vm_p, N_META)
            hm1 = _merge(hm, om.reshape(N_META, N_HEADS * V_DIM), gcm.reshape(N_META, d), gatem,
                         p["w_mla_out"], p["w_out"], N_META)
            hm1_p = jnp.pad(hm1, ((0, LANES - N_META), (0, 0)))
            xtm, sm = _peer_scores(hm1_p, _row(ffn_norm_g[l]), p["peer_wqt"], p["peer_keys"], LANES)
            r2m, cntm, e1m, e2m = _peer_topk(sm, LANES)
            hm = _peer_dense(hm1_p, xtm, p["peer_u"], p["peer_vt"], r2m, cntm, e1m, e2m, LANES,
                             rows_per_step)[:N_META]

    return h.reshape(bsz, seq, d)
```

```python
import functools
import math

import jax
import jax.numpy as jnp
from jax import lax
from jax.experimental import pallas as pl
from jax.experimental.pallas import tpu as pltpu

F32 = jnp.float32
BF16 = jnp.bfloat16

CHUNK = 64
N_META = 16
D_CONV = 512
CONV_WIDTH = 31
N_HEADS = 8
Q_LORA = 256
KV_LORA = 128
NOPE_DIM = 64
ROPE_DIM = 32
QK_DIM = NOPE_DIM + ROPE_DIM
V_DIM = 64
ROPE_THETA = 10000.0
PEER_HEADS = 8
PEER_HALF = 128
N_KEYS = 128
PEER_TOPK = 16
EPS = 1e-6
MASK_VALUE = -1e30

LANES = 128
SUBLANES = 8
VMEM_LIMIT = 48 * 1024 * 1024

HEAD_PAD = LANES
ROPE_LO = NOPE_DIM
ROPE_HALF = ROPE_DIM // 2
META_PAD = LANES
NEG_INF = float("-inf")


def _cparams(*sem):
    return pltpu.CompilerParams(dimension_semantics=sem, vmem_limit_bytes=VMEM_LIMIT)


def _rms(x, g, n):
    ms = jnp.sum(x * x, axis=-1, keepdims=True) * (1.0 / n)
    return x * lax.rsqrt(ms + EPS) * g


_C_CONV = 0
_C_Q = 2 * D_CONV
_C_KV = _C_Q + Q_LORA
_C_ROPE = _C_KV + KV_LORA
_C_GATE = _C_ROPE + HEAD_PAD


def _inproj_kernel(h_ref, g_ref, w_ref, qg_ref, kvg_ref, u_ref, cq_ref, ckv_ref, kr_ref, gate_ref):
    d = h_ref.shape[-1]
    xn = _rms(h_ref[...], g_ref[...], d).astype(BF16)

    def proj(lo, hi):
        return jnp.dot(xn, w_ref[:, lo:hi], preferred_element_type=F32)

    a = proj(_C_CONV, _C_CONV + D_CONV)
    b = proj(_C_CONV + D_CONV, _C_Q)
    u_ref[...] = a * jax.nn.sigmoid(b)
    cq_ref[...] = _rms(proj(_C_Q, _C_KV), qg_ref[...], Q_LORA).astype(BF16)
    ckv_ref[...] = _rms(proj(_C_KV, _C_ROPE), kvg_ref[...], KV_LORA).astype(BF16)
    kr_ref[...] = proj(_C_ROPE, _C_GATE)
    gate_ref[...] = jax.nn.sigmoid(proj(_C_GATE, _C_GATE + 2 * d)).astype(BF16)


def _inproj(h, g, w, qg, kvg, tb):
    t, d = h.shape
    n = w.shape[1]
    row = lambda c: pl.BlockSpec((tb, c), lambda i: (i, 0))
    full = lambda a: pl.BlockSpec(a.shape, lambda i: (0,) * a.ndim)
    return pl.pallas_call(
        _inproj_kernel,
        grid=(t // tb,),
        in_specs=[row(d), full(g), full(w), full(qg), full(kvg)],
        out_specs=[row(D_CONV), row(Q_LORA), row(KV_LORA), row(HEAD_PAD), row(2 * d)],
        out_shape=[jax.ShapeDtypeStruct((t, D_CONV), F32),
                   jax.ShapeDtypeStruct((t, Q_LORA), BF16),
                   jax.ShapeDtypeStruct((t, KV_LORA), BF16),
                   jax.ShapeDtypeStruct((t, HEAD_PAD), F32),
                   jax.ShapeDtypeStruct((t, 2 * d), BF16)],
        compiler_params=_cparams("parallel"),
        name="inproj",
    )(h, g, w, qg, kvg)


_CONV_PAD = 32
_CONV_TILE = 64


def _conv_kernel(u_ref, pre_ref, cw_ref, cb_ref, lg_ref, lb_ref, wo_ref, gate_ref, o_ref,
                 buf_ref, act_ref, *, rows):
    s = u_ref.shape[1]
    npre = pre_ref.shape[0]
    buf_ref[0:_CONV_PAD - npre, :] = jnp.zeros((_CONV_PAD - npre, D_CONV), F32)
    buf_ref[_CONV_PAD - npre:_CONV_PAD, :] = pre_ref[...]
    buf_ref[_CONV_PAD:, :] = u_ref[0]
    shift = _CONV_PAD - (CONV_WIDTH - 1)

    def tile(i, carry):
        r0 = pl.multiple_of(i * rows, rows)
        acc = jnp.zeros((rows, D_CONV), F32) + cb_ref[...]
        win = buf_ref[pl.ds(r0, rows + _CONV_PAD), :]
        for res in range(SUBLANES):
            taps = [k for k in range(CONV_WIDTH) if (shift + k) % SUBLANES == res]
            shifted = win if res == 0 else win[res:res + rows + _CONV_PAD - SUBLANES, :]
            for k in taps:
                off = (shift + k) // SUBLANES * SUBLANES
                acc = acc + cw_ref[k:k + 1, :] * shifted[off:off + rows, :]
        mu = jnp.mean(acc, axis=-1, keepdims=True)
        xc = acc - mu
        var = jnp.mean(xc * xc, axis=-1, keepdims=True)
        y = xc * lax.rsqrt(var + EPS) * lg_ref[...] + lb_ref[...]
        act_ref[pl.ds(r0, rows), :] = (y * jax.nn.sigmoid(y)).astype(BF16)
        return carry

    lax.fori_loop(0, s // rows, tile, 0)
    y = jnp.dot(act_ref[...], wo_ref[...], preferred_element_type=F32)
    o_ref[0] = (gate_ref[0].astype(F32) * y).astype(BF16)


def _conv_branch(u, pre, cw, cb, lg, lb, wo, gates):
    b, s, _ = u.shape
    d = wo.shape[1]
    rows = min(_CONV_TILE, s)
    full = lambda a: pl.BlockSpec(a.shape, lambda i: (0,) * a.ndim)
    return pl.pallas_call(
        functools.partial(_conv_kernel, rows=rows),
        grid=(b,),
        in_specs=[pl.BlockSpec((1, s, D_CONV), lambda i: (i, 0, 0)), full(pre), full(cw), full(cb),
                  full(lg), full(lb), full(wo), pl.BlockSpec((1, s, d), lambda i: (i, 0, 0))],
        out_specs=pl.BlockSpec((1, s, d), lambda i: (i, 0, 0)),
        out_shape=jax.ShapeDtypeStruct((b, s, d), BF16),
        scratch_shapes=[pltpu.VMEM((_CONV_PAD + s, D_CONV), F32), pltpu.VMEM((s, D_CONV), BF16)],
        compiler_params=_cparams("parallel"),
        name="conv_branch",
    )(u, pre, cw, cb, lg, lb, wo, gates)


def _rope_group(x, cos, sin_signed):
    lane = lax.broadcasted_iota(jnp.int32, x.shape, 1)
    rot = jnp.where(lane < ROPE_LO + ROPE_HALF, pltpu.roll(x, LANES - ROPE_HALF, 1),
                    pltpu.roll(x, ROPE_HALF, 1))
    return x * cos + rot * sin_signed


def _qkv_kernel(cq_ref, ckv_ref, kr_ref, wq_ref, wk_ref, wv_ref, qg_ref, kg_ref, cos_ref, sin_ref,
                q_ref, k_ref, v_ref):
    cos = cos_ref[...]
    sin = sin_ref[...]
    qf = jnp.dot(cq_ref[...], wq_ref[...], preferred_element_type=F32)
    kf = jnp.dot(ckv_ref[...], wk_ref[...], preferred_element_type=F32)
    kr = kr_ref[...]
    scale = QK_DIM ** -0.5
    for hd in range(N_HEADS):
        grp = slice(hd * HEAD_PAD, (hd + 1) * HEAD_PAD)
        qn = _rms(qf[:, grp], qg_ref[...], QK_DIM)
        q_ref[:, grp] = (_rope_group(qn, cos, sin) * scale).astype(BF16)
        kn = _rms(kf[:, grp] + kr, kg_ref[...], QK_DIM)
        k_ref[:, grp] = _rope_group(kn, cos, sin).astype(BF16)
    v_ref[...] = jnp.dot(ckv_ref[...], wv_ref[...], preferred_element_type=F32).astype(BF16)


def _qkv(cq, ckv, kr, wq, wk, wv, qg, kg, cos, sin, tb):
    t = cq.shape[0]
    nrope = cos.shape[0] // tb
    row = lambda c: pl.BlockSpec((tb, c), lambda i: (i, 0))
    full = lambda a: pl.BlockSpec(a.shape, lambda i: (0,) * a.ndim)
    rope = pl.BlockSpec((tb, HEAD_PAD), lambda i: (i % nrope, 0))
    hp = N_HEADS * HEAD_PAD
    return pl.pallas_call(
        _qkv_kernel,
        grid=(t // tb,),
        in_specs=[row(Q_LORA), row(KV_LORA), row(HEAD_PAD), full(wq), full(wk), full(wv),
                  full(qg), full(kg), rope, rope],
        out_specs=[row(hp), row(hp), row(N_HEADS * V_DIM)],
        out_shape=[jax.ShapeDtypeStruct((t, hp), BF16), jax.ShapeDtypeStruct((t, hp), BF16),
                   jax.ShapeDtypeStruct((t, N_HEADS * V_DIM), BF16)],
        compiler_params=_cparams("parallel"),
        name="qkv",
    )(cq, ckv, kr, wq, wk, wv, qg, kg, cos, sin)


def _attn_kernel(*refs, tq, has_real):
    if has_real:
        q_ref, k_ref, v_ref, km_ref, vm_ref, o_ref = refs
    else:
        q_ref, km_ref, vm_ref, o_ref = refs
    qi = pl.program_id(2)
    nt = (((1,), (1,)), ((), ()))
    groups = [slice(hh * HEAD_PAD, (hh + 1) * HEAD_PAD) for hh in range(2)]
    qs = [q_ref[0, :, grp] for grp in groups]

    def step(s, v, carry):
        m, l, acc = carry
        m_new = jnp.maximum(m, jnp.max(s, axis=-1, keepdims=True))
        alpha = jnp.exp(m - m_new)
        p = jnp.exp(s - m_new)
        l = alpha * l + jnp.sum(p, axis=-1, keepdims=True)
        acc = alpha * acc + jnp.dot(p.astype(BF16), v, preferred_element_type=F32)
        return m_new, l, acc

    state = []
    for q, grp in zip(qs, groups):
        s = lax.dot_general(q, km_ref[:, grp], nt, preferred_element_type=F32)
        col = lax.broadcasted_iota(jnp.int32, s.shape, 1)
        s = jnp.where(col < N_META, s, MASK_VALUE)
        m = jnp.max(s, axis=-1, keepdims=True)
        p = jnp.exp(s - m)
        l = jnp.sum(p, axis=-1, keepdims=True)
        state.append((m, l, jnp.dot(p.astype(BF16), vm_ref[...], preferred_element_type=F32)))
    state = tuple(state)

    if has_real:
        def block(r0, carry, masked):
            v = v_ref[0, pl.ds(r0, tq), :]
            out = []
            for q, grp, c in zip(qs, groups, carry):
                s = lax.dot_general(q, k_ref[0, pl.ds(r0, tq), grp], nt, preferred_element_type=F32)
                if masked:
                    row = lax.broadcasted_iota(jnp.int32, s.shape, 0) // CHUNK
                    colc = lax.broadcasted_iota(jnp.int32, s.shape, 1) // CHUNK
                    s = jnp.where(colc <= row, s, MASK_VALUE)
                out.append(step(s, v, c))
            return tuple(out)

        state = lax.fori_loop(0, qi, lambda kb, c: block(pl.multiple_of(kb * tq, tq), c, False), state)
        state = block(pl.multiple_of(qi * tq, tq), state, True)
    outs = [acc / l for (_, l, acc) in state]
    lane = lax.broadcasted_iota(jnp.int32, outs[0].shape, 1)
    o_ref[0] = jnp.where(lane < V_DIM, outs[0], outs[1]).astype(BF16)


def _attention(q, k, v, km, vm, tq):
    b, sq, _ = q.shape
    has_real = k is not None
    pair = 2 * HEAD_PAD
    vp = 2 * V_DIM
    qspec = pl.BlockSpec((1, tq, pair), lambda bi, hp, qi: (bi, qi, hp))
    mk = pl.BlockSpec((META_PAD, pair), lambda bi, hp, qi: (0, hp))
    mv = pl.BlockSpec((META_PAD, vp), lambda bi, hp, qi: (0, hp))
    if has_real:
        s = k.shape[1]
        in_specs = [qspec, pl.BlockSpec((1, s, pair), lambda bi, hp, qi: (bi, 0, hp)),
                    pl.BlockSpec((1, s, vp), lambda bi, hp, qi: (bi, 0, hp)), mk, mv]
        args = (q, k, v, km, vm)
    else:
        in_specs = [qspec, mk, mv]
        args = (q, km, vm)
    return pl.pallas_call(
        functools.partial(_attn_kernel, tq=tq, has_real=has_real),
        grid=(b, N_HEADS // 2, sq // tq),
        in_specs=in_specs,
        out_specs=pl.BlockSpec((1, tq, vp), lambda bi, hp, qi: (bi, qi, hp)),
        out_shape=jax.ShapeDtypeStruct((b, sq, N_HEADS * V_DIM), BF16),
        compiler_params=_cparams("parallel", "parallel", "arbitrary"),
        name="attention",
    )(*args)


def _merge_kernel(h_ref, o_ref, gc_ref, g2_ref, wm_ref, wo_ref, out_ref):
    ymla = jnp.dot(o_ref[...], wm_ref[...], preferred_element_type=F32)
    merged = gc_ref[...].astype(F32) + g2_ref[...].astype(F32) * ymla
    out_ref[...] = h_ref[...] + jnp.dot(merged.astype(BF16), wo_ref[...], preferred_element_type=F32)


def _merge(h, o, gc, gates, wm, wo, tb):
    t, d = h.shape
    row = lambda c: pl.BlockSpec((tb, c), lambda i: (i, 0))
    full = lambda a: pl.BlockSpec(a.shape, lambda i: (0,) * a.ndim)
    return pl.pallas_call(
        _merge_kernel,
        grid=(t // tb,),
        in_specs=[row(d), row(N_HEADS * V_DIM), row(d), pl.BlockSpec((tb, d), lambda i: (i, 1)),
                  full(wm), full(wo)],
        out_specs=row(d),
        out_shape=jax.ShapeDtypeStruct((t, d), F32),
        compiler_params=_cparams("parallel"),
        name="merge_out",
    )(h, o, gc, gates, wm, wo)


def _peer_score_kernel(h_ref, g_ref, wq_ref, keys_ref, xt_ref, s_ref):
    d = h_ref.shape[-1]
    xn = _rms(h_ref[...], g_ref[...], d)
    xt = xn.T.astype(BF16)
    xt_ref[...] = xt
    qt = jnp.dot(wq_ref[...], xt, preferred_element_type=F32)
    for g in range(2 * PEER_HEADS):
        qg = qt[g * PEER_HALF:(g + 1) * PEER_HALF, :].astype(BF16)
        s_ref[g] = jnp.dot(keys_ref[g], qg, preferred_element_type=F32)


def _peer_scores(h, g, wqt, keys, tb):
    t, d = h.shape
    full = lambda a: pl.BlockSpec(a.shape, lambda i: (0,) * a.ndim)
    ng = 2 * PEER_HEADS
    return pl.pallas_call(
        _peer_score_kernel,
        grid=(t // tb,),
        in_specs=[pl.BlockSpec((tb, d), lambda i: (i, 0)), full(g), full(wqt), full(keys)],
        out_specs=[pl.BlockSpec((d, tb), lambda i: (0, i)),
                   pl.BlockSpec((ng, N_KEYS, tb), lambda i: (0, 0, i))],
        out_shape=[jax.ShapeDtypeStruct((d, t), BF16), jax.ShapeDtypeStruct((ng, N_KEYS, t), F32)],
        compiler_params=_cparams("parallel"),
        name="peer_scores",
    )(h, g, wqt, keys)


def _extract_topk(vals, pos, sv_ref):
    big = float(vals.shape[0] * vals.shape[0])

    def body(a, carry):
        cur, rank = carry
        m = jnp.max(cur, axis=0, keepdims=True)
        first = jnp.min(jnp.where(cur == m, pos, big), axis=0, keepdims=True)
        hit = pos == first
        if sv_ref is not None:
            sv_ref[pl.ds(a, 1), :] = m
        return jnp.where(hit, NEG_INF, cur), jnp.where(hit, lax.convert_element_type(a, F32), rank)

    init = (vals, jnp.full(vals.shape, float(PEER_TOPK), F32))
    return lax.fori_loop(0, PEER_TOPK, body, init)[1]


def _topk_exact_tile(s1, s2, sv1_ref, sv2_ref):
    lanes = s1.shape[-1]
    k = PEER_TOPK
    key_pos = lax.broadcasted_iota(jnp.int32, (N_KEYS, lanes), 0).astype(F32)
    crow = lax.broadcasted_iota(jnp.int32, (k * k, lanes), 0)
    cand_pos = ((crow % k) * k + crow // k).astype(F32)
    r1 = _extract_topk(s1, key_pos, sv1_ref)
    r2 = _extract_topk(s2, key_pos, sv2_ref)
    sv1 = sv1_ref[...]
    sv2 = sv2_ref[...]
    cand = jnp.concatenate([sv1 + sv2[b:b + 1, :] for b in range(k)], axis=0)
    sel = _extract_topk(cand, cand_pos, None) < float(k)
    t1 = jnp.exp(sv1 - sv1[0:1, :])
    t2 = jnp.exp(sv2 - sv2[0:1, :])
    z = jnp.zeros((1, lanes), F32)
    n_a = jnp.zeros((k, lanes), F32)
    for b in range(k):
        sb = sel[b * k:(b + 1) * k, :]
        z = z + jnp.sum(jnp.where(sb, t1 * t2[b:b + 1, :], 0.0), axis=0, keepdims=True)
        n_a = n_a + jnp.where(sb, 1.0, 0.0)
    cnt = jnp.zeros((N_KEYS, lanes), F32)
    for a in range(k):
        cnt = jnp.where(r1 == float(a), n_a[a:a + 1, :], cnt)
    return r2, cnt, jnp.exp(s1 - sv1[0:1, :]), jnp.exp(s2 - sv2[0:1, :]) / z


def _peer_topk_small_kernel(s_ref, r2_ref, cnt_ref, e1_ref, e2_ref, sv1_ref, sv2_ref):
    r2, cnt, e1, e2 = _topk_exact_tile(s_ref[0], s_ref[1], sv1_ref, sv2_ref)
    r2_ref[0] = r2.astype(BF16)
    cnt_ref[0] = cnt
    e1_ref[0] = e1
    e2_ref[0] = e2.astype(BF16)


def _batcher_pairs(n):
    pairs = []
    p = 1
    while p < n:
        k = p
        while k >= 1:
            for j in range(k % p, n - k, 2 * k):
                for i in range(min(k, n - j - k)):
                    if (i + j) // (2 * p) == (i + j + k) // (2 * p):
                        pairs.append((i + j, i + j + k))
            k //= 2
        p *= 2
    return pairs


_SORT16 = _batcher_pairs(PEER_TOPK)
_STAIR = [(a, b) for a in range(PEER_TOPK) for b in range(PEER_TOPK) if (a + 1) * (b + 1) <= PEER_TOPK]
_TILES = LANES * SUBLANES


def _top16_desc(load, lo, n):
    k = PEER_TOPK
    if n == k:
        v = [load(lo + i) for i in range(k)]
        for i, j in _SORT16:
            v[i], v[j] = jnp.maximum(v[i], v[j]), jnp.minimum(v[i], v[j])
        return v
    x = _top16_desc(load, lo, n // 2)
    y = _top16_desc(load, lo + n // 2, n // 2)
    c = [jnp.maximum(x[i], y[k - 1 - i]) for i in range(k)]
    d = k // 2
    while d >= 1:
        for i in range(k):
            if i & d == 0:
                c[i], c[i + d] = jnp.maximum(c[i], c[i + d]), jnp.minimum(c[i], c[i + d])
        d //= 2
    return c


def _sublane_transpose(vs):
    sub = lax.broadcasted_iota(jnp.int32, vs[0].shape, 0)
    d = SUBLANES // 2
    while d >= 1:
        low = (sub & d) == 0
        nxt = list(vs)
        for j in range(SUBLANES):
            if j & d == 0:
                x, y = vs[j], vs[j + d]
                nxt[j] = jnp.where(low, x, pltpu.roll(y, d, 0))
                nxt[j + d] = jnp.where(low, pltpu.roll(x, SUBLANES - d, 0), y)
        vs = nxt
        d //= 2
    return vs


def _peer_topk_kernel(s_ref, r2_ref, cnt_ref, e1_ref, e2_ref, slab_ref, sv_ref, u_ref, misc_ref,
                      sv1_ref, sv2_ref):
    k = PEER_TOPK
    inf = float("inf")
    for half in range(2):
        for kt in range(N_KEYS // SUBLANES):
            rows = slice(kt * SUBLANES, (kt + 1) * SUBLANES)
            tiles = [s_ref[half, rows, j * LANES:(j + 1) * LANES] for j in range(SUBLANES)]
            for r, slab in enumerate(_sublane_transpose(tiles)):
                slab_ref[half * N_KEYS + kt * SUBLANES + r] = slab
    sv1 = _top16_desc(lambda i: slab_ref[i], 0, N_KEYS)
    sv2 = _top16_desc(lambda i: slab_ref[i], N_KEYS, N_KEYS)
    for a in range(k):
        sv_ref[a] = sv1[a]
        sv_ref[k + a] = sv2[a]
    tie = jnp.zeros(sv1[0].shape, F32)
    for v in (sv1, sv2):
        for a in range(k - 1):
            tie = jnp.where(v[a] == v[a + 1], 1.0, tie)
    cand = {c: sv1[c[0]] + sv2[c[1]] for c in _STAIR}
    beaten = {c: float((c[0] + 1) * (c[1] + 1) - 1) for c in _STAIR}
    dyn = {c: None for c in _STAIR}
    for x_i, x in enumerate(_STAIR):
        for y in _STAIR[x_i + 1:]:
            if (x[0] < y[0]) == (x[1] < y[1]) or x[0] == y[0] or x[1] == y[1]:
                continue
            g = jnp.where(cand[x] >= cand[y], 1.0, 0.0)
            dyn[y] = g if dyn[y] is None else dyn[y] + g
            beaten[x] += 1.0
            dyn[x] = -g if dyn[x] is None else dyn[x] - g
    sel = {c: (dyn[c] + beaten[c] if dyn[c] is not None else jnp.full(tie.shape, beaten[c])) < float(k)
           for c in _STAIR}
    t1 = [jnp.exp(sv1[a] - sv1[0]) for a in range(k)]
    t2 = [jnp.exp(sv2[b] - sv2[0]) for b in range(k)]
    z = jnp.zeros(tie.shape, F32)
    for c in _STAIR:
        z = z + jnp.where(sel[c], t1[c[0]] * t2[c[1]], 0.0)
    for b in range(k):
        u = jnp.full(tie.shape, inf, F32)
        for a in range(k):
            if (a, b) in sel:
                u = jnp.minimum(u, jnp.where(sel[(a, b)], sv1[a], inf))
        u_ref[b] = u
    misc_ref[0] = sv1[0]
    misc_ref[1] = sv2[0]
    misc_ref[2] = 1.0 / z
    misc_ref[3] = tie

    rows2 = 2 * SUBLANES
    bad = jnp.zeros((1, LANES), F32)
    for tt in range(SUBLANES):
        cols = slice(tt * LANES, (tt + 1) * LANES)
        bc = lambda ref, i: jnp.broadcast_to(ref[i, tt:tt + 1, :], (rows2, LANES))
        svb = [bc(sv_ref, k + b) for b in range(k)]
        ub = [bc(u_ref, b) for b in range(k)]
        c1, c2, zinv, last1 = bc(misc_ref, 0), bc(misc_ref, 1), bc(misc_ref, 2), bc(sv_ref, k - 1)
        n1 = jnp.zeros((rows2, LANES), F32)
        n2 = jnp.zeros((rows2, LANES), F32)
        for kt in range(N_KEYS // rows2):
            rows = slice(kt * rows2, (kt + 1) * rows2)
            x1 = s_ref[0, rows, cols]
            x2 = s_ref[1, rows, cols]
            r = jnp.full((rows2, LANES), float(k), F32)
            c = jnp.full((rows2, LANES), float(k), F32)
            for b in reversed(range(k)):
                r = jnp.where(svb[b] <= x2, float(b), r)
                c = jnp.where(ub[b] > x1, float(b), c)
            r2_ref[0, rows, cols] = r.astype(BF16)
            cnt_ref[0, rows, cols] = c
            e1_ref[0, rows, cols] = jnp.exp(x1 - c1)
            e2_ref[0, rows, cols] = (jnp.exp(x2 - c2) * zinv).astype(BF16)
            n1 = n1 + jnp.where(x1 >= last1, 1.0, 0.0)
            n2 = n2 + jnp.where(r < float(k), 1.0, 0.0)
        bad = (bad + jnp.abs(jnp.sum(n1, axis=0, keepdims=True) - k)
               + jnp.abs(jnp.sum(n2, axis=0, keepdims=True) - k))
    tied = jnp.max(bad) + jnp.max(misc_ref[3]) > 0.0

    @pl.when(tied)
    def _():
        for tt in range(SUBLANES):
            cols = slice(tt * LANES, (tt + 1) * LANES)
            r2, cnt, e1, e2 = _topk_exact_tile(s_ref[0, :, cols], s_ref[1, :, cols], sv1_ref, sv2_ref)
            r2_ref[0, :, cols] = r2.astype(BF16)
            cnt_ref[0, :, cols] = cnt
            e1_ref[0, :, cols] = e1
            e2_ref[0, :, cols] = e2.astype(BF16)


def _peer_topk(s):
    ng, nk, t = s.shape
    fast = t % _TILES == 0
    lanes = _TILES if fast else LANES
    spec = pl.BlockSpec((1, nk, lanes), lambda i, hd: (hd, 0, i))
    shp = lambda dt: jax.ShapeDtypeStruct((PEER_HEADS, nk, t), dt)
    slab = lambda n: pltpu.VMEM((n, SUBLANES, LANES), F32)
    row = pltpu.VMEM((PEER_TOPK, LANES), F32)
    scratch = [slab(2 * N_KEYS), slab(2 * PEER_TOPK), slab(PEER_TOPK), slab(4), row, row] if fast else [row, row]
    return pl.pallas_call(
        _peer_topk_kernel if fast else _peer_topk_small_kernel,
        grid=(t // lanes, PEER_HEADS),
        in_specs=[pl.BlockSpec((2, nk, lanes), lambda i, hd: (hd, 0, i))],
        out_specs=[spec, spec, spec, spec],
        out_shape=[shp(BF16), shp(F32), shp(F32), shp(BF16)],
        scratch_shapes=scratch,
        compiler_params=_cparams("parallel", "parallel"),
        name="peer_topk",
    )(s)


def _gelu_tanh(x):
    c = math.sqrt(2.0 / math.pi)
    return 0.5 * x * (1.0 + jnp.tanh(c * (x + 0.044715 * (x * x * x))))


def _peer_dense_kernel(h_ref, xt_ref, u_ref, vt_ref, r2_ref, cnt_ref, e1_ref, e2_ref, o_ref, acc_ref,
                       *, rows_per_step):
    e = pl.program_id(1)
    lanes = xt_ref.shape[-1]

    @pl.when(e == 0)
    def _():
        acc_ref[...] = jnp.zeros_like(acc_ref)

    a = jnp.dot(u_ref[...], xt_ref[...], preferred_element_type=F32)
    zero = jnp.zeros((), BF16)
    acts = []
    for ii in range(rows_per_step):
        i = e * rows_per_step + ii
        w = None
        for hd in range(PEER_HEADS):
            cnt = jnp.broadcast_to(cnt_ref[hd, pl.ds(i, 1), :].astype(BF16), (N_KEYS, lanes))
            e1 = jnp.broadcast_to(e1_ref[hd, pl.ds(i, 1), :].astype(BF16), (N_KEYS, lanes))
            term = e1 * jnp.where(r2_ref[hd] < cnt, e2_ref[hd], zero)
            w = term if w is None else w + term
        acts.append(_gelu_tanh(a[ii * N_KEYS:(ii + 1) * N_KEYS, :]).astype(BF16) * w)
    act = jnp.concatenate(acts, axis=0)
    acc_ref[...] += jnp.dot(vt_ref[...], act, preferred_element_type=F32)

    @pl.when(e == pl.num_programs(1) - 1)
    def _():
        o_ref[...] = h_ref[...] + acc_ref[...].T


def _peer_dense(h, xt, u, vt, r2, cnt, e1, e2, lanes, rows_per_step):
    t, d = h.shape
    ne = u.shape[0]
    eb = rows_per_step * N_KEYS
    tok = pl.BlockSpec((PEER_HEADS, N_KEYS, lanes), lambda i, e: (0, 0, i))
    return pl.pallas_call(
        functools.partial(_peer_dense_kernel, rows_per_step=rows_per_step),
        grid=(t // lanes, ne // eb),
        in_specs=[pl.BlockSpec((lanes, d), lambda i, e: (i, 0)),
                  pl.BlockSpec((d, lanes), lambda i, e: (0, i)),
                  pl.BlockSpec((eb, d), lambda i, e: (e, 0)),
                  pl.BlockSpec((d, eb), lambda i, e: (0, e)),
                  tok, tok, tok, tok],
        out_specs=pl.BlockSpec((lanes, d), lambda i, e: (i, 0)),
        out_shape=jax.ShapeDtypeStruct((t, d), F32),
        scratch_shapes=[pltpu.VMEM((d, lanes), F32)],
        compiler_params=_cparams("parallel", "arbitrary"),
        name="peer_dense",
    )(h, xt, u, vt, r2, cnt, e1, e2)


def _pad_groups(w, real, n_groups):
    k = w.shape[0]
    w = w.reshape(k, n_groups, real)
    return jnp.pad(w, ((0, 0), (0, 0), (0, HEAD_PAD - real))).reshape(k, n_groups * HEAD_PAD)


def _layer_params(l, w_in, w_conv_out, w_uq, w_ukv, q_norm_g, k_norm_g, w_mla_out, w_out, peer_wq,
                  peer_keys, peer_u, peer_v):
    d = w_in.shape[1]
    wi = w_in[l]
    o0 = 2 * D_CONV
    o1 = o0 + Q_LORA
    o2 = o1 + KV_LORA
    o3 = o2 + ROPE_DIM
    rope_cols = jnp.pad(wi[:, o2:o3], ((0, 0), (ROPE_LO, HEAD_PAD - ROPE_LO - ROPE_DIM)))
    w_in_p = jnp.concatenate([wi[:, :o2], rope_cols, wi[:, o3:]], axis=1).astype(BF16)
    wkv = w_ukv[l].reshape(KV_LORA, N_HEADS, NOPE_DIM + V_DIM)
    wk = _pad_groups(wkv[:, :, :NOPE_DIM].reshape(KV_LORA, N_HEADS * NOPE_DIM), NOPE_DIM, N_HEADS)
    wv = wkv[:, :, NOPE_DIM:].reshape(KV_LORA, N_HEADS * V_DIM)
    padg = lambda g: jnp.pad(g, (0, HEAD_PAD - QK_DIM)).reshape(1, HEAD_PAD)
    return dict(
        w_in=w_in_p,
        w_conv_out=w_conv_out[l].astype(BF16),
        wq=_pad_groups(w_uq[l], QK_DIM, N_HEADS).astype(BF16),
        wk=wk.astype(BF16),
        wv=wv.astype(BF16),
        qg=padg(q_norm_g[l]),
        kg=padg(k_norm_g[l]),
        w_mla_out=w_mla_out[l].astype(BF16),
        w_out=w_out[l].astype(BF16),
        peer_wqt=peer_wq[l].T.astype(BF16),
        peer_keys=peer_keys[l].reshape(2 * PEER_HEADS, N_KEYS, PEER_HALF).astype(BF16),
        peer_u=peer_u[l].astype(BF16),
        peer_vt=peer_v[l].T.astype(BF16),
    )


def _rope_tables(length):
    pos = jnp.arange(length, dtype=F32)
    inv = 1.0 / (ROPE_THETA ** (jnp.arange(0, ROPE_DIM, 2, dtype=F32) / ROPE_DIM))
    ang = pos[:, None] * inv[None, :]
    cos = jnp.concatenate([jnp.cos(ang), jnp.cos(ang)], axis=-1)
    sin = jnp.concatenate([-jnp.sin(ang), jnp.sin(ang)], axis=-1)
    cos = jnp.pad(cos, ((0, 0), (ROPE_LO, 0)), constant_values=1.0)
    cos = jnp.pad(cos, ((0, 0), (0, HEAD_PAD - ROPE_LO - ROPE_DIM)))
    sin = jnp.pad(sin, ((0, 0), (ROPE_LO, HEAD_PAD - ROPE_LO - ROPE_DIM)))
    return cos, sin


def _row(v):
    return v.reshape(1, -1)


def kernel(x, meta_tokens, mix_norm_g, w_in, conv_w, conv_b, conv_ln_g, conv_ln_b, w_conv_out,
           q_a_norm_g, w_uq, kv_a_norm_g, w_ukv, q_norm_g, k_norm_g, w_mla_out, w_out, ffn_norm_g,
           peer_wq, peer_keys, peer_u, peer_v):
    bsz, seq, d = x.shape
    depth = w_in.shape[0]
    t = bsz * seq
    assert meta_tokens.shape[0] == N_META and seq % 256 == 0 and d % LANES == 0
    tb = min(512, seq)
    tq = min(512, seq)
    dense_lanes = min(512, t)
    rows_per_step = 16

    cos, sin = _rope_tables(N_META + seq)
    h = x.reshape(t, d)
    hm = meta_tokens.astype(x.dtype)
    zero_pre = jnp.zeros((N_META, D_CONV), F32)

    for l in range(depth):
        p = _layer_params(l, w_in, w_conv_out, w_uq, w_ukv, q_norm_g, k_norm_g, w_mla_out, w_out,
                          peer_wq, peer_keys, peer_u, peer_v)
        last = l == depth - 1
        g_mix, qag, kvag = _row(mix_norm_g[l]), _row(q_a_norm_g[l]), _row(kv_a_norm_g[l])
        conv_args = (conv_w[l], _row(conv_b[l]), _row(conv_ln_g[l]), _row(conv_ln_b[l]), p["w_conv_out"])

        um, cqm, ckvm, krm, gatem = _inproj(hm, g_mix, p["w_in"], qag, kvag, N_META)
        qm, km, vm = _qkv(cqm, ckvm, krm, p["wq"], p["wk"], p["wv"], p["qg"], p["kg"],
                          cos[:N_META], sin[:N_META], N_META)
        km_p = jnp.pad(km, ((0, META_PAD - N_META), (0, 0)))
        vm_p = jnp.pad(vm, ((0, META_PAD - N_META), (0, 0)))

        u, cq, ckv, kr, gates = _inproj(h, g_mix, p["w_in"], qag, kvag, tb)
        gc = _conv_branch(u.reshape(bsz, seq, D_CONV), um, *conv_args, gates.reshape(bsz, seq, 2 * d))
        q, k, v = _qkv(cq, ckv, kr, p["wq"], p["wk"], p["wv"], p["qg"], p["kg"],
                       cos[N_META:], sin[N_META:], tb)
        hp = N_HEADS * HEAD_PAD
        o = _attention(q.reshape(bsz, seq, hp), k.reshape(bsz, seq, hp),
                       v.reshape(bsz, seq, N_HEADS * V_DIM), km_p, vm_p, tq)
        h1 = _merge(h, o.reshape(t, N_HEADS * V_DIM), gc.reshape(t, d), gates, p["w_mla_out"],
                    p["w_out"], tb)
        xt, s = _peer_scores(h1, _row(ffn_norm_g[l]), p["peer_wqt"], p["peer_keys"], tb)
        r2, cnt, e1, e2 = _peer_topk(s)
        h = _peer_dense(h1, xt, p["peer_u"], p["peer_vt"], r2, cnt, e1, e2, dense_lanes, rows_per_step)

        if not last:
            gcm = _conv_branch(um.reshape(1, N_META, D_CONV), zero_pre, *conv_args,
                               gatem.reshape(1, N_META, 2 * d))
            om = _attention(qm.reshape(1, N_META, hp), None, None, km_p, vm_p, N_META)
            hm1 = _merge(hm, om.reshape(N_META, N_HEADS * V_DIM), gcm.reshape(N_META, d), gatem,
                         p["w_mla_out"], p["w_out"], N_META)
            hm1_p = jnp.pad(hm1, ((0, LANES - N_META), (0, 0)))
            xtm, sm = _peer_scores(hm1_p, _row(ffn_norm_g[l]), p["peer_wqt"], p["peer_keys"], LANES)
            r2m, cntm, e1m, e2m = _peer_topk(sm)
            hm = _peer_dense(hm1_p, xtm, p["peer_u"], p["peer_vt"], r2m, cntm, e1m, e2m, LANES,
                             rows_per_step)[:N_META]

    return h.reshape(bsz, seq, d)
```

```python
import functools
import math

import jax
import jax.numpy as jnp
from jax import lax
from jax.experimental import pallas as pl
from jax.experimental.pallas import tpu as pltpu

F32 = jnp.float32
BF16 = jnp.bfloat16

CHUNK = 64
N_META = 16
D_CONV = 512
CONV_WIDTH = 31
N_HEADS = 8
Q_LORA = 256
KV_LORA = 128
NOPE_DIM = 64
ROPE_DIM = 32
QK_DIM = NOPE_DIM + ROPE_DIM
V_DIM = 64
ROPE_THETA = 10000.0
PEER_HEADS = 8
PEER_HALF = 128
N_KEYS = 128
PEER_TOPK = 16
EPS = 1e-6
MASK_VALUE = -1e30

LANES = 128
SUBLANES = 8
VMEM_LIMIT = 48 * 1024 * 1024

HEAD_PAD = LANES
ROPE_LO = NOPE_DIM
ROPE_HALF = ROPE_DIM // 2
META_PAD = LANES
NEG_INF = float("-inf")


def _cparams(*sem):
    return pltpu.CompilerParams(dimension_semantics=sem, vmem_limit_bytes=VMEM_LIMIT)


def _rms(x, g, n):
    ms = jnp.sum(x * x, axis=-1, keepdims=True) * (1.0 / n)
    return x * lax.rsqrt(ms + EPS) * g


_C_CONV = 0
_C_Q = 2 * D_CONV
_C_KV = _C_Q + Q_LORA
_C_ROPE = _C_KV + KV_LORA
_C_GATE = _C_ROPE + HEAD_PAD


def _inproj_kernel(h_ref, g_ref, w_ref, qg_ref, kvg_ref, u_ref, cq_ref, ckv_ref, kr_ref, gate_ref):
    d = h_ref.shape[-1]
    xn = _rms(h_ref[...], g_ref[...], d).astype(BF16)

    def proj(lo, hi):
        return jnp.dot(xn, w_ref[:, lo:hi], preferred_element_type=F32)

    a = proj(_C_CONV, _C_CONV + D_CONV)
    b = proj(_C_CONV + D_CONV, _C_Q)
    u_ref[...] = a * jax.nn.sigmoid(b)
    cq_ref[...] = _rms(proj(_C_Q, _C_KV), qg_ref[...], Q_LORA).astype(BF16)
    ckv_ref[...] = _rms(proj(_C_KV, _C_ROPE), kvg_ref[...], KV_LORA).astype(BF16)
    kr_ref[...] = proj(_C_ROPE, _C_GATE)
    gate_ref[...] = jax.nn.sigmoid(proj(_C_GATE, _C_GATE + 2 * d)).astype(BF16)


def _inproj(h, g, w, qg, kvg, tb):
    t, d = h.shape
    n = w.shape[1]
    row = lambda c: pl.BlockSpec((tb, c), lambda i: (i, 0))
    full = lambda a: pl.BlockSpec(a.shape, lambda i: (0,) * a.ndim)
    return pl.pallas_call(
        _inproj_kernel,
        grid=(t // tb,),
        in_specs=[row(d), full(g), full(w), full(qg), full(kvg)],
        out_specs=[row(D_CONV), row(Q_LORA), row(KV_LORA), row(HEAD_PAD), row(2 * d)],
        out_shape=[jax.ShapeDtypeStruct((t, D_CONV), F32),
                   jax.ShapeDtypeStruct((t, Q_LORA), BF16),
                   jax.ShapeDtypeStruct((t, KV_LORA), BF16),
                   jax.ShapeDtypeStruct((t, HEAD_PAD), F32),
                   jax.ShapeDtypeStruct((t, 2 * d), BF16)],
        compiler_params=_cparams("parallel"),
        name="inproj",
    )(h, g, w, qg, kvg)


_CONV_PAD = 32
_CONV_TILE = 64


def _conv_kernel(u_ref, pre_ref, cw_ref, cb_ref, lg_ref, lb_ref, wo_ref, gate_ref, o_ref,
                 buf_ref, act_ref, *, rows):
    s = u_ref.shape[1]
    npre = pre_ref.shape[0]
    buf_ref[0:_CONV_PAD - npre, :] = jnp.zeros((_CONV_PAD - npre, D_CONV), F32)
    buf_ref[_CONV_PAD - npre:_CONV_PAD, :] = pre_ref[...]
    buf_ref[_CONV_PAD:, :] = u_ref[0]
    shift = _CONV_PAD - (CONV_WIDTH - 1)

    def tile(i, carry):
        r0 = pl.multiple_of(i * rows, rows)
        acc = jnp.zeros((rows, D_CONV), F32) + cb_ref[...]
        win = buf_ref[pl.ds(r0, rows + _CONV_PAD), :]
        for res in range(SUBLANES):
            taps = [k for k in range(CONV_WIDTH) if (shift + k) % SUBLANES == res]
            shifted = win if res == 0 else win[res:res + rows + _CONV_PAD - SUBLANES, :]
            for k in taps:
                off = (shift + k) // SUBLANES * SUBLANES
                acc = acc + cw_ref[k:k + 1, :] * shifted[off:off + rows, :]
        mu = jnp.mean(acc, axis=-1, keepdims=True)
        xc = acc - mu
        var = jnp.mean(xc * xc, axis=-1, keepdims=True)
        y = xc * lax.rsqrt(var + EPS) * lg_ref[...] + lb_ref[...]
        act_ref[pl.ds(r0, rows), :] = (y * jax.nn.sigmoid(y)).astype(BF16)
        return carry

    lax.fori_loop(0, s // rows, tile, 0)
    y = jnp.dot(act_ref[...], wo_ref[...], preferred_element_type=F32)
    o_ref[0] = (gate_ref[0].astype(F32) * y).astype(BF16)


def _conv_branch(u, pre, cw, cb, lg, lb, wo, gates):
    b, s, _ = u.shape
    d = wo.shape[1]
    rows = min(_CONV_TILE, s)
    full = lambda a: pl.BlockSpec(a.shape, lambda i: (0,) * a.ndim)
    return pl.pallas_call(
        functools.partial(_conv_kernel, rows=rows),
        grid=(b,),
        in_specs=[pl.BlockSpec((1, s, D_CONV), lambda i: (i, 0, 0)), full(pre), full(cw), full(cb),
                  full(lg), full(lb), full(wo), pl.BlockSpec((1, s, d), lambda i: (i, 0, 0))],
        out_specs=pl.BlockSpec((1, s, d), lambda i: (i, 0, 0)),
        out_shape=jax.ShapeDtypeStruct((b, s, d), BF16),
        scratch_shapes=[pltpu.VMEM((_CONV_PAD + s, D_CONV), F32), pltpu.VMEM((s, D_CONV), BF16)],
        compiler_params=_cparams("parallel"),
        name="conv_branch",
    )(u, pre, cw, cb, lg, lb, wo, gates)


def _rope_group(x, cos, sin_signed):
    lane = lax.broadcasted_iota(jnp.int32, x.shape, 1)
    rot = jnp.where(lane < ROPE_LO + ROPE_HALF, pltpu.roll(x, LANES - ROPE_HALF, 1),
                    pltpu.roll(x, ROPE_HALF, 1))
    return x * cos + rot * sin_signed


def _qkv_kernel(cq_ref, ckv_ref, kr_ref, wq_ref, wk_ref, wv_ref, qg_ref, kg_ref, cos_ref, sin_ref,
                q_ref, k_ref, v_ref):
    cos = cos_ref[...]
    sin = sin_ref[...]
    qf = jnp.dot(cq_ref[...], wq_ref[...], preferred_element_type=F32)
    kf = jnp.dot(ckv_ref[...], wk_ref[...], preferred_element_type=F32)
    kr = kr_ref[...]
    scale = QK_DIM ** -0.5 * math.log2(math.e)
    for hd in range(N_HEADS):
        grp = slice(hd * HEAD_PAD, (hd + 1) * HEAD_PAD)
        qn = _rms(qf[:, grp], qg_ref[...], QK_DIM)
        q_ref[:, grp] = (_rope_group(qn, cos, sin) * scale).astype(BF16)
        kn = _rms(kf[:, grp] + kr, kg_ref[...], QK_DIM)
        k_ref[:, grp] = _rope_group(kn, cos, sin).astype(BF16)
    v_ref[...] = jnp.dot(ckv_ref[...], wv_ref[...], preferred_element_type=F32).astype(BF16)


def _qkv(cq, ckv, kr, wq, wk, wv, qg, kg, cos, sin, tb):
    t = cq.shape[0]
    nrope = cos.shape[0] // tb
    row = lambda c: pl.BlockSpec((tb, c), lambda i: (i, 0))
    full = lambda a: pl.BlockSpec(a.shape, lambda i: (0,) * a.ndim)
    rope = pl.BlockSpec((tb, HEAD_PAD), lambda i: (i % nrope, 0))
    hp = N_HEADS * HEAD_PAD
    return pl.pallas_call(
        _qkv_kernel,
        grid=(t // tb,),
        in_specs=[row(Q_LORA), row(KV_LORA), row(HEAD_PAD), full(wq), full(wk), full(wv),
                  full(qg), full(kg), rope, rope],
        out_specs=[row(hp), row(hp), row(N_HEADS * V_DIM)],
        out_shape=[jax.ShapeDtypeStruct((t, hp), BF16), jax.ShapeDtypeStruct((t, hp), BF16),
                   jax.ShapeDtypeStruct((t, N_HEADS * V_DIM), BF16)],
        compiler_params=_cparams("parallel"),
        name="qkv",
    )(cq, ckv, kr, wq, wk, wv, qg, kg, cos, sin)


_ATTN_PAIRS = 2


def _attn_kernel(*refs, tq, has_real):
    if has_real:
        q_ref, k_ref, v_ref, km_ref, vm_ref, vis_ref, o_ref = refs
    else:
        q_ref, km_ref, vm_ref, o_ref = refs
    qi = pl.program_id(2)
    nt = (((1,), (1,)), ((), ()))
    heads = range(2 * _ATTN_PAIRS)
    groups = [slice(hh * HEAD_PAD, (hh + 1) * HEAD_PAD) for hh in heads]
    vcols = [slice((hh // 2) * 2 * V_DIM, (hh // 2 + 1) * 2 * V_DIM) for hh in heads]
    qs = [q_ref[0, :, grp] for grp in groups]

    def step(s, v, carry):
        m, l, acc = carry
        m_new = jnp.maximum(m, jnp.max(s, axis=-1, keepdims=True))
        alpha = jnp.exp2(m - m_new)
        p = jnp.exp2(s - m_new)
        l = alpha * l + jnp.sum(p, axis=-1, keepdims=True)
        acc = alpha * acc + jnp.dot(p.astype(BF16), v, preferred_element_type=F32)
        return m_new, l, acc

    state = []
    for q, grp, vc in zip(qs, groups, vcols):
        s = lax.dot_general(q, km_ref[:, grp], nt, preferred_element_type=F32)
        col = lax.broadcasted_iota(jnp.int32, s.shape, 1)
        s = jnp.where(col < N_META, s, MASK_VALUE)
        m = jnp.max(s, axis=-1, keepdims=True)
        p = jnp.exp2(s - m)
        l = jnp.sum(p, axis=-1, keepdims=True)
        state.append((m, l, jnp.dot(p.astype(BF16), vm_ref[:, vc], preferred_element_type=F32)))
    state = tuple(state)

    if has_real:
        def block(r0, carry, masked):
            out = []
            for q, grp, vc, c in zip(qs, groups, vcols, carry):
                s = lax.dot_general(q, k_ref[0, pl.ds(r0, tq), grp], nt, preferred_element_type=F32)
                if masked:
                    s = jnp.where(vis_ref[...] > 0.0, s, MASK_VALUE)
                out.append(step(s, v_ref[0, pl.ds(r0, tq), vc], c))
            return tuple(out)

        state = lax.fori_loop(0, qi, lambda kb, c: block(pl.multiple_of(kb * tq, tq), c, False), state)
        state = block(pl.multiple_of(qi * tq, tq), state, True)
    outs = [acc / l for (_, l, acc) in state]
    lane = lax.broadcasted_iota(jnp.int32, outs[0].shape, 1)
    pairs = [jnp.where(lane < V_DIM, outs[2 * p], outs[2 * p + 1]) for p in range(_ATTN_PAIRS)]
    o_ref[0] = jnp.concatenate(pairs, axis=-1).astype(BF16)


def _attention(q, k, v, km, vm, tq):
    b, sq, _ = q.shape
    has_real = k is not None
    qw = 2 * _ATTN_PAIRS * HEAD_PAD
    vw = 2 * _ATTN_PAIRS * V_DIM
    qspec = pl.BlockSpec((1, tq, qw), lambda bi, hp, qi: (bi, qi, hp))
    mk = pl.BlockSpec((META_PAD, qw), lambda bi, hp, qi: (0, hp))
    mv = pl.BlockSpec((META_PAD, vw), lambda bi, hp, qi: (0, hp))
    if has_real:
        s = k.shape[1]
        chunk = jnp.arange(tq, dtype=jnp.int32) // CHUNK
        vis = (chunk[None, :] <= chunk[:, None]).astype(F32)
        in_specs = [qspec, pl.BlockSpec((1, s, qw), lambda bi, hp, qi: (bi, 0, hp)),
                    pl.BlockSpec((1, s, vw), lambda bi, hp, qi: (bi, 0, hp)), mk, mv,
                    pl.BlockSpec((tq, tq), lambda bi, hp, qi: (0, 0))]
        args = (q, k, v, km, vm, vis)
    else:
        in_specs = [qspec, mk, mv]
        args = (q, km, vm)
    return pl.pallas_call(
        functools.partial(_attn_kernel, tq=tq, has_real=has_real),
        grid=(b, N_HEADS // (2 * _ATTN_PAIRS), sq // tq),
        in_specs=in_specs,
        out_specs=pl.BlockSpec((1, tq, vw), lambda bi, hp, qi: (bi, qi, hp)),
        out_shape=jax.ShapeDtypeStruct((b, sq, N_HEADS * V_DIM), BF16),
        compiler_params=_cparams("parallel", "parallel", "arbitrary"),
        name="attention",
    )(*args)


def _merge_kernel(h_ref, o_ref, gc_ref, g2_ref, wm_ref, wo_ref, out_ref):
    ymla = jnp.dot(o_ref[...], wm_ref[...], preferred_element_type=F32)
    merged = gc_ref[...].astype(F32) + g2_ref[...].astype(F32) * ymla
    out_ref[...] = h_ref[...] + jnp.dot(merged.astype(BF16), wo_ref[...], preferred_element_type=F32)


def _merge(h, o, gc, gates, wm, wo, tb):
    t, d = h.shape
    row = lambda c: pl.BlockSpec((tb, c), lambda i: (i, 0))
    full = lambda a: pl.BlockSpec(a.shape, lambda i: (0,) * a.ndim)
    return pl.pallas_call(
        _merge_kernel,
        grid=(t // tb,),
        in_specs=[row(d), row(N_HEADS * V_DIM), row(d), pl.BlockSpec((tb, d), lambda i: (i, 1)),
                  full(wm), full(wo)],
        out_specs=row(d),
        out_shape=jax.ShapeDtypeStruct((t, d), F32),
        compiler_params=_cparams("parallel"),
        name="merge_out",
    )(h, o, gc, gates, wm, wo)


def _peer_score_kernel(h_ref, g_ref, wq_ref, keys_ref, xt_ref, s_ref):
    d = h_ref.shape[-1]
    xn = _rms(h_ref[...], g_ref[...], d)
    xt = xn.T.astype(BF16)
    xt_ref[...] = xt
    qt = jnp.dot(wq_ref[...], xt, preferred_element_type=F32)
    for g in range(2 * PEER_HEADS):
        qg = qt[g * PEER_HALF:(g + 1) * PEER_HALF, :].astype(BF16)
        s_ref[g] = jnp.dot(keys_ref[g], qg, preferred_element_type=F32)


def _peer_scores(h, g, wqt, keys, tb):
    t, d = h.shape
    full = lambda a: pl.BlockSpec(a.shape, lambda i: (0,) * a.ndim)
    ng = 2 * PEER_HEADS
    return pl.pallas_call(
        _peer_score_kernel,
        grid=(t // tb,),
        in_specs=[pl.BlockSpec((tb, d), lambda i: (i, 0)), full(g), full(wqt), full(keys)],
        out_specs=[pl.BlockSpec((d, tb), lambda i: (0, i)),
                   pl.BlockSpec((ng, N_KEYS, tb), lambda i: (0, 0, i))],
        out_shape=[jax.ShapeDtypeStruct((d, t), BF16), jax.ShapeDtypeStruct((ng, N_KEYS, t), F32)],
        compiler_params=_cparams("parallel"),
        name="peer_scores",
    )(h, g, wqt, keys)


def _extract_topk(vals, pos, sv_ref):
    big = float(vals.shape[0] * vals.shape[0])

    def body(a, carry):
        cur, rank = carry
        m = jnp.max(cur, axis=0, keepdims=True)
        first = jnp.min(jnp.where(cur == m, pos, big), axis=0, keepdims=True)
        hit = pos == first
        if sv_ref is not None:
            sv_ref[pl.ds(a, 1), :] = m
        return jnp.where(hit, NEG_INF, cur), jnp.where(hit, lax.convert_element_type(a, F32), rank)

    init = (vals, jnp.full(vals.shape, float(PEER_TOPK), F32))
    return lax.fori_loop(0, PEER_TOPK, body, init)[1]


def _topk_exact_tile(s1, s2, sv1_ref, sv2_ref):
    lanes = s1.shape[-1]
    k = PEER_TOPK
    key_pos = lax.broadcasted_iota(jnp.int32, (N_KEYS, lanes), 0).astype(F32)
    crow = lax.broadcasted_iota(jnp.int32, (k * k, lanes), 0)
    cand_pos = ((crow % k) * k + crow // k).astype(F32)
    r1 = _extract_topk(s1, key_pos, sv1_ref)
    r2 = _extract_topk(s2, key_pos, sv2_ref)
    sv1 = sv1_ref[...]
    sv2 = sv2_ref[...]
    cand = jnp.concatenate([sv1 + sv2[b:b + 1, :] for b in range(k)], axis=0)
    sel = _extract_topk(cand, cand_pos, None) < float(k)
    t1 = jnp.exp(sv1 - sv1[0:1, :])
    t2 = jnp.exp(sv2 - sv2[0:1, :])
    z = jnp.zeros((1, lanes), F32)
    n_a = jnp.zeros((k, lanes), F32)
    for b in range(k):
        sb = sel[b * k:(b + 1) * k, :]
        z = z + jnp.sum(jnp.where(sb, t1 * t2[b:b + 1, :], 0.0), axis=0, keepdims=True)
        n_a = n_a + jnp.where(sb, 1.0, 0.0)
    cnt = jnp.zeros((N_KEYS, lanes), F32)
    for a in range(k):
        cnt = jnp.where(r1 == float(a), n_a[a:a + 1, :], cnt)
    return r2, cnt, jnp.exp(s1 - sv1[0:1, :]), jnp.exp(s2 - sv2[0:1, :]) / z


def _dup_bf16(x):
    bits = pltpu.bitcast(x.astype(BF16).astype(F32), jnp.uint32)
    return bits | (bits >> 16)


def _peer_topk_small_kernel(s_ref, r2_ref, cnt_ref, e1_ref, e2_ref, sv1_ref, sv2_ref):
    r2, cnt, e1, e2 = _topk_exact_tile(s_ref[0], s_ref[1], sv1_ref, sv2_ref)
    r2_ref[0] = r2.astype(BF16)
    cnt_ref[0] = _dup_bf16(cnt)
    e1_ref[0] = _dup_bf16(e1)
    e2_ref[0] = e2.astype(BF16)


def _batcher_pairs(n):
    pairs = []
    p = 1
    while p < n:
        k = p
        while k >= 1:
            for j in range(k % p, n - k, 2 * k):
                for i in range(min(k, n - j - k)):
                    if (i + j) // (2 * p) == (i + j + k) // (2 * p):
                        pairs.append((i + j, i + j + k))
            k //= 2
        p *= 2
    return pairs


_SORT16 = _batcher_pairs(PEER_TOPK)
_STAIR = [(a, b) for a in range(PEER_TOPK) for b in range(PEER_TOPK) if (a + 1) * (b + 1) <= PEER_TOPK]
_TILES = LANES * SUBLANES


def _top16_desc(load, lo, n):
    k = PEER_TOPK
    if n == k:
        v = [load(lo + i) for i in range(k)]
        for i, j in _SORT16:
            v[i], v[j] = jnp.maximum(v[i], v[j]), jnp.minimum(v[i], v[j])
        return v
    x = _top16_desc(load, lo, n // 2)
    y = _top16_desc(load, lo + n // 2, n // 2)
    c = [jnp.maximum(x[i], y[k - 1 - i]) for i in range(k)]
    d = k // 2
    while d >= 1:
        for i in range(k):
            if i & d == 0:
                c[i], c[i + d] = jnp.maximum(c[i], c[i + d]), jnp.minimum(c[i], c[i + d])
        d //= 2
    return c


def _sublane_transpose(vs):
    sub = lax.broadcasted_iota(jnp.int32, vs[0].shape, 0)
    d = SUBLANES // 2
    while d >= 1:
        low = (sub & d) == 0
        nxt = list(vs)
        for j in range(SUBLANES):
            if j & d == 0:
                x, y = vs[j], vs[j + d]
                nxt[j] = jnp.where(low, x, pltpu.roll(y, d, 0))
                nxt[j + d] = jnp.where(low, pltpu.roll(x, SUBLANES - d, 0), y)
        vs = nxt
        d //= 2
    return vs


def _peer_topk_kernel(s_ref, r2_ref, cnt_ref, e1_ref, e2_ref, slab_ref, sv_ref, u_ref, misc_ref,
                      sv1_ref, sv2_ref):
    k = PEER_TOPK
    inf = float("inf")
    for half in range(2):
        for kt in range(N_KEYS // SUBLANES):
            rows = slice(kt * SUBLANES, (kt + 1) * SUBLANES)
            tiles = [s_ref[half, rows, j * LANES:(j + 1) * LANES] for j in range(SUBLANES)]
            for r, slab in enumerate(_sublane_transpose(tiles)):
                slab_ref[half * N_KEYS + kt * SUBLANES + r] = slab
    sv1 = _top16_desc(lambda i: slab_ref[i], 0, N_KEYS)
    sv2 = _top16_desc(lambda i: slab_ref[i], N_KEYS, N_KEYS)
    for a in range(k):
        sv_ref[a] = sv1[a]
        sv_ref[k + a] = sv2[a]
    tie = jnp.zeros(sv1[0].shape, F32)
    for v in (sv1, sv2):
        for a in range(k - 1):
            tie = jnp.where(v[a] == v[a + 1], 1.0, tie)
    cand = {c: sv1[c[0]] + sv2[c[1]] for c in _STAIR}
    beaten = {c: float((c[0] + 1) * (c[1] + 1) - 1) for c in _STAIR}
    dyn = {c: None for c in _STAIR}
    for x_i, x in enumerate(_STAIR):
        for y in _STAIR[x_i + 1:]:
            if (x[0] < y[0]) == (x[1] < y[1]) or x[0] == y[0] or x[1] == y[1]:
                continue
            g = jnp.where(cand[x] >= cand[y], 1.0, 0.0)
            dyn[y] = g if dyn[y] is None else dyn[y] + g
            beaten[x] += 1.0
            dyn[x] = -g if dyn[x] is None else dyn[x] - g
    sel = {c: (dyn[c] + beaten[c] if dyn[c] is not None else jnp.full(tie.shape, beaten[c])) < float(k)
           for c in _STAIR}
    t1 = [jnp.exp(sv1[a] - sv1[0]) for a in range(k)]
    t2 = [jnp.exp(sv2[b] - sv2[0]) for b in range(k)]
    z = jnp.zeros(tie.shape, F32)
    for c in _STAIR:
        z = z + jnp.where(sel[c], t1[c[0]] * t2[c[1]], 0.0)
    for b in range(k):
        u = jnp.full(tie.shape, inf, F32)
        for a in range(k):
            if (a, b) in sel:
                u = jnp.minimum(u, jnp.where(sel[(a, b)], sv1[a], inf))
        u_ref[b] = u
    misc_ref[0] = sv1[0]
    misc_ref[1] = sv2[0]
    misc_ref[2] = 1.0 / z
    misc_ref[3] = tie

    rows2 = 2 * SUBLANES
    bad = jnp.zeros((1, LANES), F32)
    for tt in range(SUBLANES):
        cols = slice(tt * LANES, (tt + 1) * LANES)
        bc = lambda ref, i: jnp.broadcast_to(ref[i, tt:tt + 1, :], (rows2, LANES))
        svb = [bc(sv_ref, k + b) for b in range(k)]
        ub = [bc(u_ref, b) for b in range(k)]
        c1, c2, zinv, last1 = bc(misc_ref, 0), bc(misc_ref, 1), bc(misc_ref, 2), bc(sv_ref, k - 1)
        n1 = jnp.zeros((rows2, LANES), F32)
        n2 = jnp.zeros((rows2, LANES), F32)
        for kt in range(N_KEYS // rows2):
            rows = slice(kt * rows2, (kt + 1) * rows2)
            x1 = s_ref[0, rows, cols]
            x2 = s_ref[1, rows, cols]
            r = jnp.full((rows2, LANES), float(k), F32)
            c = jnp.full((rows2, LANES), float(k), F32)
            for b in reversed(range(k)):
                r = jnp.where(svb[b] <= x2, float(b), r)
                c = jnp.where(ub[b] > x1, float(b), c)
            r2_ref[0, rows, cols] = r.astype(BF16)
            cnt_ref[0, rows, cols] = _dup_bf16(c)
            e1_ref[0, rows, cols] = _dup_bf16(jnp.exp(x1 - c1))
            e2_ref[0, rows, cols] = (jnp.exp(x2 - c2) * zinv).astype(BF16)
            n1 = n1 + jnp.where(x1 >= last1, 1.0, 0.0)
            n2 = n2 + jnp.where(r < float(k), 1.0, 0.0)
        bad = (bad + jnp.abs(jnp.sum(n1, axis=0, keepdims=True) - k)
               + jnp.abs(jnp.sum(n2, axis=0, keepdims=True) - k))
    tied = jnp.max(bad) + jnp.max(misc_ref[3]) > 0.0

    @pl.when(tied)
    def _():
        for tt in range(SUBLANES):
            cols = slice(tt * LANES, (tt + 1) * LANES)
            r2, cnt, e1, e2 = _topk_exact_tile(s_ref[0, :, cols], s_ref[1, :, cols], sv1_ref, sv2_ref)
            r2_ref[0, :, cols] = r2.astype(BF16)
            cnt_ref[0, :, cols] = _dup_bf16(cnt)
            e1_ref[0, :, cols] = _dup_bf16(e1)
            e2_ref[0, :, cols] = e2.astype(BF16)


def _peer_topk(s):
    ng, nk, t = s.shape
    fast = t % _TILES == 0
    lanes = _TILES if fast else LANES
    spec = pl.BlockSpec((1, nk, lanes), lambda i, hd: (hd, 0, i))
    shp = lambda dt: jax.ShapeDtypeStruct((PEER_HEADS, nk, t), dt)
    slab = lambda n: pltpu.VMEM((n, SUBLANES, LANES), F32)
    row = pltpu.VMEM((PEER_TOPK, LANES), F32)
    scratch = [slab(2 * N_KEYS), slab(2 * PEER_TOPK), slab(PEER_TOPK), slab(4), row, row] if fast else [row, row]
    return pl.pallas_call(
        _peer_topk_kernel if fast else _peer_topk_small_kernel,
        grid=(t // lanes, PEER_HEADS),
        in_specs=[pl.BlockSpec((2, nk, lanes), lambda i, hd: (hd, 0, i))],
        out_specs=[spec, spec, spec, spec],
        out_shape=[shp(BF16), shp(jnp.uint32), shp(jnp.uint32), shp(BF16)],
        scratch_shapes=scratch,
        compiler_params=_cparams("parallel", "parallel"),
        name="peer_topk",
    )(s)


def _gelu_tanh(x):
    c = -2.0 * math.sqrt(2.0 / math.pi) * math.log2(math.e)
    return x / (1.0 + jnp.exp2(x * (c + (c * 0.044715) * (x * x))))


_DENSE_CHUNK_ROWS = 8


def _peer_dense_kernel(h_ref, xt_ref, u_ref, vt_ref, r2_ref, cnt_ref, e1_ref, e2_ref, o_ref, acc_ref,
                       *, rows_per_step):
    e = pl.program_id(1)
    lanes = xt_ref.shape[-1]
    bf16_rows = 2 * SUBLANES

    @pl.when(e == 0)
    def _():
        acc_ref[...] = jnp.zeros_like(acc_ref)

    ce = _DENSE_CHUNK_ROWS * N_KEYS
    n_chunks = rows_per_step // _DENSE_CHUNK_ROWS
    zero = jnp.zeros((), BF16)

    def scores(c):
        return jnp.dot(u_ref[c * ce:(c + 1) * ce, :], xt_ref[...], preferred_element_type=F32)

    def packed_row(ref, hd, i):
        word = jnp.broadcast_to(ref[hd, pl.ds(i, 1), :], (SUBLANES, lanes))
        return jnp.tile(pltpu.bitcast(word, BF16), (N_KEYS // bf16_rows, 1))

    a_next = scores(0)
    total = None
    for c in range(n_chunks):
        a = a_next
        ws = []
        for ii in range(_DENSE_CHUNK_ROWS):
            i = e * rows_per_step + c * _DENSE_CHUNK_ROWS + ii
            w = None
            for hd in range(PEER_HEADS):
                term = packed_row(e1_ref, hd, i) * jnp.where(r2_ref[hd] < packed_row(cnt_ref, hd, i),
                                                             e2_ref[hd], zero)
                w = term if w is None else w + term
            ws.append(w)
        if c + 1 < n_chunks:
            a_next = scores(c + 1)
        act = _gelu_tanh(a).astype(BF16) * jnp.concatenate(ws, axis=0)
        part = jnp.dot(vt_ref[:, c * ce:(c + 1) * ce], act, preferred_element_type=F32)
        total = part if total is None else part + total
    acc_ref[...] += total

    @pl.when(e == pl.num_programs(1) - 1)
    def _():
        o_ref[...] = h_ref[...] + acc_ref[...].T


def _peer_dense(h, xt, u, vt, r2, cnt, e1, e2, lanes, rows_per_step):
    t, d = h.shape
    ne = u.shape[0]
    eb = rows_per_step * N_KEYS
    tok = pl.BlockSpec((PEER_HEADS, N_KEYS, lanes), lambda i, e: (0, 0, i))
    return pl.pallas_call(
        functools.partial(_peer_dense_kernel, rows_per_step=rows_per_step),
        grid=(t // lanes, ne // eb),
        in_specs=[pl.BlockSpec((lanes, d), lambda i, e: (i, 0)),
                  pl.BlockSpec((d, lanes), lambda i, e: (0, i)),
                  pl.BlockSpec((eb, d), lambda i, e: (e, 0)),
                  pl.BlockSpec((d, eb), lambda i, e: (0, e)),
                  tok, tok, tok, tok],
        out_specs=pl.BlockSpec((lanes, d), lambda i, e: (i, 0)),
        out_shape=jax.ShapeDtypeStruct((t, d), F32),
        scratch_shapes=[pltpu.VMEM((d, lanes), F32)],
        compiler_params=_cparams("parallel", "arbitrary"),
        name="peer_dense",
    )(h, xt, u, vt, r2, cnt, e1, e2)


def _pad_groups(w, real, n_groups):
    k = w.shape[0]
    w = w.reshape(k, n_groups, real)
    return jnp.pad(w, ((0, 0), (0, 0), (0, HEAD_PAD - real))).reshape(k, n_groups * HEAD_PAD)


def _layer_params(l, w_in, w_conv_out, w_uq, w_ukv, q_norm_g, k_norm_g, w_mla_out, w_out, peer_wq,
                  peer_keys, peer_u, peer_v):
    d = w_in.shape[1]
    wi = w_in[l]
    o0 = 2 * D_CONV
    o1 = o0 + Q_LORA
    o2 = o1 + KV_LORA
    o3 = o2 + ROPE_DIM
    rope_cols = jnp.pad(wi[:, o2:o3], ((0, 0), (ROPE_LO, HEAD_PAD - ROPE_LO - ROPE_DIM)))
    w_in_p = jnp.concatenate([wi[:, :o2], rope_cols, wi[:, o3:]], axis=1).astype(BF16)
    wkv = w_ukv[l].reshape(KV_LORA, N_HEADS, NOPE_DIM + V_DIM)
    wk = _pad_groups(wkv[:, :, :NOPE_DIM].reshape(KV_LORA, N_HEADS * NOPE_DIM), NOPE_DIM, N_HEADS)
    wv = wkv[:, :, NOPE_DIM:].reshape(KV_LORA, N_HEADS * V_DIM)
    padg = lambda g: jnp.pad(g, (0, HEAD_PAD - QK_DIM)).reshape(1, HEAD_PAD)
    return dict(
        w_in=w_in_p,
        w_conv_out=w_conv_out[l].astype(BF16),
        wq=_pad_groups(w_uq[l], QK_DIM, N_HEADS).astype(BF16),
        wk=wk.astype(BF16),
        wv=wv.astype(BF16),
        qg=padg(q_norm_g[l]),
        kg=padg(k_norm_g[l]),
        w_mla_out=w_mla_out[l].astype(BF16),
        w_out=w_out[l].astype(BF16),
        peer_wqt=peer_wq[l],
        peer_keys=peer_keys[l].reshape(2 * PEER_HEADS, N_KEYS, PEER_HALF).astype(BF16),
        peer_u=peer_u[l],
        peer_vt=peer_v[l],
    )


def _rope_tables(length):
    pos = jnp.arange(length, dtype=F32)
    inv = 1.0 / (ROPE_THETA ** (jnp.arange(0, ROPE_DIM, 2, dtype=F32) / ROPE_DIM))
    ang = pos[:, None] * inv[None, :]
    cos = jnp.concatenate([jnp.cos(ang), jnp.cos(ang)], axis=-1)
    sin = jnp.concatenate([-jnp.sin(ang), jnp.sin(ang)], axis=-1)
    cos = jnp.pad(cos, ((0, 0), (ROPE_LO, 0)), constant_values=1.0)
    cos = jnp.pad(cos, ((0, 0), (0, HEAD_PAD - ROPE_LO - ROPE_DIM)))
    sin = jnp.pad(sin, ((0, 0), (ROPE_LO, HEAD_PAD - ROPE_LO - ROPE_DIM)))
    return cos, sin


def _row(v):
    return v.reshape(1, -1)


def kernel(x, meta_tokens, mix_norm_g, w_in, conv_w, conv_b, conv_ln_g, conv_ln_b, w_conv_out,
           q_a_norm_g, w_uq, kv_a_norm_g, w_ukv, q_norm_g, k_norm_g, w_mla_out, w_out, ffn_norm_g,
           peer_wq, peer_keys, peer_u, peer_v):
    bsz, seq, d = x.shape
    depth = w_in.shape[0]
    t = bsz * seq
    assert meta_tokens.shape[0] == N_META and seq % 256 == 0 and d % LANES == 0
    tb = min(512, seq)
    tq = min(512, seq)
    dense_lanes = min(512, t)
    rows_per_step = 16

    cos, sin = _rope_tables(N_META + seq)
    h = x.reshape(t, d)
    hm = meta_tokens.astype(x.dtype)
    zero_pre = jnp.zeros((N_META, D_CONV), F32)

    peer_wqt = jnp.swapaxes(peer_wq, 1, 2).astype(BF16)
    peer_ub = peer_u.astype(BF16)
    peer_vt = jnp.swapaxes(peer_v, 1, 2).astype(BF16)

    for l in range(depth):
        p = _layer_params(l, w_in, w_conv_out, w_uq, w_ukv, q_norm_g, k_norm_g, w_mla_out, w_out,
                          peer_wqt, peer_keys, peer_ub, peer_vt)
        last = l == depth - 1
        g_mix, qag, kvag = _row(mix_norm_g[l]), _row(q_a_norm_g[l]), _row(kv_a_norm_g[l])
        conv_args = (conv_w[l], _row(conv_b[l]), _row(conv_ln_g[l]), _row(conv_ln_b[l]), p["w_conv_out"])

        um, cqm, ckvm, krm, gatem = _inproj(hm, g_mix, p["w_in"], qag, kvag, N_META)
        qm, km, vm = _qkv(cqm, ckvm, krm, p["wq"], p["wk"], p["wv"], p["qg"], p["kg"],
                          cos[:N_META], sin[:N_META], N_META)
        km_p = jnp.pad(km, ((0, META_PAD - N_META), (0, 0)))
        vm_p = jnp.pad(vm, ((0, META_PAD - N_META), (0, 0)))

        u, cq, ckv, kr, gates = _inproj(h, g_mix, p["w_in"], qag, kvag, tb)
        gc = _conv_branch(u.reshape(bsz, seq, D_CONV), um, *conv_args, gates.reshape(bsz, seq, 2 * d))
        q, k, v = _qkv(cq, ckv, kr, p["wq"], p["wk"], p["wv"], p["qg"], p["kg"],
                       cos[N_META:], sin[N_META:], tb)
        hp = N_HEADS * HEAD_PAD
        o = _attention(q.reshape(bsz, seq, hp), k.reshape(bsz, seq, hp),
                       v.reshape(bsz, seq, N_HEADS * V_DIM), km_p, vm_p, tq)
        h1 = _merge(h, o.reshape(t, N_HEADS * V_DIM), gc.reshape(t, d), gates, p["w_mla_out"],
                    p["w_out"], tb)
        xt, s = _peer_scores(h1, _row(ffn_norm_g[l]), p["peer_wqt"], p["peer_keys"], tb)
        r2, cnt, e1, e2 = _peer_topk(s)
        h = _peer_dense(h1, xt, p["peer_u"], p["peer_vt"], r2, cnt, e1, e2, dense_lanes, rows_per_step)

        if not last:
            gcm = _conv_branch(um.reshape(1, N_META, D_CONV), zero_pre, *conv_args,
                               gatem.reshape(1, N_META, 2 * d))
            om = _attention(qm.reshape(1, N_META, hp), None, None, km_p, vm_p, N_META)
            hm1 = _merge(hm, om.reshape(N_META, N_HEADS * V_DIM), gcm.reshape(N_META, d), gatem,
                         p["w_mla_out"], p["w_out"], N_META)
            hm1_p = jnp.pad(hm1, ((0, LANES - N_META), (0, 0)))
            xtm, sm = _peer_scores(hm1_p, _row(ffn_norm_g[l]), p["peer_wqt"], p["peer_keys"], LANES)
            r2m, cntm, e1m, e2m = _peer_topk(sm)
            hm = _peer_dense(hm1_p, xtm, p["peer_u"], p["peer_vt"], r2m, cntm, e1m, e2m, LANES,
                             rows_per_step)[:N_META]

    return h.reshape(bsz, seq, d)
```

```python
import functools
import math

import jax
import jax.numpy as jnp
from jax import lax
from jax.experimental import pallas as pl
from jax.experimental.pallas import tpu as pltpu

F32 = jnp.float32
BF16 = jnp.bfloat16

CHUNK = 64
N_META = 16
D_CONV = 512
CONV_WIDTH = 31
N_HEADS = 8
Q_LORA = 256
KV_LORA = 128
NOPE_DIM = 64
ROPE_DIM = 32
QK_DIM = NOPE_DIM + ROPE_DIM
V_DIM = 64
ROPE_THETA = 10000.0
PEER_HEADS = 8
PEER_HALF = 128
N_KEYS = 128
PEER_TOPK = 16
EPS = 1e-6
MASK_VALUE = -1e30

LANES = 128
SUBLANES = 8
VMEM_LIMIT = 48 * 1024 * 1024

HEAD_PAD = LANES
ROPE_LO = NOPE_DIM
ROPE_HALF = ROPE_DIM // 2
META_PAD = LANES
NEG_INF = float("-inf")


def _cparams(*sem):
    return pltpu.CompilerParams(dimension_semantics=sem, vmem_limit_bytes=VMEM_LIMIT)


def _rms(x, g, n, valid=None):
    xs = x if valid is None else jnp.where(valid, x, 0.0)
    ms = jnp.sum(xs * xs, axis=-1, keepdims=True) * (1.0 / n)
    return x * lax.rsqrt(ms + EPS) * g


_C_CONV = 0
_C_Q = 2 * D_CONV
_C_KV = _C_Q + Q_LORA
_C_ROPE = _C_KV + KV_LORA
_C_GATE = _C_ROPE + HEAD_PAD


def _inproj_kernel(h_ref, g_ref, w_ref, qg_ref, kvg_ref, u_ref, cq_ref, ckv_ref, kr_ref, gate_ref):
    d = h_ref.shape[-1]
    xn = _rms(h_ref[...], g_ref[...], d).astype(BF16)

    def proj(lo, hi):
        return jnp.dot(xn, w_ref[:, lo:hi], preferred_element_type=F32)

    a = proj(_C_CONV, _C_CONV + D_CONV)
    b = proj(_C_CONV + D_CONV, _C_Q)
    u_ref[...] = a * jax.nn.sigmoid(b)
    cq_ref[...] = _rms(proj(_C_Q, _C_KV), qg_ref[...], Q_LORA).astype(BF16)
    ckv_ref[...] = _rms(proj(_C_KV, _C_ROPE), kvg_ref[...], KV_LORA).astype(BF16)
    kr_ref[...] = proj(_C_ROPE, _C_GATE)
    gate_ref[...] = jax.nn.sigmoid(proj(_C_GATE, _C_GATE + 2 * d)).astype(BF16)


def _inproj(h, g, w, qg, kvg, tb):
    t, d = h.shape
    n = w.shape[1]
    row = lambda c: pl.BlockSpec((tb, c), lambda i: (i, 0))
    full = lambda a: pl.BlockSpec(a.shape, lambda i: (0,) * a.ndim)
    return pl.pallas_call(
        _inproj_kernel,
        grid=(t // tb,),
        in_specs=[row(d), full(g), full(w), full(qg), full(kvg)],
        out_specs=[row(D_CONV), row(Q_LORA), row(KV_LORA), row(HEAD_PAD), row(2 * d)],
        out_shape=[jax.ShapeDtypeStruct((t, D_CONV), F32),
                   jax.ShapeDtypeStruct((t, Q_LORA), BF16),
                   jax.ShapeDtypeStruct((t, KV_LORA), BF16),
                   jax.ShapeDtypeStruct((t, HEAD_PAD), F32),
                   jax.ShapeDtypeStruct((t, 2 * d), BF16)],
        compiler_params=_cparams("parallel"),
        name="inproj",
    )(h, g, w, qg, kvg)


_CONV_PAD = 32
_CONV_TILE = 64


def _conv_kernel(u_ref, pre_ref, cw_ref, cb_ref, lg_ref, lb_ref, wo_ref, gate_ref, o_ref,
                 buf_ref, act_ref, sh_ref, *, rows):
    s = u_ref.shape[1]
    npre = pre_ref.shape[0]
    buf_ref[0:_CONV_PAD - npre, :] = jnp.zeros((_CONV_PAD - npre, D_CONV), F32)
    buf_ref[_CONV_PAD - npre:_CONV_PAD, :] = pre_ref[...]
    buf_ref[_CONV_PAD:, :] = u_ref[0]
    shift = _CONV_PAD - (CONV_WIDTH - 1)

    def tile(i, carry):
        r0 = pl.multiple_of(i * rows, rows)
        acc = jnp.zeros((rows, D_CONV), F32) + cb_ref[...]
        win = buf_ref[pl.ds(r0, rows + _CONV_PAD), :]
        for res in range(1, SUBLANES):
            sh_ref[res - 1] = win[res:res + rows + _CONV_PAD - SUBLANES, :]
        for k in range(CONV_WIDTH):
            res, off = (shift + k) % SUBLANES, (shift + k) // SUBLANES * SUBLANES
            x = win[off:off + rows, :] if res == 0 else sh_ref[res - 1, off:off + rows, :]
            acc = acc + cw_ref[k:k + 1, :] * x
        mu = jnp.mean(acc, axis=-1, keepdims=True)
        xc = acc - mu
        var = jnp.mean(xc * xc, axis=-1, keepdims=True)
        y = xc * lax.rsqrt(var + EPS) * lg_ref[...] + lb_ref[...]
        act_ref[pl.ds(r0, rows), :] = (y * jax.nn.sigmoid(y)).astype(BF16)
        return carry

    lax.fori_loop(0, s // rows, tile, 0)
    y = jnp.dot(act_ref[...], wo_ref[...], preferred_element_type=F32)
    o_ref[0] = (gate_ref[0].astype(F32) * y).astype(BF16)


def _conv_branch(u, pre, cw, cb, lg, lb, wo, gates):
    b, s, _ = u.shape
    d = wo.shape[1]
    rows = min(_CONV_TILE, s)
    full = lambda a: pl.BlockSpec(a.shape, lambda i: (0,) * a.ndim)
    return pl.pallas_call(
        functools.partial(_conv_kernel, rows=rows),
        grid=(b,),
        in_specs=[pl.BlockSpec((1, s, D_CONV), lambda i: (i, 0, 0)), full(pre), full(cw), full(cb),
                  full(lg), full(lb), full(wo), pl.BlockSpec((1, s, d), lambda i: (i, 0, 0))],
        out_specs=pl.BlockSpec((1, s, d), lambda i: (i, 0, 0)),
        out_shape=jax.ShapeDtypeStruct((b, s, d), BF16),
        scratch_shapes=[pltpu.VMEM((_CONV_PAD + s, D_CONV), F32), pltpu.VMEM((s, D_CONV), BF16),
                        pltpu.VMEM((SUBLANES - 1, rows + _CONV_PAD - SUBLANES, D_CONV), F32)],
        compiler_params=_cparams("parallel"),
        name="conv_branch",
    )(u, pre, cw, cb, lg, lb, wo, gates)


def _rope_group(x, cos, sin_signed):
    return x * cos + pltpu.roll(x, LANES - ROPE_HALF, 1) * sin_signed


def _qkv_kernel(cq_ref, ckv_ref, kr_ref, wq_ref, wk_ref, wv_ref, qg_ref, kg_ref, cos_ref, sin_ref,
                q_ref, k_ref, v_ref):
    cos = cos_ref[...]
    sin = sin_ref[...]
    qf = jnp.dot(cq_ref[...], wq_ref[...], preferred_element_type=F32)
    kf = jnp.dot(ckv_ref[...], wk_ref[...], preferred_element_type=F32)
    kr = kr_ref[...]
    scale = QK_DIM ** -0.5 * math.log2(math.e)
    real = lax.broadcasted_iota(jnp.int32, kr.shape, 1) < QK_DIM
    for hd in range(N_HEADS):
        grp = slice(hd * HEAD_PAD, (hd + 1) * HEAD_PAD)
        qn = _rms(qf[:, grp], qg_ref[...], QK_DIM, real)
        q_ref[:, grp] = (_rope_group(qn, cos, sin) * scale).astype(BF16)
        kn = _rms(kf[:, grp] + kr, kg_ref[...], QK_DIM, real)
        k_ref[:, grp] = _rope_group(kn, cos, sin).astype(BF16)
    v_ref[...] = jnp.dot(ckv_ref[...], wv_ref[...], preferred_element_type=F32).astype(BF16)


def _qkv(cq, ckv, kr, wq, wk, wv, qg, kg, cos, sin, tb):
    t = cq.shape[0]
    nrope = cos.shape[0] // tb
    row = lambda c: pl.BlockSpec((tb, c), lambda i: (i, 0))
    full = lambda a: pl.BlockSpec(a.shape, lambda i: (0,) * a.ndim)
    rope = pl.BlockSpec((tb, HEAD_PAD), lambda i: (i % nrope, 0))
    hp = N_HEADS * HEAD_PAD
    return pl.pallas_call(
        _qkv_kernel,
        grid=(t // tb,),
        in_specs=[row(Q_LORA), row(KV_LORA), row(HEAD_PAD), full(wq), full(wk), full(wv),
                  full(qg), full(kg), rope, rope],
        out_specs=[row(hp), row(hp), row(N_HEADS * V_DIM)],
        out_shape=[jax.ShapeDtypeStruct((t, hp), BF16), jax.ShapeDtypeStruct((t, hp), BF16),
                   jax.ShapeDtypeStruct((t, N_HEADS * V_DIM), BF16)],
        compiler_params=_cparams("parallel"),
        name="qkv",
    )(cq, ckv, kr, wq, wk, wv, qg, kg, cos, sin)


_ATTN_PAIRS = 2


def _attn_kernel(*refs, tq, has_real):
    if has_real:
        q_ref, k_ref, v_ref, km_ref, vm_ref, vis_ref, o_ref = refs
    else:
        q_ref, km_ref, vm_ref, o_ref = refs
    qi = pl.program_id(2)
    nt = (((1,), (1,)), ((), ()))
    heads = range(2 * _ATTN_PAIRS)
    groups = [slice(hh * HEAD_PAD, (hh + 1) * HEAD_PAD) for hh in heads]
    vcols = [slice((hh // 2) * 2 * V_DIM, (hh // 2 + 1) * 2 * V_DIM) for hh in heads]
    qs = [q_ref[0, :, grp] for grp in groups]

    def step(s, v, carry):
        m, l, acc = carry
        m_new = jnp.maximum(m, jnp.max(s, axis=-1, keepdims=True))
        alpha = jnp.exp2(m - m_new)
        p = jnp.exp2(s - m_new)
        l = alpha * l + jnp.sum(p, axis=-1, keepdims=True)
        acc = alpha * acc + jnp.dot(p.astype(BF16), v, preferred_element_type=F32)
        return m_new, l, acc

    state = []
    for q, grp, vc in zip(qs, groups, vcols):
        s = lax.dot_general(q, km_ref[:, grp], nt, preferred_element_type=F32)
        col = lax.broadcasted_iota(jnp.int32, s.shape, 1)
        s = jnp.where(col < N_META, s, MASK_VALUE)
        m = jnp.max(s, axis=-1, keepdims=True)
        p = jnp.exp2(s - m)
        l = jnp.sum(p, axis=-1, keepdims=True)
        state.append((m, l, jnp.dot(p.astype(BF16), vm_ref[:, vc], preferred_element_type=F32)))
    state = tuple(state)

    if has_real:
        def block(r0, carry, masked):
            out = []
            for q, grp, vc, c in zip(qs, groups, vcols, carry):
                s = lax.dot_general(q, k_ref[0, pl.ds(r0, tq), grp], nt, preferred_element_type=F32)
                if masked:
                    s = jnp.where(vis_ref[...] > 0.0, s, MASK_VALUE)
                out.append(step(s, v_ref[0, pl.ds(r0, tq), vc], c))
            return tuple(out)

        state = lax.fori_loop(0, qi, lambda kb, c: block(pl.multiple_of(kb * tq, tq), c, False), state)
        state = block(pl.multiple_of(qi * tq, tq), state, True)
    outs = [acc / l for (_, l, acc) in state]
    lane = lax.broadcasted_iota(jnp.int32, outs[0].shape, 1)
    pairs = [jnp.where(lane < V_DIM, outs[2 * p], outs[2 * p + 1]) for p in range(_ATTN_PAIRS)]
    o_ref[0] = jnp.concatenate(pairs, axis=-1).astype(BF16)


def _attention(q, k, v, km, vm, tq):
    b, sq, _ = q.shape
    has_real = k is not None
    qw = 2 * _ATTN_PAIRS * HEAD_PAD
    vw = 2 * _ATTN_PAIRS * V_DIM
    qspec = pl.BlockSpec((1, tq, qw), lambda bi, hp, qi: (bi, qi, hp))
    mk = pl.BlockSpec((META_PAD, qw), lambda bi, hp, qi: (0, hp))
    mv = pl.BlockSpec((META_PAD, vw), lambda bi, hp, qi: (0, hp))
    if has_real:
        s = k.shape[1]
        chunk = jnp.arange(tq, dtype=jnp.int32) // CHUNK
        vis = (chunk[None, :] <= chunk[:, None]).astype(F32)
        in_specs = [qspec, pl.BlockSpec((1, s, qw), lambda bi, hp, qi: (bi, 0, hp)),
                    pl.BlockSpec((1, s, vw), lambda bi, hp, qi: (bi, 0, hp)), mk, mv,
                    pl.BlockSpec((tq, tq), lambda bi, hp, qi: (0, 0))]
        args = (q, k, v, km, vm, vis)
    else:
        in_specs = [qspec, mk, mv]
        args = (q, km, vm)
    return pl.pallas_call(
        functools.partial(_attn_kernel, tq=tq, has_real=has_real),
        grid=(b, N_HEADS // (2 * _ATTN_PAIRS), sq // tq),
        in_specs=in_specs,
        out_specs=pl.BlockSpec((1, tq, vw), lambda bi, hp, qi: (bi, qi, hp)),
        out_shape=jax.ShapeDtypeStruct((b, sq, N_HEADS * V_DIM), BF16),
        compiler_params=_cparams("parallel", "parallel", "arbitrary"),
        name="attention",
    )(*args)


def _merge_kernel(h_ref, o_ref, gc_ref, g2_ref, wm_ref, wo_ref, out_ref):
    ymla = jnp.dot(o_ref[...], wm_ref[...], preferred_element_type=F32)
    merged = gc_ref[...].astype(F32) + g2_ref[...].astype(F32) * ymla
    out_ref[...] = h_ref[...] + jnp.dot(merged.astype(BF16), wo_ref[...], preferred_element_type=F32)


def _merge(h, o, gc, gates, wm, wo, tb):
    t, d = h.shape
    row = lambda c: pl.BlockSpec((tb, c), lambda i: (i, 0))
    full = lambda a: pl.BlockSpec(a.shape, lambda i: (0,) * a.ndim)
    return pl.pallas_call(
        _merge_kernel,
        grid=(t // tb,),
        in_specs=[row(d), row(N_HEADS * V_DIM), row(d), pl.BlockSpec((tb, d), lambda i: (i, 1)),
                  full(wm), full(wo)],
        out_specs=row(d),
        out_shape=jax.ShapeDtypeStruct((t, d), F32),
        compiler_params=_cparams("parallel"),
        name="merge_out",
    )(h, o, gc, gates, wm, wo)


def _peer_score_kernel(h_ref, g_ref, wq_ref, keys_ref, xt_ref, s_ref):
    d = h_ref.shape[-1]
    xn = _rms(h_ref[...], g_ref[...], d)
    xt = xn.T.astype(BF16)
    xt_ref[...] = xt
    qt = jnp.dot(wq_ref[...], xt, preferred_element_type=F32)
    for g in range(2 * PEER_HEADS):
        qg = qt[g * PEER_HALF:(g + 1) * PEER_HALF, :].astype(BF16)
        s_ref[g] = jnp.dot(keys_ref[g], qg, preferred_element_type=F32)


def _peer_scores(h, g, wqt, keys, tb):
    t, d = h.shape
    full = lambda a: pl.BlockSpec(a.shape, lambda i: (0,) * a.ndim)
    ng = 2 * PEER_HEADS
    return pl.pallas_call(
        _peer_score_kernel,
        grid=(t // tb,),
        in_specs=[pl.BlockSpec((tb, d), lambda i: (i, 0)), full(g), full(wqt), full(keys)],
        out_specs=[pl.BlockSpec((d, tb), lambda i: (0, i)),
                   pl.BlockSpec((ng, N_KEYS, tb), lambda i: (0, 0, i))],
        out_shape=[jax.ShapeDtypeStruct((d, t), BF16), jax.ShapeDtypeStruct((ng, N_KEYS, t), F32)],
        compiler_params=_cparams("parallel"),
        name="peer_scores",
    )(h, g, wqt, keys)


def _extract_topk(vals, pos, sv_ref):
    big = float(vals.shape[0] * vals.shape[0])

    def body(a, carry):
        cur, rank = carry
        m = jnp.max(cur, axis=0, keepdims=True)
        first = jnp.min(jnp.where(cur == m, pos, big), axis=0, keepdims=True)
        hit = pos == first
        if sv_ref is not None:
            sv_ref[pl.ds(a, 1), :] = m
        return jnp.where(hit, NEG_INF, cur), jnp.where(hit, lax.convert_element_type(a, F32), rank)

    init = (vals, jnp.full(vals.shape, float(PEER_TOPK), F32))
    return lax.fori_loop(0, PEER_TOPK, body, init)[1]


def _topk_exact_tile(s1, s2, sv1_ref, sv2_ref):
    lanes = s1.shape[-1]
    k = PEER_TOPK
    key_pos = lax.broadcasted_iota(jnp.int32, (N_KEYS, lanes), 0).astype(F32)
    crow = lax.broadcasted_iota(jnp.int32, (k * k, lanes), 0)
    cand_pos = ((crow % k) * k + crow // k).astype(F32)
    r1 = _extract_topk(s1, key_pos, sv1_ref)
    r2 = _extract_topk(s2, key_pos, sv2_ref)
    sv1 = sv1_ref[...]
    sv2 = sv2_ref[...]
    cand = jnp.concatenate([sv1 + sv2[b:b + 1, :] for b in range(k)], axis=0)
    sel = _extract_topk(cand, cand_pos, None) < float(k)
    t1 = jnp.exp(sv1 - sv1[0:1, :])
    t2 = jnp.exp(sv2 - sv2[0:1, :])
    z = jnp.zeros((1, lanes), F32)
    n_a = jnp.zeros((k, lanes), F32)
    for b in range(k):
        sb = sel[b * k:(b + 1) * k, :]
        z = z + jnp.sum(jnp.where(sb, t1 * t2[b:b + 1, :], 0.0), axis=0, keepdims=True)
        n_a = n_a + jnp.where(sb, 1.0, 0.0)
    cnt = jnp.zeros((N_KEYS, lanes), F32)
    for a in range(k):
        cnt = jnp.where(r1 == float(a), n_a[a:a + 1, :], cnt)
    return r2, cnt, jnp.exp(s1 - sv1[0:1, :]), jnp.exp(s2 - sv2[0:1, :]) / z


def _dup_bf16(x):
    bits = pltpu.bitcast(x.astype(BF16).astype(F32), jnp.uint32)
    return bits | (bits >> 16)


def _peer_topk_small_kernel(s_ref, r2_ref, cnt_ref, e1_ref, e2_ref, sv1_ref, sv2_ref):
    r2, cnt, e1, e2 = _topk_exact_tile(s_ref[0], s_ref[1], sv1_ref, sv2_ref)
    r2_ref[0] = r2.astype(BF16)
    cnt_ref[0] = _dup_bf16(cnt)
    e1_ref[0] = _dup_bf16(e1)
    e2_ref[0] = e2.astype(BF16)


def _batcher_pairs(n):
    pairs = []
    p = 1
    while p < n:
        k = p
        while k >= 1:
            for j in range(k % p, n - k, 2 * k):
                for i in range(min(k, n - j - k)):
                    if (i + j) // (2 * p) == (i + j + k) // (2 * p):
                        pairs.append((i + j, i + j + k))
            k //= 2
        p *= 2
    return pairs


_SORT16 = _batcher_pairs(PEER_TOPK)
_STAIR = [(a, b) for a in range(PEER_TOPK) for b in range(PEER_TOPK) if (a + 1) * (b + 1) <= PEER_TOPK]
_TILES = LANES * SUBLANES


def _top16_desc(load, lo, n):
    k = PEER_TOPK
    if n == k:
        v = [load(lo + i) for i in range(k)]
        for i, j in _SORT16:
            v[i], v[j] = jnp.maximum(v[i], v[j]), jnp.minimum(v[i], v[j])
        return v
    x = _top16_desc(load, lo, n // 2)
    y = _top16_desc(load, lo + n // 2, n // 2)
    c = [jnp.maximum(x[i], y[k - 1 - i]) for i in range(k)]
    d = k // 2
    while d >= 1:
        for i in range(k):
            if i & d == 0:
                c[i], c[i + d] = jnp.maximum(c[i], c[i + d]), jnp.minimum(c[i], c[i + d])
        d //= 2
    return c


def _sublane_transpose(vs):
    sub = lax.broadcasted_iota(jnp.int32, vs[0].shape, 0)
    d = SUBLANES // 2
    while d >= 1:
        low = (sub & d) == 0
        nxt = list(vs)
        for j in range(SUBLANES):
            if j & d == 0:
                x, y = vs[j], vs[j + d]
                nxt[j] = jnp.where(low, x, pltpu.roll(y, d, 0))
                nxt[j + d] = jnp.where(low, pltpu.roll(x, SUBLANES - d, 0), y)
        vs = nxt
        d //= 2
    return vs


def _peer_topk_kernel(s_ref, r2_ref, cnt_ref, e1_ref, e2_ref, slab_ref, sv_ref, u_ref, misc_ref,
                      sv1_ref, sv2_ref):
    k = PEER_TOPK
    inf = float("inf")
    for half in range(2):
        for kt in range(N_KEYS // SUBLANES):
            rows = slice(kt * SUBLANES, (kt + 1) * SUBLANES)
            tiles = [s_ref[half, rows, j * LANES:(j + 1) * LANES] for j in range(SUBLANES)]
            for r, slab in enumerate(_sublane_transpose(tiles)):
                slab_ref[half * N_KEYS + kt * SUBLANES + r] = slab
    sv1 = _top16_desc(lambda i: slab_ref[i], 0, N_KEYS)
    sv2 = _top16_desc(lambda i: slab_ref[i], N_KEYS, N_KEYS)
    for a in range(k):
        sv_ref[a] = sv1[a]
        sv_ref[k + a] = sv2[a]
    tie = jnp.zeros(sv1[0].shape, F32)
    for v in (sv1, sv2):
        for a in range(k - 1):
            tie = jnp.where(v[a] == v[a + 1], 1.0, tie)
    cand = {c: sv1[c[0]] + sv2[c[1]] for c in _STAIR}
    beaten = {c: float((c[0] + 1) * (c[1] + 1) - 1) for c in _STAIR}
    dyn = {c: None for c in _STAIR}
    for x_i, x in enumerate(_STAIR):
        for y in _STAIR[x_i + 1:]:
            if (x[0] < y[0]) == (x[1] < y[1]) or x[0] == y[0] or x[1] == y[1]:
                continue
            g = jnp.where(cand[x] >= cand[y], 1.0, 0.0)
            dyn[y] = g if dyn[y] is None else dyn[y] + g
            beaten[x] += 1.0
            dyn[x] = -g if dyn[x] is None else dyn[x] - g
    sel = {c: (dyn[c] + beaten[c] if dyn[c] is not None else jnp.full(tie.shape, beaten[c])) < float(k)
           for c in _STAIR}
    t1 = [jnp.exp(sv1[a] - sv1[0]) for a in range(k)]
    t2 = [jnp.exp(sv2[b] - sv2[0]) for b in range(k)]
    z = jnp.zeros(tie.shape, F32)
    for c in _STAIR:
        z = z + jnp.where(sel[c], t1[c[0]] * t2[c[1]], 0.0)
    for b in range(k):
        u = jnp.full(tie.shape, inf, F32)
        for a in range(k):
            if (a, b) in sel:
                u = jnp.minimum(u, jnp.where(sel[(a, b)], sv1[a], inf))
        u_ref[b] = u
    misc_ref[0] = sv1[0]
    misc_ref[1] = sv2[0]
    misc_ref[2] = 1.0 / z
    misc_ref[3] = tie

    rows2 = 2 * SUBLANES
    bad = jnp.zeros((1, LANES), F32)
    for tt in range(SUBLANES):
        cols = slice(tt * LANES, (tt + 1) * LANES)
        bc = lambda ref, i: jnp.broadcast_to(ref[i, tt:tt + 1, :], (rows2, LANES))
        svb = [bc(sv_ref, k + b) for b in range(k)]
        ub = [bc(u_ref, b) for b in range(k)]
        c1, c2, zinv, last1 = bc(misc_ref, 0), bc(misc_ref, 1), bc(misc_ref, 2), bc(sv_ref, k - 1)
        n1 = jnp.zeros((rows2, LANES), F32)
        n2 = jnp.zeros((rows2, LANES), F32)
        for kt in range(N_KEYS // rows2):
            rows = slice(kt * rows2, (kt + 1) * rows2)
            x1 = s_ref[0, rows, cols]
            x2 = s_ref[1, rows, cols]
            r = jnp.full((rows2, LANES), float(k), F32)
            c = jnp.full((rows2, LANES), float(k), F32)
            for b in reversed(range(k)):
                r = jnp.where(svb[b] <= x2, float(b), r)
                c = jnp.where(ub[b] > x1, float(b), c)
            r2_ref[0, rows, cols] = r.astype(BF16)
            cnt_ref[0, rows, cols] = _dup_bf16(c)
            e1_ref[0, rows, cols] = _dup_bf16(jnp.exp(x1 - c1))
            e2_ref[0, rows, cols] = (jnp.exp(x2 - c2) * zinv).astype(BF16)
            n1 = n1 + jnp.where(x1 >= last1, 1.0, 0.0)
            n2 = n2 + jnp.where(r < float(k), 1.0, 0.0)
        bad = (bad + jnp.abs(jnp.sum(n1, axis=0, keepdims=True) - k)
               + jnp.abs(jnp.sum(n2, axis=0, keepdims=True) - k))
    tied = jnp.max(bad) + jnp.max(misc_ref[3]) > 0.0

    @pl.when(tied)
    def _():
        for tt in range(SUBLANES):
            cols = slice(tt * LANES, (tt + 1) * LANES)
            r2, cnt, e1, e2 = _topk_exact_tile(s_ref[0, :, cols], s_ref[1, :, cols], sv1_ref, sv2_ref)
            r2_ref[0, :, cols] = r2.astype(BF16)
            cnt_ref[0, :, cols] = _dup_bf16(cnt)
            e1_ref[0, :, cols] = _dup_bf16(e1)
            e2_ref[0, :, cols] = e2.astype(BF16)


def _peer_topk(s):
    ng, nk, t = s.shape
    fast = t % _TILES == 0
    lanes = _TILES if fast else LANES
    spec = pl.BlockSpec((1, nk, lanes), lambda i, hd: (hd, 0, i))
    shp = lambda dt: jax.ShapeDtypeStruct((PEER_HEADS, nk, t), dt)
    slab = lambda n: pltpu.VMEM((n, SUBLANES, LANES), F32)
    row = pltpu.VMEM((PEER_TOPK, LANES), F32)
    scratch = [slab(2 * N_KEYS), slab(2 * PEER_TOPK), slab(PEER_TOPK), slab(4), row, row] if fast else [row, row]
    return pl.pallas_call(
        _peer_topk_kernel if fast else _peer_topk_small_kernel,
        grid=(t // lanes, PEER_HEADS),
        in_specs=[pl.BlockSpec((2, nk, lanes), lambda i, hd: (hd, 0, i))],
        out_specs=[spec, spec, spec, spec],
        out_shape=[shp(BF16), shp(jnp.uint32), shp(jnp.uint32), shp(BF16)],
        scratch_shapes=scratch,
        compiler_params=_cparams("parallel", "parallel"),
        name="peer_topk",
    )(s)


def _gelu_tanh(x):
    c = -2.0 * math.sqrt(2.0 / math.pi) * math.log2(math.e)
    return x / (1.0 + jnp.exp2(x * (c + (c * 0.044715) * (x * x))))


_DENSE_CHUNK_ROWS = 8


def _peer_dense_kernel(h_ref, xt_ref, u_ref, vt_ref, r2_ref, cnt_ref, e1_ref, e2_ref, o_ref, acc_ref,
                       *, rows_per_step):
    e = pl.program_id(1)
    lanes = xt_ref.shape[-1]
    bf16_rows = 2 * SUBLANES

    @pl.when(e == 0)
    def _():
        acc_ref[...] = jnp.zeros_like(acc_ref)

    ce = _DENSE_CHUNK_ROWS * N_KEYS
    n_chunks = rows_per_step // _DENSE_CHUNK_ROWS
    zero = jnp.zeros((), BF16)

    def scores(c):
        return jnp.dot(u_ref[c * ce:(c + 1) * ce, :], xt_ref[...], preferred_element_type=F32)

    def packed_row(ref, hd, i):
        word = jnp.broadcast_to(ref[hd, pl.ds(i, 1), :], (SUBLANES, lanes))
        return jnp.tile(pltpu.bitcast(word, BF16), (N_KEYS // bf16_rows, 1))

    a_next = scores(0)
    total = None
    for c in range(n_chunks):
        a = a_next
        ws = []
        for ii in range(_DENSE_CHUNK_ROWS):
            i = e * rows_per_step + c * _DENSE_CHUNK_ROWS + ii
            w = None
            for hd in range(PEER_HEADS):
                term = packed_row(e1_ref, hd, i) * jnp.where(r2_ref[hd] < packed_row(cnt_ref, hd, i),
                                                             e2_ref[hd], zero)
                w = term if w is None else w + term
            ws.append(w)
        if c + 1 < n_chunks:
            a_next = scores(c + 1)
        act = _gelu_tanh(a).astype(BF16) * jnp.concatenate(ws, axis=0)
        part = jnp.dot(vt_ref[:, c * ce:(c + 1) * ce], act, preferred_element_type=F32)
        total = part if total is None else part + total
    acc_ref[...] += total

    @pl.when(e == pl.num_programs(1) - 1)
    def _():
        o_ref[...] = h_ref[...] + acc_ref[...].T


def _peer_dense(h, xt, u, vt, r2, cnt, e1, e2, lanes, rows_per_step):
    t, d = h.shape
    ne = u.shape[0]
    eb = rows_per_step * N_KEYS
    tok = pl.BlockSpec((PEER_HEADS, N_KEYS, lanes), lambda i, e: (0, 0, i))
    return pl.pallas_call(
        functools.partial(_peer_dense_kernel, rows_per_step=rows_per_step),
        grid=(t // lanes, ne // eb),
        in_specs=[pl.BlockSpec((lanes, d), lambda i, e: (i, 0)),
                  pl.BlockSpec((d, lanes), lambda i, e: (0, i)),
                  pl.BlockSpec((eb, d), lambda i, e: (e, 0)),
                  pl.BlockSpec((d, eb), lambda i, e: (0, e)),
                  tok, tok, tok, tok],
        out_specs=pl.BlockSpec((lanes, d), lambda i, e: (i, 0)),
        out_shape=jax.ShapeDtypeStruct((t, d), F32),
        scratch_shapes=[pltpu.VMEM((d, lanes), F32)],
        compiler_params=_cparams("parallel", "arbitrary"),
        name="peer_dense",
    )(h, xt, u, vt, r2, cnt, e1, e2)


def _pad_groups(w, real, n_groups):
    k = w.shape[0]
    w = w.reshape(k, n_groups, real)
    return jnp.pad(w, ((0, 0), (0, 0), (0, HEAD_PAD - real))).reshape(k, n_groups * HEAD_PAD)


def _with_rope_copy(w):
    first = w[..., ROPE_LO:ROPE_LO + ROPE_HALF]
    pad = jnp.zeros(w.shape[:-1] + (HEAD_PAD - QK_DIM - ROPE_HALF,), w.dtype)
    return jnp.concatenate([w, first, pad], axis=-1)


def _layer_params(l, w_in, w_conv_out, w_uq, w_ukv, q_norm_g, k_norm_g, w_mla_out, w_out, peer_wq,
                  peer_keys, peer_u, peer_v):
    d = w_in.shape[1]
    wi = w_in[l]
    o0 = 2 * D_CONV
    o1 = o0 + Q_LORA
    o2 = o1 + KV_LORA
    o3 = o2 + ROPE_DIM
    rope_cols = _with_rope_copy(jnp.pad(wi[:, o2:o3], ((0, 0), (ROPE_LO, 0))))
    w_in_p = jnp.concatenate([wi[:, :o2], rope_cols, wi[:, o3:]], axis=1).astype(BF16)
    wkv = w_ukv[l].reshape(KV_LORA, N_HEADS, NOPE_DIM + V_DIM)
    wk = _pad_groups(wkv[:, :, :NOPE_DIM].reshape(KV_LORA, N_HEADS * NOPE_DIM), NOPE_DIM, N_HEADS)
    wv = wkv[:, :, NOPE_DIM:].reshape(KV_LORA, N_HEADS * V_DIM)
    padg = lambda g: _with_rope_copy(g).reshape(1, HEAD_PAD)
    return dict(
        w_in=w_in_p,
        w_conv_out=w_conv_out[l].astype(BF16),
        wq=_with_rope_copy(w_uq[l].reshape(Q_LORA, N_HEADS, QK_DIM)).reshape(Q_LORA, N_HEADS * HEAD_PAD).astype(BF16),
        wk=wk.astype(BF16),
        wv=wv.astype(BF16),
        qg=padg(q_norm_g[l]),
        kg=padg(k_norm_g[l]),
        w_mla_out=w_mla_out[l].astype(BF16),
        w_out=w_out[l].astype(BF16),
        peer_wqt=peer_wq[l],
        peer_keys=peer_keys[l].reshape(2 * PEER_HEADS, N_KEYS, PEER_HALF).astype(BF16),
        peer_u=peer_u[l],
        peer_vt=peer_v[l],
    )


def _rope_tables(length):
    pos = jnp.arange(length, dtype=F32)
    inv = 1.0 / (ROPE_THETA ** (jnp.arange(0, ROPE_DIM, 2, dtype=F32) / ROPE_DIM))
    ang = pos[:, None] * inv[None, :]
    cos = jnp.concatenate([jnp.cos(ang), jnp.cos(ang)], axis=-1)
    sin = jnp.concatenate([-jnp.sin(ang), jnp.sin(ang)], axis=-1)
    cos = jnp.pad(cos, ((0, 0), (ROPE_LO, 0)), constant_values=1.0)
    cos = jnp.pad(cos, ((0, 0), (0, HEAD_PAD - ROPE_LO - ROPE_DIM)))
    sin = jnp.pad(sin, ((0, 0), (ROPE_LO, HEAD_PAD - ROPE_LO - ROPE_DIM)))
    return cos, sin


def _row(v):
    return v.reshape(1, -1)


def kernel(x, meta_tokens, mix_norm_g, w_in, conv_w, conv_b, conv_ln_g, conv_ln_b, w_conv_out,
           q_a_norm_g, w_uq, kv_a_norm_g, w_ukv, q_norm_g, k_norm_g, w_mla_out, w_out, ffn_norm_g,
           peer_wq, peer_keys, peer_u, peer_v):
    bsz, seq, d = x.shape
    depth = w_in.shape[0]
    t = bsz * seq
    assert meta_tokens.shape[0] == N_META and seq % 256 == 0 and d % LANES == 0
    tb = min(512, seq)
    tq = min(512, seq)
    dense_lanes = min(512, t)
    rows_per_step = 16

    cos, sin = _rope_tables(N_META + seq)
    h = x.reshape(t, d)
    hm = meta_tokens.astype(x.dtype)
    zero_pre = jnp.zeros((N_META, D_CONV), F32)

    peer_wqt = jnp.swapaxes(peer_wq, 1, 2).astype(BF16)
    peer_ub = peer_u.astype(BF16)
    peer_vt = jnp.swapaxes(peer_v, 1, 2).astype(BF16)

    for l in range(depth):
        p = _layer_params(l, w_in, w_conv_out, w_uq, w_ukv, q_norm_g, k_norm_g, w_mla_out, w_out,
                          peer_wqt, peer_keys, peer_ub, peer_vt)
        last = l == depth - 1
        g_mix, qag, kvag = _row(mix_norm_g[l]), _row(q_a_norm_g[l]), _row(kv_a_norm_g[l])
        conv_args = (conv_w[l], _row(conv_b[l]), _row(conv_ln_g[l]), _row(conv_ln_b[l]), p["w_conv_out"])

        um, cqm, ckvm, krm, gatem = _inproj(hm, g_mix, p["w_in"], qag, kvag, N_META)
        qm, km, vm = _qkv(cqm, ckvm, krm, p["wq"], p["wk"], p["wv"], p["qg"], p["kg"],
                          cos[:N_META], sin[:N_META], N_META)
        km_p = jnp.pad(km, ((0, META_PAD - N_META), (0, 0)))
        vm_p = jnp.pad(vm, ((0, META_PAD - N_META), (0, 0)))

        u, cq, ckv, kr, gates = _inproj(h, g_mix, p["w_in"], qag, kvag, tb)
        gc = _conv_branch(u.reshape(bsz, seq, D_CONV), um, *conv_args, gates.reshape(bsz, seq, 2 * d))
        q, k, v = _qkv(cq, ckv, kr, p["wq"], p["wk"], p["wv"], p["qg"], p["kg"],
                       cos[N_META:], sin[N_META:], tb)
        hp = N_HEADS * HEAD_PAD
        o = _attention(q.reshape(bsz, seq, hp), k.reshape(bsz, seq, hp),
                       v.reshape(bsz, seq, N_HEADS * V_DIM), km_p, vm_p, tq)
        h1 = _merge(h, o.reshape(t, N_HEADS * V_DIM), gc.reshape(t, d), gates, p["w_mla_out"],
                    p["w_out"], tb)
        xt, s = _peer_scores(h1, _row(ffn_norm_g[l]), p["peer_wqt"], p["peer_keys"], tb)
        r2, cnt, e1, e2 = _peer_topk(s)
        h = _peer_dense(h1, xt, p["peer_u"], p["peer_vt"], r2, cnt, e1, e2, dense_lanes, rows_per_step)

        if not last:
            gcm = _conv_branch(um.reshape(1, N_META, D_CONV), zero_pre, *conv_args,
                               gatem.reshape(1, N_META, 2 * d))
            om = _attention(qm.reshape(1, N_META, hp), None, None, km_p, vm_p, N_META)
            hm1 = _merge(hm, om.reshape(N_META, N_HEADS * V_DIM), gcm.reshape(N_META, d), gatem,
                         p["w_mla_out"], p["w_out"], N_META)
            hm1_p = jnp.pad(hm1, ((0, LANES - N_META), (0, 0)))
            xtm, sm = _peer_scores(hm1_p, _row(ffn_norm_g[l]), p["peer_wqt"], p["peer_keys"], LANES)
            r2m, cntm, e1m, e2m = _peer_topk(sm)
            hm = _peer_dense(hm1_p, xtm, p["peer_u"], p["peer_vt"], r2m, cntm, e1m, e2m, LANES,
                             rows_per_step)[:N_META]

    return h.reshape(bsz, seq, d)
```

```python
import functools
import math

import jax
import jax.numpy as jnp
from jax import lax
from jax.experimental import pallas as pl
from jax.experimental.pallas import tpu as pltpu

F32 = jnp.float32
BF16 = jnp.bfloat16

CHUNK = 64
N_META = 16
D_CONV = 512
CONV_WIDTH = 31
N_HEADS = 8
Q_LORA = 256
KV_LORA = 128
NOPE_DIM = 64
ROPE_DIM = 32
QK_DIM = NOPE_DIM + ROPE_DIM
V_DIM = 64
ROPE_THETA = 10000.0
PEER_HEADS = 8
PEER_HALF = 128
N_KEYS = 128
PEER_TOPK = 16
EPS = 1e-6
MASK_VALUE = -1e30

LANES = 128
SUBLANES = 8
VMEM_LIMIT = 48 * 1024 * 1024

HEAD_PAD = LANES
ROPE_LO = NOPE_DIM
ROPE_HALF = ROPE_DIM // 2
META_PAD = LANES
NEG_INF = float("-inf")


def _cparams(*sem):
    return pltpu.CompilerParams(dimension_semantics=sem, vmem_limit_bytes=VMEM_LIMIT)


def _rms(x, g, n, valid=None):
    xs = x if valid is None else jnp.where(valid, x, 0.0)
    ms = jnp.sum(xs * xs, axis=-1, keepdims=True) * (1.0 / n)
    return x * lax.rsqrt(ms + EPS) * g


_C_CONV = 0
_C_Q = 2 * D_CONV
_C_KV = _C_Q + Q_LORA
_C_ROPE = _C_KV + KV_LORA
_C_GATE = _C_ROPE + HEAD_PAD


def _inproj_kernel(h_ref, g_ref, w_ref, qg_ref, kvg_ref, u_ref, cq_ref, ckv_ref, kr_ref, gate_ref):
    d = h_ref.shape[-1]
    xn = _rms(h_ref[...], g_ref[...], d).astype(BF16)

    def proj(lo, hi):
        return jnp.dot(xn, w_ref[:, lo:hi], preferred_element_type=F32)

    a = proj(_C_CONV, _C_CONV + D_CONV)
    b = proj(_C_CONV + D_CONV, _C_Q)
    u_ref[...] = a * jax.nn.sigmoid(b)
    cq_ref[...] = _rms(proj(_C_Q, _C_KV), qg_ref[...], Q_LORA).astype(BF16)
    ckv_ref[...] = _rms(proj(_C_KV, _C_ROPE), kvg_ref[...], KV_LORA).astype(BF16)
    kr_ref[...] = proj(_C_ROPE, _C_GATE)
    gate_ref[...] = jax.nn.sigmoid(proj(_C_GATE, _C_GATE + 2 * d)).astype(BF16)


def _inproj(h, g, w, qg, kvg, tb):
    t, d = h.shape
    n = w.shape[1]
    row = lambda c: pl.BlockSpec((tb, c), lambda i: (i, 0))
    full = lambda a: pl.BlockSpec(a.shape, lambda i: (0,) * a.ndim)
    return pl.pallas_call(
        _inproj_kernel,
        grid=(t // tb,),
        in_specs=[row(d), full(g), full(w), full(qg), full(kvg)],
        out_specs=[row(D_CONV), row(Q_LORA), row(KV_LORA), row(HEAD_PAD), row(2 * d)],
        out_shape=[jax.ShapeDtypeStruct((t, D_CONV), F32),
                   jax.ShapeDtypeStruct((t, Q_LORA), BF16),
                   jax.ShapeDtypeStruct((t, KV_LORA), BF16),
                   jax.ShapeDtypeStruct((t, HEAD_PAD), F32),
                   jax.ShapeDtypeStruct((t, 2 * d), BF16)],
        compiler_params=_cparams("parallel"),
        name="inproj",
    )(h, g, w, qg, kvg)


_CONV_PAD = 32
_CONV_TILE = 64


def _conv_kernel(u_ref, pre_ref, cw_ref, cb_ref, lg_ref, lb_ref, wo_ref, gate_ref, o_ref,
                 buf_ref, act_ref, sh_ref, *, rows):
    s = u_ref.shape[1]
    npre = pre_ref.shape[0]
    buf_ref[0:_CONV_PAD - npre, :] = jnp.zeros((_CONV_PAD - npre, D_CONV), F32)
    buf_ref[_CONV_PAD - npre:_CONV_PAD, :] = pre_ref[...]
    buf_ref[_CONV_PAD:, :] = u_ref[0]
    shift = _CONV_PAD - (CONV_WIDTH - 1)

    def tile(i, carry):
        r0 = pl.multiple_of(i * rows, rows)
        acc = jnp.zeros((rows, D_CONV), F32) + cb_ref[...]
        win = buf_ref[pl.ds(r0, rows + _CONV_PAD), :]
        for res in range(1, SUBLANES):
            sh_ref[res - 1] = win[res:res + rows + _CONV_PAD - SUBLANES, :]
        for k in range(CONV_WIDTH):
            res, off = (shift + k) % SUBLANES, (shift + k) // SUBLANES * SUBLANES
            x = win[off:off + rows, :] if res == 0 else sh_ref[res - 1, off:off + rows, :]
            acc = acc + cw_ref[k:k + 1, :] * x
        mu = jnp.mean(acc, axis=-1, keepdims=True)
        xc = acc - mu
        var = jnp.mean(xc * xc, axis=-1, keepdims=True)
        y = xc * lax.rsqrt(var + EPS) * lg_ref[...] + lb_ref[...]
        act_ref[pl.ds(r0, rows), :] = (y * jax.nn.sigmoid(y)).astype(BF16)
        return carry

    lax.fori_loop(0, s // rows, tile, 0)
    y = jnp.dot(act_ref[...], wo_ref[...], preferred_element_type=F32)
    o_ref[0] = (gate_ref[0].astype(F32) * y).astype(BF16)


def _conv_branch(u, pre, cw, cb, lg, lb, wo, gates):
    b, s, _ = u.shape
    d = wo.shape[1]
    rows = min(_CONV_TILE, s)
    full = lambda a: pl.BlockSpec(a.shape, lambda i: (0,) * a.ndim)
    return pl.pallas_call(
        functools.partial(_conv_kernel, rows=rows),
        grid=(b,),
        in_specs=[pl.BlockSpec((1, s, D_CONV), lambda i: (i, 0, 0)), full(pre), full(cw), full(cb),
                  full(lg), full(lb), full(wo), pl.BlockSpec((1, s, d), lambda i: (i, 0, 0))],
        out_specs=pl.BlockSpec((1, s, d), lambda i: (i, 0, 0)),
        out_shape=jax.ShapeDtypeStruct((b, s, d), BF16),
        scratch_shapes=[pltpu.VMEM((_CONV_PAD + s, D_CONV), F32), pltpu.VMEM((s, D_CONV), BF16),
                        pltpu.VMEM((SUBLANES - 1, rows + _CONV_PAD - SUBLANES, D_CONV), F32)],
        compiler_params=_cparams("parallel"),
        name="conv_branch",
    )(u, pre, cw, cb, lg, lb, wo, gates)


def _rope_group(x, cos, sin_signed):
    return x * cos + pltpu.roll(x, LANES - ROPE_HALF, 1) * sin_signed


def _qkv_kernel(cq_ref, ckv_ref, kr_ref, wq_ref, wk_ref, wv_ref, qg_ref, kg_ref, cos_ref, sin_ref,
                q_ref, k_ref, v_ref):
    cos = cos_ref[...]
    sin = sin_ref[...]
    qf = jnp.dot(cq_ref[...], wq_ref[...], preferred_element_type=F32)
    kf = jnp.dot(ckv_ref[...], wk_ref[...], preferred_element_type=F32)
    kr = kr_ref[...]
    scale = QK_DIM ** -0.5 * math.log2(math.e)
    real = lax.broadcasted_iota(jnp.int32, kr.shape, 1) < QK_DIM
    for hd in range(N_HEADS):
        grp = slice(hd * HEAD_PAD, (hd + 1) * HEAD_PAD)
        qn = _rms(qf[:, grp], qg_ref[...], QK_DIM, real)
        q_ref[:, grp] = (_rope_group(qn, cos, sin) * scale).astype(BF16)
        kn = _rms(kf[:, grp] + kr, kg_ref[...], QK_DIM, real)
        k_ref[:, grp] = _rope_group(kn, cos, sin).astype(BF16)
    v_ref[...] = jnp.dot(ckv_ref[...], wv_ref[...], preferred_element_type=F32).astype(BF16)


def _qkv(cq, ckv, kr, wq, wk, wv, qg, kg, cos, sin, tb):
    t = cq.shape[0]
    nrope = cos.shape[0] // tb
    row = lambda c: pl.BlockSpec((tb, c), lambda i: (i, 0))
    full = lambda a: pl.BlockSpec(a.shape, lambda i: (0,) * a.ndim)
    rope = pl.BlockSpec((tb, HEAD_PAD), lambda i: (i % nrope, 0))
    hp = N_HEADS * HEAD_PAD
    return pl.pallas_call(
        _qkv_kernel,
        grid=(t // tb,),
        in_specs=[row(Q_LORA), row(KV_LORA), row(HEAD_PAD), full(wq), full(wk), full(wv),
                  full(qg), full(kg), rope, rope],
        out_specs=[row(hp), row(hp), row(N_HEADS * V_DIM)],
        out_shape=[jax.ShapeDtypeStruct((t, hp), BF16), jax.ShapeDtypeStruct((t, hp), BF16),
                   jax.ShapeDtypeStruct((t, N_HEADS * V_DIM), BF16)],
        compiler_params=_cparams("parallel"),
        name="qkv",
    )(cq, ckv, kr, wq, wk, wv, qg, kg, cos, sin)


_ATTN_PAIRS = 2


def _attn_kernel(*refs, tq, has_real):
    if has_real:
        q_ref, k_ref, v_ref, km_ref, vm_ref, vis_ref, o_ref = refs
    else:
        q_ref, km_ref, vm_ref, o_ref = refs
    qi = pl.program_id(2)
    nt = (((1,), (1,)), ((), ()))
    heads = range(2 * _ATTN_PAIRS)
    groups = [slice(hh * HEAD_PAD, (hh + 1) * HEAD_PAD) for hh in heads]
    vcols = [slice((hh // 2) * 2 * V_DIM, (hh // 2 + 1) * 2 * V_DIM) for hh in heads]
    qs = [q_ref[0, :, grp] for grp in groups]

    def step(s, v, carry):
        m, l, acc = carry
        m_new = jnp.maximum(m, jnp.max(s, axis=-1, keepdims=True))
        alpha = jnp.exp2(m - m_new)
        p = jnp.exp2(s - m_new)
        l = alpha * l + jnp.sum(p, axis=-1, keepdims=True)
        acc = alpha * acc + jnp.dot(p.astype(BF16), v, preferred_element_type=F32)
        return m_new, l, acc

    state = []
    for q, grp, vc in zip(qs, groups, vcols):
        s = lax.dot_general(q, km_ref[:, grp], nt, preferred_element_type=F32)
        col = lax.broadcasted_iota(jnp.int32, s.shape, 1)
        s = jnp.where(col < N_META, s, MASK_VALUE)
        m = jnp.max(s, axis=-1, keepdims=True)
        p = jnp.exp2(s - m)
        l = jnp.sum(p, axis=-1, keepdims=True)
        state.append((m, l, jnp.dot(p.astype(BF16), vm_ref[:, vc], preferred_element_type=F32)))
    state = tuple(state)

    if has_real:
        def block(r0, carry, masked):
            out = []
            for q, grp, vc, c in zip(qs, groups, vcols, carry):
                s = lax.dot_general(q, k_ref[0, pl.ds(r0, tq), grp], nt, preferred_element_type=F32)
                if masked:
                    s = jnp.where(vis_ref[...] > 0.0, s, MASK_VALUE)
                out.append(step(s, v_ref[0, pl.ds(r0, tq), vc], c))
            return tuple(out)

        state = lax.fori_loop(0, qi, lambda kb, c: block(pl.multiple_of(kb * tq, tq), c, False), state)
        state = block(pl.multiple_of(qi * tq, tq), state, True)
    outs = [acc / l for (_, l, acc) in state]
    lane = lax.broadcasted_iota(jnp.int32, outs[0].shape, 1)
    pairs = [jnp.where(lane < V_DIM, outs[2 * p], outs[2 * p + 1]) for p in range(_ATTN_PAIRS)]
    o_ref[0] = jnp.concatenate(pairs, axis=-1).astype(BF16)


def _attention(q, k, v, km, vm, tq):
    b, sq, _ = q.shape
    has_real = k is not None
    qw = 2 * _ATTN_PAIRS * HEAD_PAD
    vw = 2 * _ATTN_PAIRS * V_DIM
    qspec = pl.BlockSpec((1, tq, qw), lambda bi, hp, qi: (bi, qi, hp))
    mk = pl.BlockSpec((META_PAD, qw), lambda bi, hp, qi: (0, hp))
    mv = pl.BlockSpec((META_PAD, vw), lambda bi, hp, qi: (0, hp))
    if has_real:
        s = k.shape[1]
        chunk = jnp.arange(tq, dtype=jnp.int32) // CHUNK
        vis = (chunk[None, :] <= chunk[:, None]).astype(F32)
        in_specs = [qspec, pl.BlockSpec((1, s, qw), lambda bi, hp, qi: (bi, 0, hp)),
                    pl.BlockSpec((1, s, vw), lambda bi, hp, qi: (bi, 0, hp)), mk, mv,
                    pl.BlockSpec((tq, tq), lambda bi, hp, qi: (0, 0))]
        args = (q, k, v, km, vm, vis)
    else:
        in_specs = [qspec, mk, mv]
        args = (q, km, vm)
    return pl.pallas_call(
        functools.partial(_attn_kernel, tq=tq, has_real=has_real),
        grid=(b, N_HEADS // (2 * _ATTN_PAIRS), sq // tq),
        in_specs=in_specs,
        out_specs=pl.BlockSpec((1, tq, vw), lambda bi, hp, qi: (bi, qi, hp)),
        out_shape=jax.ShapeDtypeStruct((b, sq, N_HEADS * V_DIM), BF16),
        compiler_params=_cparams("parallel", "parallel", "arbitrary"),
        name="attention",
    )(*args)


def _merge_kernel(h_ref, o_ref, gc_ref, g2_ref, wm_ref, wo_ref, out_ref):
    ymla = jnp.dot(o_ref[...], wm_ref[...], preferred_element_type=F32)
    merged = gc_ref[...].astype(F32) + g2_ref[...].astype(F32) * ymla
    out_ref[...] = h_ref[...] + jnp.dot(merged.astype(BF16), wo_ref[...], preferred_element_type=F32)


def _merge(h, o, gc, gates, wm, wo, tb):
    t, d = h.shape
    row = lambda c: pl.BlockSpec((tb, c), lambda i: (i, 0))
    full = lambda a: pl.BlockSpec(a.shape, lambda i: (0,) * a.ndim)
    return pl.pallas_call(
        _merge_kernel,
        grid=(t // tb,),
        in_specs=[row(d), row(N_HEADS * V_DIM), row(d), pl.BlockSpec((tb, d), lambda i: (i, 1)),
                  full(wm), full(wo)],
        out_specs=row(d),
        out_shape=jax.ShapeDtypeStruct((t, d), F32),
        compiler_params=_cparams("parallel"),
        name="merge_out",
    )(h, o, gc, gates, wm, wo)


def _peer_score_kernel(h_ref, g_ref, wq_ref, keys_ref, xt_ref, s_ref):
    d = h_ref.shape[-1]
    xn = _rms(h_ref[...], g_ref[...], d)
    xt = xn.T.astype(BF16)
    xt_ref[...] = xt
    qt = jnp.dot(wq_ref[...], xt, preferred_element_type=F32)
    for g in range(2 * PEER_HEADS):
        qg = qt[g * PEER_HALF:(g + 1) * PEER_HALF, :].astype(BF16)
        s_ref[g] = jnp.dot(keys_ref[g], qg, preferred_element_type=F32)


def _peer_scores(h, g, wqt, keys, tb):
    t, d = h.shape
    full = lambda a: pl.BlockSpec(a.shape, lambda i: (0,) * a.ndim)
    ng = 2 * PEER_HEADS
    return pl.pallas_call(
        _peer_score_kernel,
        grid=(t // tb,),
        in_specs=[pl.BlockSpec((tb, d), lambda i: (i, 0)), full(g), full(wqt), full(keys)],
        out_specs=[pl.BlockSpec((d, tb), lambda i: (0, i)),
                   pl.BlockSpec((ng, N_KEYS, tb), lambda i: (0, 0, i))],
        out_shape=[jax.ShapeDtypeStruct((d, t), BF16), jax.ShapeDtypeStruct((ng, N_KEYS, t), F32)],
        compiler_params=_cparams("parallel"),
        name="peer_scores",
    )(h, g, wqt, keys)


def _extract_topk(vals, pos, sv_ref):
    big = float(vals.shape[0] * vals.shape[0])

    def body(a, carry):
        cur, rank = carry
        m = jnp.max(cur, axis=0, keepdims=True)
        first = jnp.min(jnp.where(cur == m, pos, big), axis=0, keepdims=True)
        hit = pos == first
        if sv_ref is not None:
            sv_ref[pl.ds(a, 1), :] = m
        return jnp.where(hit, NEG_INF, cur), jnp.where(hit, lax.convert_element_type(a, F32), rank)

    init = (vals, jnp.full(vals.shape, float(PEER_TOPK), F32))
    return lax.fori_loop(0, PEER_TOPK, body, init)[1]


def _topk_exact_tile(s1, s2, sv1_ref, sv2_ref):
    lanes = s1.shape[-1]
    k = PEER_TOPK
    key_pos = lax.broadcasted_iota(jnp.int32, (N_KEYS, lanes), 0).astype(F32)
    crow = lax.broadcasted_iota(jnp.int32, (k * k, lanes), 0)
    cand_pos = ((crow % k) * k + crow // k).astype(F32)
    r1 = _extract_topk(s1, key_pos, sv1_ref)
    r2 = _extract_topk(s2, key_pos, sv2_ref)
    sv1 = sv1_ref[...]
    sv2 = sv2_ref[...]
    cand = jnp.concatenate([sv1 + sv2[b:b + 1, :] for b in range(k)], axis=0)
    sel = _extract_topk(cand, cand_pos, None) < float(k)
    t1 = jnp.exp(sv1 - sv1[0:1, :])
    t2 = jnp.exp(sv2 - sv2[0:1, :])
    z = jnp.zeros((1, lanes), F32)
    n_a = jnp.zeros((k, lanes), F32)
    for b in range(k):
        sb = sel[b * k:(b + 1) * k, :]
        z = z + jnp.sum(jnp.where(sb, t1 * t2[b:b + 1, :], 0.0), axis=0, keepdims=True)
        n_a = n_a + jnp.where(sb, 1.0, 0.0)
    cnt = jnp.zeros((N_KEYS, lanes), F32)
    for a in range(k):
        cnt = jnp.where(r1 == float(a), n_a[a:a + 1, :], cnt)
    return r2, cnt, jnp.exp(s1 - sv1[0:1, :]), jnp.exp(s2 - sv2[0:1, :]) / z


def _dup_bf16(x):
    bits = pltpu.bitcast(x.astype(BF16).astype(F32), jnp.uint32)
    return bits | (bits >> 16)


def _peer_topk_small_kernel(s_ref, r2_ref, cnt_ref, e1_ref, e2_ref, sv1_ref, sv2_ref):
    r2, cnt, e1, e2 = _topk_exact_tile(s_ref[0], s_ref[1], sv1_ref, sv2_ref)
    r2_ref[0] = r2.astype(BF16)
    cnt_ref[0] = _dup_bf16(cnt)
    e1_ref[0] = _dup_bf16(e1)
    e2_ref[0] = e2.astype(BF16)


def _batcher_pairs(n):
    pairs = []
    p = 1
    while p < n:
        k = p
        while k >= 1:
            for j in range(k % p, n - k, 2 * k):
                for i in range(min(k, n - j - k)):
                    if (i + j) // (2 * p) == (i + j + k) // (2 * p):
                        pairs.append((i + j, i + j + k))
            k //= 2
        p *= 2
    return pairs


_SORT16 = _batcher_pairs(PEER_TOPK)
_STAIR = [(a, b) for a in range(PEER_TOPK) for b in range(PEER_TOPK) if (a + 1) * (b + 1) <= PEER_TOPK]
_TILES = LANES * SUBLANES


def _top16_desc(load, lo, n):
    k = PEER_TOPK
    if n == k:
        v = [load(lo + i) for i in range(k)]
        for i, j in _SORT16:
            v[i], v[j] = jnp.maximum(v[i], v[j]), jnp.minimum(v[i], v[j])
        return v
    x = _top16_desc(load, lo, n // 2)
    y = _top16_desc(load, lo + n // 2, n // 2)
    c = [jnp.maximum(x[i], y[k - 1 - i]) for i in range(k)]
    d = k // 2
    while d >= 1:
        for i in range(k):
            if i & d == 0:
                c[i], c[i + d] = jnp.maximum(c[i], c[i + d]), jnp.minimum(c[i], c[i + d])
        d //= 2
    return c


def _sublane_transpose(vs):
    sub = lax.broadcasted_iota(jnp.int32, vs[0].shape, 0)
    d = SUBLANES // 2
    while d >= 1:
        low = (sub & d) == 0
        nxt = list(vs)
        for j in range(SUBLANES):
            if j & d == 0:
                x, y = vs[j], vs[j + d]
                nxt[j] = jnp.where(low, x, pltpu.roll(y, d, 0))
                nxt[j + d] = jnp.where(low, pltpu.roll(x, SUBLANES - d, 0), y)
        vs = nxt
        d //= 2
    return vs


def _peer_topk_kernel(s_ref, r2_ref, cnt_ref, e1_ref, e2_ref, slab_ref, sv_ref, u_ref, misc_ref,
                      sv1_ref, sv2_ref):
    k = PEER_TOPK
    inf = float("inf")
    for half in range(2):
        for kt in range(N_KEYS // SUBLANES):
            rows = slice(kt * SUBLANES, (kt + 1) * SUBLANES)
            tiles = [s_ref[half, rows, j * LANES:(j + 1) * LANES] for j in range(SUBLANES)]
            for r, slab in enumerate(_sublane_transpose(tiles)):
                slab_ref[half * N_KEYS + kt * SUBLANES + r] = slab
    sv1 = _top16_desc(lambda i: slab_ref[i], 0, N_KEYS)
    sv2 = _top16_desc(lambda i: slab_ref[i], N_KEYS, N_KEYS)
    for a in range(k):
        sv_ref[a] = sv1[a]
        sv_ref[k + a] = sv2[a]
    tie = jnp.zeros(sv1[0].shape, F32)
    for v in (sv1, sv2):
        for a in range(k - 1):
            tie = jnp.where(v[a] == v[a + 1], 1.0, tie)
    cand = {c: sv1[c[0]] + sv2[c[1]] for c in _STAIR}
    beaten = {c: float((c[0] + 1) * (c[1] + 1) - 1) for c in _STAIR}
    dyn = {c: None for c in _STAIR}
    for x_i, x in enumerate(_STAIR):
        for y in _STAIR[x_i + 1:]:
            if (x[0] < y[0]) == (x[1] < y[1]) or x[0] == y[0] or x[1] == y[1]:
                continue
            g = jnp.where(cand[x] >= cand[y], 1.0, 0.0)
            dyn[y] = g if dyn[y] is None else dyn[y] + g
            beaten[x] += 1.0
            dyn[x] = -g if dyn[x] is None else dyn[x] - g
    sel = {c: (dyn[c] + beaten[c] if dyn[c] is not None else jnp.full(tie.shape, beaten[c])) < float(k)
           for c in _STAIR}
    t1 = [jnp.exp(sv1[a] - sv1[0]) for a in range(k)]
    t2 = [jnp.exp(sv2[b] - sv2[0]) for b in range(k)]
    z = jnp.zeros(tie.shape, F32)
    for c in _STAIR:
        z = z + jnp.where(sel[c], t1[c[0]] * t2[c[1]], 0.0)
    for b in range(k):
        u = jnp.full(tie.shape, inf, F32)
        for a in range(k):
            if (a, b) in sel:
                u = jnp.minimum(u, jnp.where(sel[(a, b)], sv1[a], inf))
        u_ref[b] = u
    misc_ref[0] = sv1[0]
    misc_ref[1] = sv2[0]
    misc_ref[2] = 1.0 / z
    misc_ref[3] = tie

    rows2 = 2 * SUBLANES
    bad = []
    for tt in range(SUBLANES):
        cols = slice(tt * LANES, (tt + 1) * LANES)
        bc = lambda ref, i: jnp.broadcast_to(ref[i, tt:tt + 1, :], (rows2, LANES))
        svb = [bc(sv_ref, k + b) for b in range(k)]
        ub = [bc(u_ref, b) for b in range(k)]
        c1, c2, zinv, last1 = bc(misc_ref, 0), bc(misc_ref, 1), bc(misc_ref, 2), bc(sv_ref, k - 1)
        n1 = jnp.zeros((rows2, LANES), F32)
        n2 = jnp.zeros((rows2, LANES), F32)
        for kt in range(N_KEYS // rows2):
            rows = slice(kt * rows2, (kt + 1) * rows2)
            x1 = s_ref[0, rows, cols]
            x2 = s_ref[1, rows, cols]
            r = jnp.full((rows2, LANES), float(k), F32)
            c = jnp.full((rows2, LANES), float(k), F32)
            for b in reversed(range(k)):
                r = jnp.where(svb[b] <= x2, float(b), r)
                c = jnp.where(ub[b] > x1, float(b), c)
            r2_ref[0, rows, cols] = r.astype(BF16)
            cnt_ref[0, rows, cols] = _dup_bf16(c)
            e1_ref[0, rows, cols] = _dup_bf16(jnp.exp(x1 - c1))
            e2_ref[0, rows, cols] = (jnp.exp(x2 - c2) * zinv).astype(BF16)
            n1 = n1 + jnp.where(x1 >= last1, 1.0, 0.0)
            n2 = n2 + jnp.where(r < float(k), 1.0, 0.0)
        bad.append(jnp.abs(jnp.sum(n1, axis=0, keepdims=True) - k)
                   + jnp.abs(jnp.sum(n2, axis=0, keepdims=True) - k))

    for tt in range(SUBLANES):
        tied = jnp.max(bad[tt]) + jnp.max(misc_ref[3, tt:tt + 1, :]) > 0.0

        @pl.when(tied)
        def _():
            cols = slice(tt * LANES, (tt + 1) * LANES)
            r2, cnt, e1, e2 = _topk_exact_tile(s_ref[0, :, cols], s_ref[1, :, cols], sv1_ref, sv2_ref)
            r2_ref[0, :, cols] = r2.astype(BF16)
            cnt_ref[0, :, cols] = _dup_bf16(cnt)
            e1_ref[0, :, cols] = _dup_bf16(e1)
            e2_ref[0, :, cols] = e2.astype(BF16)


def _peer_topk(s):
    ng, nk, t = s.shape
    fast = t % _TILES == 0
    lanes = _TILES if fast else LANES
    spec = pl.BlockSpec((1, nk, lanes), lambda i, hd: (hd, 0, i))
    shp = lambda dt: jax.ShapeDtypeStruct((PEER_HEADS, nk, t), dt)
    slab = lambda n: pltpu.VMEM((n, SUBLANES, LANES), F32)
    row = pltpu.VMEM((PEER_TOPK, LANES), F32)
    scratch = [slab(2 * N_KEYS), slab(2 * PEER_TOPK), slab(PEER_TOPK), slab(4), row, row] if fast else [row, row]
    return pl.pallas_call(
        _peer_topk_kernel if fast else _peer_topk_small_kernel,
        grid=(t // lanes, PEER_HEADS),
        in_specs=[pl.BlockSpec((2, nk, lanes), lambda i, hd: (hd, 0, i))],
        out_specs=[spec, spec, spec, spec],
        out_shape=[shp(BF16), shp(jnp.uint32), shp(jnp.uint32), shp(BF16)],
        scratch_shapes=scratch,
        compiler_params=_cparams("parallel", "parallel"),
        name="peer_topk",
    )(s)


def _gelu_tanh(x):
    c = -2.0 * math.sqrt(2.0 / math.pi) * math.log2(math.e)
    return x / (1.0 + jnp.exp2(x * (c + (c * 0.044715) * (x * x))))


_DENSE_CHUNK_ROWS = 8
_DENSE_KW = 512


def _peer_dense_kernel(*refs, rows_per_step, n_u, n_v):
    h_ref, xt_ref = refs[:2]
    u_refs = refs[2:2 + n_u]
    vt_refs = refs[2 + n_u:2 + n_u + n_v]
    r2_ref, cnt_ref, e1_ref, e2_ref, o_ref, acc_ref = refs[2 + n_u + n_v:]
    e = pl.program_id(1)
    lanes = xt_ref.shape[-1]
    bf16_rows = 2 * SUBLANES
    kw = _DENSE_KW

    @pl.when(e == 0)
    def _():
        acc_ref[...] = jnp.zeros_like(acc_ref)

    ce = _DENSE_CHUNK_ROWS * N_KEYS
    n_chunks = rows_per_step // _DENSE_CHUNK_ROWS
    zero = jnp.zeros((), BF16)

    def scores(c):
        out = None
        for k, u_ref in enumerate(u_refs):
            part = jnp.dot(u_ref[c * ce:(c + 1) * ce, :], xt_ref[k * kw:(k + 1) * kw, :],
                           preferred_element_type=F32)
            out = part if out is None else part + out
        return out

    def packed_row(ref, hd, i):
        word = jnp.broadcast_to(ref[hd, pl.ds(i, 1), :], (SUBLANES, lanes))
        return jnp.tile(pltpu.bitcast(word, BF16), (N_KEYS // bf16_rows, 1))

    a_next = scores(0)
    total = None
    for c in range(n_chunks):
        a = a_next
        ws = []
        for ii in range(_DENSE_CHUNK_ROWS):
            i = e * rows_per_step + c * _DENSE_CHUNK_ROWS + ii
            w = None
            for hd in range(PEER_HEADS):
                term = packed_row(e1_ref, hd, i) * jnp.where(r2_ref[hd] < packed_row(cnt_ref, hd, i),
                                                             e2_ref[hd], zero)
                w = term if w is None else w + term
            ws.append(w)
        if c + 1 < n_chunks:
            a_next = scores(c + 1)
        act = _gelu_tanh(a.astype(BF16)) * jnp.concatenate(ws, axis=0)
        for k in range(ce // kw):
            part = jnp.dot(vt_refs[c * (ce // kw) + k][...], act[k * kw:(k + 1) * kw, :],
                           preferred_element_type=F32)
            total = part if total is None else part + total
    acc_ref[...] += total

    @pl.when(e == pl.num_programs(1) - 1)
    def _():
        o_ref[...] = h_ref[...] + acc_ref[...].T


def _peer_dense(h, xt, u, vt, r2, cnt, e1, e2, lanes, rows_per_step):
    t, d = h.shape
    ne = u.shape[0]
    eb = rows_per_step * N_KEYS
    kw = _DENSE_KW
    n_u, n_v = d // kw, eb // kw
    tok = pl.BlockSpec((PEER_HEADS, N_KEYS, lanes), lambda i, e: (0, 0, i))
    u_specs = [pl.BlockSpec((eb, kw), functools.partial(lambda i, e, k: (e, k), k=k)) for k in range(n_u)]
    v_specs = [pl.BlockSpec((d, kw), functools.partial(lambda i, e, k: (0, e * n_v + k), k=k))
               for k in range(n_v)]
    return pl.pallas_call(
        functools.partial(_peer_dense_kernel, rows_per_step=rows_per_step, n_u=n_u, n_v=n_v),
        grid=(t // lanes, ne // eb),
        in_specs=[pl.BlockSpec((lanes, d), lambda i, e: (i, 0)),
                  pl.BlockSpec((d, lanes), lambda i, e: (0, i))] + u_specs + v_specs + [tok, tok, tok, tok],
        out_specs=pl.BlockSpec((lanes, d), lambda i, e: (i, 0)),
        out_shape=jax.ShapeDtypeStruct((t, d), F32),
        scratch_shapes=[pltpu.VMEM((d, lanes), F32)],
        compiler_params=_cparams("parallel", "arbitrary"),
        name="peer_dense",
    )(h, xt, *([u] * n_u), *([vt] * n_v), r2, cnt, e1, e2)


def _pad_groups(w, real, n_groups):
    k = w.shape[0]
    w = w.reshape(k, n_groups, real)
    return jnp.pad(w, ((0, 0), (0, 0), (0, HEAD_PAD - real))).reshape(k, n_groups * HEAD_PAD)


def _with_rope_copy(w):
    first = w[..., ROPE_LO:ROPE_LO + ROPE_HALF]
    pad = jnp.zeros(w.shape[:-1] + (HEAD_PAD - QK_DIM - ROPE_HALF,), w.dtype)
    return jnp.concatenate([w, first, pad], axis=-1)


def _layer_params(l, w_in, w_conv_out, w_uq, w_ukv, q_norm_g, k_norm_g, w_mla_out, w_out, peer_wq,
                  peer_keys, peer_u, peer_v):
    d = w_in.shape[1]
    wi = w_in[l]
    o0 = 2 * D_CONV
    o1 = o0 + Q_LORA
    o2 = o1 + KV_LORA
    o3 = o2 + ROPE_DIM
    rope_cols = _with_rope_copy(jnp.pad(wi[:, o2:o3], ((0, 0), (ROPE_LO, 0))))
    w_in_p = jnp.concatenate([wi[:, :o2], rope_cols, wi[:, o3:]], axis=1).astype(BF16)
    wkv = w_ukv[l].reshape(KV_LORA, N_HEADS, NOPE_DIM + V_DIM)
    wk = _pad_groups(wkv[:, :, :NOPE_DIM].reshape(KV_LORA, N_HEADS * NOPE_DIM), NOPE_DIM, N_HEADS)
    wv = wkv[:, :, NOPE_DIM:].reshape(KV_LORA, N_HEADS * V_DIM)
    padg = lambda g: _with_rope_copy(g).reshape(1, HEAD_PAD)
    return dict(
        w_in=w_in_p,
        w_conv_out=w_conv_out[l].astype(BF16),
        wq=_with_rope_copy(w_uq[l].reshape(Q_LORA, N_HEADS, QK_DIM)).reshape(Q_LORA, N_HEADS * HEAD_PAD).astype(BF16),
        wk=wk.astype(BF16),
        wv=wv.astype(BF16),
        qg=padg(q_norm_g[l]),
        kg=padg(k_norm_g[l]),
        w_mla_out=w_mla_out[l].astype(BF16),
        w_out=w_out[l].astype(BF16),
        peer_wqt=peer_wq[l],
        peer_keys=peer_keys[l].reshape(2 * PEER_HEADS, N_KEYS, PEER_HALF).astype(BF16),
        peer_u=peer_u[l],
        peer_vt=peer_v[l],
    )


def _rope_tables(length):
    pos = jnp.arange(length, dtype=F32)
    inv = 1.0 / (ROPE_THETA ** (jnp.arange(0, ROPE_DIM, 2, dtype=F32) / ROPE_DIM))
    ang = pos[:, None] * inv[None, :]
    cos = jnp.concatenate([jnp.cos(ang), jnp.cos(ang)], axis=-1)
    sin = jnp.concatenate([-jnp.sin(ang), jnp.sin(ang)], axis=-1)
    cos = jnp.pad(cos, ((0, 0), (ROPE_LO, 0)), constant_values=1.0)
    cos = jnp.pad(cos, ((0, 0), (0, HEAD_PAD - ROPE_LO - ROPE_DIM)))
    sin = jnp.pad(sin, ((0, 0), (ROPE_LO, HEAD_PAD - ROPE_LO - ROPE_DIM)))
    return cos, sin


def _row(v):
    return v.reshape(1, -1)


def kernel(x, meta_tokens, mix_norm_g, w_in, conv_w, conv_b, conv_ln_g, conv_ln_b, w_conv_out,
           q_a_norm_g, w_uq, kv_a_norm_g, w_ukv, q_norm_g, k_norm_g, w_mla_out, w_out, ffn_norm_g,
           peer_wq, peer_keys, peer_u, peer_v):
    bsz, seq, d = x.shape
    depth = w_in.shape[0]
    t = bsz * seq
    assert meta_tokens.shape[0] == N_META and seq % 256 == 0 and d % LANES == 0
    tb = min(512, seq)
    tq = min(512, seq)
    dense_lanes = min(512, t)
    rows_per_step = 16

    cos, sin = _rope_tables(N_META + seq)
    h = x.reshape(t, d)
    hm = meta_tokens.astype(x.dtype)
    zero_pre = jnp.zeros((N_META, D_CONV), F32)

    peer_wqt = jnp.swapaxes(peer_wq, 1, 2).astype(BF16)
    peer_ub = peer_u.astype(BF16)
    peer_vt = jnp.swapaxes(peer_v, 1, 2).astype(BF16)

    for l in range(depth):
        p = _layer_params(l, w_in, w_conv_out, w_uq, w_ukv, q_norm_g, k_norm_g, w_mla_out, w_out,
                          peer_wqt, peer_keys, peer_ub, peer_vt)
        last = l == depth - 1
        g_mix, qag, kvag = _row(mix_norm_g[l]), _row(q_a_norm_g[l]), _row(kv_a_norm_g[l])
        conv_args = (conv_w[l], _row(conv_b[l]), _row(conv_ln_g[l]), _row(conv_ln_b[l]), p["w_conv_out"])

        um, cqm, ckvm, krm, gatem = _inproj(hm, g_mix, p["w_in"], qag, kvag, N_META)
        qm, km, vm = _qkv(cqm, ckvm, krm, p["wq"], p["wk"], p["wv"], p["qg"], p["kg"],
                          cos[:N_META], sin[:N_META], N_META)
        km_p = jnp.pad(km, ((0, META_PAD - N_META), (0, 0)))
        vm_p = jnp.pad(vm, ((0, META_PAD - N_META), (0, 0)))

        u, cq, ckv, kr, gates = _inproj(h, g_mix, p["w_in"], qag, kvag, tb)
        gc = _conv_branch(u.reshape(bsz, seq, D_CONV), um, *conv_args, gates.reshape(bsz, seq, 2 * d))
        q, k, v = _qkv(cq, ckv, kr, p["wq"], p["wk"], p["wv"], p["qg"], p["kg"],
                       cos[N_META:], sin[N_META:], tb)
        hp = N_HEADS * HEAD_PAD
        o = _attention(q.reshape(bsz, seq, hp), k.reshape(bsz, seq, hp),
                       v.reshape(bsz, seq, N_HEADS * V_DIM), km_p, vm_p, tq)
        h1 = _merge(h, o.reshape(t, N_HEADS * V_DIM), gc.reshape(t, d), gates, p["w_mla_out"],
                    p["w_out"], tb)
        xt, s = _peer_scores(h1, _row(ffn_norm_g[l]), p["peer_wqt"], p["peer_keys"], tb)
        r2, cnt, e1, e2 = _peer_topk(s)
        h = _peer_dense(h1, xt, p["peer_u"], p["peer_vt"], r2, cnt, e1, e2, dense_lanes, rows_per_step)

        if not last:
            gcm = _conv_branch(um.reshape(1, N_META, D_CONV), zero_pre, *conv_args,
                               gatem.reshape(1, N_META, 2 * d))
            om = _attention(qm.reshape(1, N_META, hp), None, None, km_p, vm_p, N_META)
            hm1 = _merge(hm, om.reshape(N_META, N_HEADS * V_DIM), gcm.reshape(N_META, d), gatem,
                         p["w_mla_out"], p["w_out"], N_META)
            hm1_p = jnp.pad(hm1, ((0, LANES - N_META), (0, 0)))
            xtm, sm = _peer_scores(hm1_p, _row(ffn_norm_g[l]), p["peer_wqt"], p["peer_keys"], LANES)
            r2m, cntm, e1m, e2m = _peer_topk(sm)
            hm = _peer_dense(hm1_p, xtm, p["peer_u"], p["peer_vt"], r2m, cntm, e1m, e2m, LANES,
                             rows_per_step)[:N_META]

    return h.reshape(bsz, seq, d)
```

```python
import functools
import math

import jax
import jax.numpy as jnp
from jax import lax
from jax.experimental import pallas as pl
from jax.experimental.pallas import tpu as pltpu

F32 = jnp.float32
BF16 = jnp.bfloat16

CHUNK = 64
N_META = 16
D_CONV = 512
CONV_WIDTH = 31
N_HEADS = 8
Q_LORA = 256
KV_LORA = 128
NOPE_DIM = 64
ROPE_DIM = 32
QK_DIM = NOPE_DIM + ROPE_DIM
V_DIM = 64
ROPE_THETA = 10000.0
PEER_HEADS = 8
PEER_HALF = 128
N_KEYS = 128
PEER_TOPK = 16
EPS = 1e-6
MASK_VALUE = -1e30

LANES = 128
SUBLANES = 8
VMEM_LIMIT = 48 * 1024 * 1024

HEAD_PAD = LANES
ROPE_LO = NOPE_DIM
ROPE_HALF = ROPE_DIM // 2
META_PAD = LANES
NEG_INF = float("-inf")


def _cparams(*sem):
    return pltpu.CompilerParams(dimension_semantics=sem, vmem_limit_bytes=VMEM_LIMIT)


def _rms(x, g, n, valid=None):
    xs = x if valid is None else jnp.where(valid, x, 0.0)
    ms = jnp.sum(xs * xs, axis=-1, keepdims=True) * (1.0 / n)
    return x * lax.rsqrt(ms + EPS) * g


_C_CONV = 0
_C_Q = 2 * D_CONV
_C_KV = _C_Q + Q_LORA
_C_ROPE = _C_KV + KV_LORA
_C_GATE = _C_ROPE + HEAD_PAD


def _inproj_kernel(h_ref, g_ref, w_ref, qg_ref, kvg_ref, u_ref, cq_ref, ckv_ref, kr_ref, gate_ref):
    d = h_ref.shape[-1]
    xn = _rms(h_ref[...], g_ref[...], d).astype(BF16)

    def proj(lo, hi):
        return jnp.dot(xn, w_ref[:, lo:hi], preferred_element_type=F32)

    a = proj(_C_CONV, _C_CONV + D_CONV)
    b = proj(_C_CONV + D_CONV, _C_Q)
    u_ref[...] = a * jax.nn.sigmoid(b)
    cq_ref[...] = _rms(proj(_C_Q, _C_KV), qg_ref[...], Q_LORA).astype(BF16)
    ckv_ref[...] = _rms(proj(_C_KV, _C_ROPE), kvg_ref[...], KV_LORA).astype(BF16)
    kr_ref[...] = proj(_C_ROPE, _C_GATE)
    gate_ref[...] = jax.nn.sigmoid(proj(_C_GATE, _C_GATE + 2 * d)).astype(BF16)


def _inproj(h, g, w, qg, kvg, tb):
    t, d = h.shape
    n = w.shape[1]
    row = lambda c: pl.BlockSpec((tb, c), lambda i: (i, 0))
    full = lambda a: pl.BlockSpec(a.shape, lambda i: (0,) * a.ndim)
    return pl.pallas_call(
        _inproj_kernel,
        grid=(t // tb,),
        in_specs=[row(d), full(g), full(w), full(qg), full(kvg)],
        out_specs=[row(D_CONV), row(Q_LORA), row(KV_LORA), row(HEAD_PAD), row(2 * d)],
        out_shape=[jax.ShapeDtypeStruct((t, D_CONV), F32),
                   jax.ShapeDtypeStruct((t, Q_LORA), BF16),
                   jax.ShapeDtypeStruct((t, KV_LORA), BF16),
                   jax.ShapeDtypeStruct((t, HEAD_PAD), F32),
                   jax.ShapeDtypeStruct((t, 2 * d), BF16)],
        compiler_params=_cparams("parallel"),
        name="inproj",
    )(h, g, w, qg, kvg)


_CONV_PAD = 32
_CONV_TILE = 64


def _conv_kernel(u_ref, pre_ref, cw_ref, cb_ref, lg_ref, lb_ref, wo_ref, gate_ref, o_ref,
                 buf_ref, act_ref, sh_ref, *, rows):
    s = u_ref.shape[1]
    npre = pre_ref.shape[0]
    buf_ref[0:_CONV_PAD - npre, :] = jnp.zeros((_CONV_PAD - npre, D_CONV), F32)
    buf_ref[_CONV_PAD - npre:_CONV_PAD, :] = pre_ref[...]
    buf_ref[_CONV_PAD:, :] = u_ref[0]
    shift = _CONV_PAD - (CONV_WIDTH - 1)

    def tile(i, carry):
        r0 = pl.multiple_of(i * rows, rows)
        acc = jnp.zeros((rows, D_CONV), F32) + cb_ref[...]
        win = buf_ref[pl.ds(r0, rows + _CONV_PAD), :]
        for res in range(1, SUBLANES):
            sh_ref[res - 1] = win[res:res + rows + _CONV_PAD - SUBLANES, :]
        for k in range(CONV_WIDTH):
            res, off = (shift + k) % SUBLANES, (shift + k) // SUBLANES * SUBLANES
            x = win[off:off + rows, :] if res == 0 else sh_ref[res - 1, off:off + rows, :]
            acc = acc + cw_ref[k:k + 1, :] * x
        mu = jnp.mean(acc, axis=-1, keepdims=True)
        xc = acc - mu
        var = jnp.mean(xc * xc, axis=-1, keepdims=True)
        y = xc * lax.rsqrt(var + EPS) * lg_ref[...] + lb_ref[...]
        act_ref[pl.ds(r0, rows), :] = (y * jax.nn.sigmoid(y)).astype(BF16)
        return carry

    lax.fori_loop(0, s // rows, tile, 0)
    y = jnp.dot(act_ref[...], wo_ref[...], preferred_element_type=F32)
    o_ref[0] = (gate_ref[0].astype(F32) * y).astype(BF16)


def _conv_branch(u, pre, cw, cb, lg, lb, wo, gates):
    b, s, _ = u.shape
    d = wo.shape[1]
    rows = min(_CONV_TILE, s)
    full = lambda a: pl.BlockSpec(a.shape, lambda i: (0,) * a.ndim)
    return pl.pallas_call(
        functools.partial(_conv_kernel, rows=rows),
        grid=(b,),
        in_specs=[pl.BlockSpec((1, s, D_CONV), lambda i: (i, 0, 0)), full(pre), full(cw), full(cb),
                  full(lg), full(lb), full(wo), pl.BlockSpec((1, s, d), lambda i: (i, 0, 0))],
        out_specs=pl.BlockSpec((1, s, d), lambda i: (i, 0, 0)),
        out_shape=jax.ShapeDtypeStruct((b, s, d), BF16),
        scratch_shapes=[pltpu.VMEM((_CONV_PAD + s, D_CONV), F32), pltpu.VMEM((s, D_CONV), BF16),
                        pltpu.VMEM((SUBLANES - 1, rows + _CONV_PAD - SUBLANES, D_CONV), F32)],
        compiler_params=_cparams("parallel"),
        name="conv_branch",
    )(u, pre, cw, cb, lg, lb, wo, gates)


def _rope_group(x, cos, sin_signed):
    return x * cos + pltpu.roll(x, LANES - ROPE_HALF, 1) * sin_signed


def _qkv_kernel(cq_ref, ckv_ref, kr_ref, wq_ref, wk_ref, wv_ref, qg_ref, kg_ref, cos_ref, sin_ref,
                q_ref, k_ref, v_ref):
    cos = cos_ref[...]
    sin = sin_ref[...]
    qf = jnp.dot(cq_ref[...], wq_ref[...], preferred_element_type=F32)
    kf = jnp.dot(ckv_ref[...], wk_ref[...], preferred_element_type=F32)
    kr = kr_ref[...]
    scale = QK_DIM ** -0.5 * math.log2(math.e)
    real = lax.broadcasted_iota(jnp.int32, kr.shape, 1) < QK_DIM
    for hd in range(N_HEADS):
        grp = slice(hd * HEAD_PAD, (hd + 1) * HEAD_PAD)
        qn = _rms(qf[:, grp], qg_ref[...], QK_DIM, real)
        q_ref[:, grp] = (_rope_group(qn, cos, sin) * scale).astype(BF16)
        kn = _rms(kf[:, grp] + kr, kg_ref[...], QK_DIM, real)
        k_ref[:, grp] = _rope_group(kn, cos, sin).astype(BF16)
    v_ref[...] = jnp.dot(ckv_ref[...], wv_ref[...], preferred_element_type=F32).astype(BF16)


def _qkv(cq, ckv, kr, wq, wk, wv, qg, kg, cos, sin, tb):
    t = cq.shape[0]
    nrope = cos.shape[0] // tb
    row = lambda c: pl.BlockSpec((tb, c), lambda i: (i, 0))
    full = lambda a: pl.BlockSpec(a.shape, lambda i: (0,) * a.ndim)
    rope = pl.BlockSpec((tb, HEAD_PAD), lambda i: (i % nrope, 0))
    hp = N_HEADS * HEAD_PAD
    return pl.pallas_call(
        _qkv_kernel,
        grid=(t // tb,),
        in_specs=[row(Q_LORA), row(KV_LORA), row(HEAD_PAD), full(wq), full(wk), full(wv),
                  full(qg), full(kg), rope, rope],
        out_specs=[row(hp), row(hp), row(N_HEADS * V_DIM)],
        out_shape=[jax.ShapeDtypeStruct((t, hp), BF16), jax.ShapeDtypeStruct((t, hp), BF16),
                   jax.ShapeDtypeStruct((t, N_HEADS * V_DIM), BF16)],
        compiler_params=_cparams("parallel"),
        name="qkv",
    )(cq, ckv, kr, wq, wk, wv, qg, kg, cos, sin)


_ATTN_PAIRS = 2


def _attn_kernel(*refs, tq, has_real):
    if has_real:
        q_ref, k_ref, v_ref, km_ref, vm_ref, vis_ref, o_ref = refs
    else:
        q_ref, km_ref, vm_ref, o_ref = refs
    qi = pl.program_id(2)
    nt = (((1,), (1,)), ((), ()))
    heads = range(2 * _ATTN_PAIRS)
    groups = [slice(hh * HEAD_PAD, (hh + 1) * HEAD_PAD) for hh in heads]
    vcols = [slice((hh // 2) * 2 * V_DIM, (hh // 2 + 1) * 2 * V_DIM) for hh in heads]
    qs = [q_ref[0, :, grp] for grp in groups]

    def step(s, v, carry):
        m, l, acc = carry
        m_new = jnp.maximum(m, jnp.max(s, axis=-1, keepdims=True))
        alpha = jnp.exp2(m - m_new)
        p = jnp.exp2(s - m_new)
        l = alpha * l + jnp.sum(p, axis=-1, keepdims=True)
        acc = alpha * acc + jnp.dot(p.astype(BF16), v, preferred_element_type=F32)
        return m_new, l, acc

    state = []
    for q, grp, vc in zip(qs, groups, vcols):
        s = lax.dot_general(q, km_ref[:, grp], nt, preferred_element_type=F32)
        col = lax.broadcasted_iota(jnp.int32, s.shape, 1)
        s = jnp.where(col < N_META, s, MASK_VALUE)
        m = jnp.max(s, axis=-1, keepdims=True)
        p = jnp.exp2(s - m)
        l = jnp.sum(p, axis=-1, keepdims=True)
        state.append((m, l, jnp.dot(p.astype(BF16), vm_ref[:, vc], preferred_element_type=F32)))
    state = tuple(state)

    if has_real:
        def block(r0, carry, masked):
            out = []
            for q, grp, vc, c in zip(qs, groups, vcols, carry):
                s = lax.dot_general(q, k_ref[0, pl.ds(r0, tq), grp], nt, preferred_element_type=F32)
                if masked:
                    s = jnp.where(vis_ref[...] > 0.0, s, MASK_VALUE)
                out.append(step(s, v_ref[0, pl.ds(r0, tq), vc], c))
            return tuple(out)

        state = lax.fori_loop(0, qi, lambda kb, c: block(pl.multiple_of(kb * tq, tq), c, False), state)
        state = block(pl.multiple_of(qi * tq, tq), state, True)
    outs = [acc / l for (_, l, acc) in state]
    lane = lax.broadcasted_iota(jnp.int32, outs[0].shape, 1)
    pairs = [jnp.where(lane < V_DIM, outs[2 * p], outs[2 * p + 1]) for p in range(_ATTN_PAIRS)]
    o_ref[0] = jnp.concatenate(pairs, axis=-1).astype(BF16)


def _attention(q, k, v, km, vm, tq):
    b, sq, _ = q.shape
    has_real = k is not None
    qw = 2 * _ATTN_PAIRS * HEAD_PAD
    vw = 2 * _ATTN_PAIRS * V_DIM
    qspec = pl.BlockSpec((1, tq, qw), lambda bi, hp, qi: (bi, qi, hp))
    mk = pl.BlockSpec((META_PAD, qw), lambda bi, hp, qi: (0, hp))
    mv = pl.BlockSpec((META_PAD, vw), lambda bi, hp, qi: (0, hp))
    if has_real:
        s = k.shape[1]
        chunk = jnp.arange(tq, dtype=jnp.int32) // CHUNK
        vis = (chunk[None, :] <= chunk[:, None]).astype(F32)
        in_specs = [qspec, pl.BlockSpec((1, s, qw), lambda bi, hp, qi: (bi, 0, hp)),
                    pl.BlockSpec((1, s, vw), lambda bi, hp, qi: (bi, 0, hp)), mk, mv,
                    pl.BlockSpec((tq, tq), lambda bi, hp, qi: (0, 0))]
        args = (q, k, v, km, vm, vis)
    else:
        in_specs = [qspec, mk, mv]
        args = (q, km, vm)
    return pl.pallas_call(
        functools.partial(_attn_kernel, tq=tq, has_real=has_real),
        grid=(b, N_HEADS // (2 * _ATTN_PAIRS), sq // tq),
        in_specs=in_specs,
        out_specs=pl.BlockSpec((1, tq, vw), lambda bi, hp, qi: (bi, qi, hp)),
        out_shape=jax.ShapeDtypeStruct((b, sq, N_HEADS * V_DIM), BF16),
        compiler_params=_cparams("parallel", "parallel", "arbitrary"),
        name="attention",
    )(*args)


def _merge_kernel(h_ref, o_ref, gc_ref, g2_ref, wm_ref, wo_ref, out_ref):
    ymla = jnp.dot(o_ref[...], wm_ref[...], preferred_element_type=F32)
    merged = gc_ref[...].astype(F32) + g2_ref[...].astype(F32) * ymla
    out_ref[...] = h_ref[...] + jnp.dot(merged.astype(BF16), wo_ref[...], preferred_element_type=F32)


def _merge(h, o, gc, gates, wm, wo, tb):
    t, d = h.shape
    row = lambda c: pl.BlockSpec((tb, c), lambda i: (i, 0))
    full = lambda a: pl.BlockSpec(a.shape, lambda i: (0,) * a.ndim)
    return pl.pallas_call(
        _merge_kernel,
        grid=(t // tb,),
        in_specs=[row(d), row(N_HEADS * V_DIM), row(d), pl.BlockSpec((tb, d), lambda i: (i, 1)),
                  full(wm), full(wo)],
        out_specs=row(d),
        out_shape=jax.ShapeDtypeStruct((t, d), F32),
        compiler_params=_cparams("parallel"),
        name="merge_out",
    )(h, o, gc, gates, wm, wo)


def _peer_score_kernel(h_ref, g_ref, wq_ref, keys_ref, xt_ref, s_ref):
    d = h_ref.shape[-1]
    xn = _rms(h_ref[...], g_ref[...], d)
    xt = xn.T.astype(BF16)
    xt_ref[...] = xt
    qt = jnp.dot(wq_ref[...], xt, preferred_element_type=F32)
    for g in range(2 * PEER_HEADS):
        qg = qt[g * PEER_HALF:(g + 1) * PEER_HALF, :].astype(BF16)
        s_ref[g] = jnp.dot(keys_ref[g], qg, preferred_element_type=F32)


def _peer_scores(h, g, wqt, keys, tb):
    t, d = h.shape
    full = lambda a: pl.BlockSpec(a.shape, lambda i: (0,) * a.ndim)
    ng = 2 * PEER_HEADS
    return pl.pallas_call(
        _peer_score_kernel,
        grid=(t // tb,),
        in_specs=[pl.BlockSpec((tb, d), lambda i: (i, 0)), full(g), full(wqt), full(keys)],
        out_specs=[pl.BlockSpec((d, tb), lambda i: (0, i)),
                   pl.BlockSpec((ng, N_KEYS, tb), lambda i: (0, 0, i))],
        out_shape=[jax.ShapeDtypeStruct((d, t), BF16), jax.ShapeDtypeStruct((ng, N_KEYS, t), F32)],
        compiler_params=_cparams("parallel"),
        name="peer_scores",
    )(h, g, wqt, keys)


def _extract_topk(vals, pos, sv_ref):
    big = float(vals.shape[0] * vals.shape[0])

    def body(a, carry):
        cur, rank = carry
        m = jnp.max(cur, axis=0, keepdims=True)
        first = jnp.min(jnp.where(cur == m, pos, big), axis=0, keepdims=True)
        hit = pos == first
        if sv_ref is not None:
            sv_ref[pl.ds(a, 1), :] = m
        return jnp.where(hit, NEG_INF, cur), jnp.where(hit, lax.convert_element_type(a, F32), rank)

    init = (vals, jnp.full(vals.shape, float(PEER_TOPK), F32))
    return lax.fori_loop(0, PEER_TOPK, body, init)[1]


def _topk_exact_tile(s1, s2, sv1_ref, sv2_ref):
    lanes = s1.shape[-1]
    k = PEER_TOPK
    key_pos = lax.broadcasted_iota(jnp.int32, (N_KEYS, lanes), 0).astype(F32)
    crow = lax.broadcasted_iota(jnp.int32, (k * k, lanes), 0)
    cand_pos = ((crow % k) * k + crow // k).astype(F32)
    r1 = _extract_topk(s1, key_pos, sv1_ref)
    r2 = _extract_topk(s2, key_pos, sv2_ref)
    sv1 = sv1_ref[...]
    sv2 = sv2_ref[...]
    cand = jnp.concatenate([sv1 + sv2[b:b + 1, :] for b in range(k)], axis=0)
    sel = _extract_topk(cand, cand_pos, None) < float(k)
    t1 = jnp.exp(sv1 - sv1[0:1, :])
    t2 = jnp.exp(sv2 - sv2[0:1, :])
    z = jnp.zeros((1, lanes), F32)
    n_a = jnp.zeros((k, lanes), F32)
    for b in range(k):
        sb = sel[b * k:(b + 1) * k, :]
        z = z + jnp.sum(jnp.where(sb, t1 * t2[b:b + 1, :], 0.0), axis=0, keepdims=True)
        n_a = n_a + jnp.where(sb, 1.0, 0.0)
    cnt = jnp.zeros((N_KEYS, lanes), F32)
    for a in range(k):
        cnt = jnp.where(r1 == float(a), n_a[a:a + 1, :], cnt)
    return r2, cnt, jnp.exp(s1 - sv1[0:1, :]), jnp.exp(s2 - sv2[0:1, :]) / z


def _dup_bf16(x):
    bits = pltpu.bitcast(x.astype(BF16).astype(F32), jnp.uint32)
    return bits | (bits >> 16)


def _peer_topk_small_kernel(s_ref, r2_ref, cnt_ref, e1_ref, e2_ref, sv1_ref, sv2_ref):
    r2, cnt, e1, e2 = _topk_exact_tile(s_ref[0], s_ref[1], sv1_ref, sv2_ref)
    r2_ref[0] = r2.astype(BF16)
    cnt_ref[0] = _dup_bf16(cnt)
    e1_ref[0] = _dup_bf16(e1)
    e2_ref[0] = e2.astype(BF16)


def _batcher_pairs(n):
    pairs = []
    p = 1
    while p < n:
        k = p
        while k >= 1:
            for j in range(k % p, n - k, 2 * k):
                for i in range(min(k, n - j - k)):
                    if (i + j) // (2 * p) == (i + j + k) // (2 * p):
                        pairs.append((i + j, i + j + k))
            k //= 2
        p *= 2
    return pairs


_SORT16 = _batcher_pairs(PEER_TOPK)
_STAIR = [(a, b) for a in range(PEER_TOPK) for b in range(PEER_TOPK) if (a + 1) * (b + 1) <= PEER_TOPK]
_TILES = LANES * SUBLANES


def _top16_desc(load, lo, n):
    k = PEER_TOPK
    if n == k:
        v = [load(lo + i) for i in range(k)]
        for i, j in _SORT16:
            v[i], v[j] = jnp.maximum(v[i], v[j]), jnp.minimum(v[i], v[j])
        return v
    x = _top16_desc(load, lo, n // 2)
    y = _top16_desc(load, lo + n // 2, n // 2)
    c = [jnp.maximum(x[i], y[k - 1 - i]) for i in range(k)]
    d = k // 2
    while d >= 1:
        for i in range(k):
            if i & d == 0:
                c[i], c[i + d] = jnp.maximum(c[i], c[i + d]), jnp.minimum(c[i], c[i + d])
        d //= 2
    return c


def _sublane_transpose(vs):
    sub = lax.broadcasted_iota(jnp.int32, vs[0].shape, 0)
    d = SUBLANES // 2
    while d >= 1:
        low = (sub & d) == 0
        nxt = list(vs)
        for j in range(SUBLANES):
            if j & d == 0:
                x, y = vs[j], vs[j + d]
                nxt[j] = jnp.where(low, x, pltpu.roll(y, d, 0))
                nxt[j + d] = jnp.where(low, pltpu.roll(x, SUBLANES - d, 0), y)
        vs = nxt
        d //= 2
    return vs


def _peer_topk_kernel(s_ref, r2_ref, cnt_ref, e1_ref, e2_ref, slab_ref, sv_ref, u_ref, misc_ref,
                      sv1_ref, sv2_ref):
    k = PEER_TOPK
    inf = float("inf")
    for half in range(2):
        for kt in range(N_KEYS // SUBLANES):
            rows = slice(kt * SUBLANES, (kt + 1) * SUBLANES)
            tiles = [s_ref[half, rows, j * LANES:(j + 1) * LANES] for j in range(SUBLANES)]
            for r, slab in enumerate(_sublane_transpose(tiles)):
                slab_ref[half * N_KEYS + kt * SUBLANES + r] = slab
    sv1 = _top16_desc(lambda i: slab_ref[i], 0, N_KEYS)
    sv2 = _top16_desc(lambda i: slab_ref[i], N_KEYS, N_KEYS)
    for a in range(k):
        sv_ref[a] = sv1[a]
        sv_ref[k + a] = sv2[a]
    tie = jnp.zeros(sv1[0].shape, F32)
    for v in (sv1, sv2):
        for a in range(k - 1):
            tie = jnp.where(v[a] == v[a + 1], 1.0, tie)
    cand = {c: sv1[c[0]] + sv2[c[1]] for c in _STAIR}
    beaten = {c: float((c[0] + 1) * (c[1] + 1) - 1) for c in _STAIR}
    dyn = {c: None for c in _STAIR}
    for x_i, x in enumerate(_STAIR):
        for y in _STAIR[x_i + 1:]:
            if (x[0] < y[0]) == (x[1] < y[1]) or x[0] == y[0] or x[1] == y[1]:
                continue
            g = jnp.where(cand[x] >= cand[y], 1.0, 0.0)
            dyn[y] = g if dyn[y] is None else dyn[y] + g
            beaten[x] += 1.0
            dyn[x] = -g if dyn[x] is None else dyn[x] - g
    sel = {c: (dyn[c] + beaten[c] if dyn[c] is not None else jnp.full(tie.shape, beaten[c])) < float(k)
           for c in _STAIR}
    t1 = [jnp.exp(sv1[a] - sv1[0]) for a in range(k)]
    t2 = [jnp.exp(sv2[b] - sv2[0]) for b in range(k)]
    z = jnp.zeros(tie.shape, F32)
    for c in _STAIR:
        z = z + jnp.where(sel[c], t1[c[0]] * t2[c[1]], 0.0)
    for b in range(k):
        u = jnp.full(tie.shape, inf, F32)
        for a in range(k):
            if (a, b) in sel:
                u = jnp.minimum(u, jnp.where(sel[(a, b)], sv1[a], inf))
        u_ref[b] = u
    misc_ref[0] = sv1[0]
    misc_ref[1] = sv2[0]
    misc_ref[2] = 1.0 / z
    misc_ref[3] = tie

    rows2 = 2 * SUBLANES
    bad = []
    for tt in range(SUBLANES):
        cols = slice(tt * LANES, (tt + 1) * LANES)
        bc = lambda ref, i: jnp.broadcast_to(ref[i, tt:tt + 1, :], (rows2, LANES))
        svb = [bc(sv_ref, k + b) for b in range(k)]
        ub = [bc(u_ref, b) for b in range(k)]
        c1, c2, zinv, last1 = bc(misc_ref, 0), bc(misc_ref, 1), bc(misc_ref, 2), bc(sv_ref, k - 1)
        n1 = jnp.zeros((rows2, LANES), F32)
        n2 = jnp.zeros((rows2, LANES), F32)
        for kt in range(N_KEYS // rows2):
            rows = slice(kt * rows2, (kt + 1) * rows2)
            x1 = s_ref[0, rows, cols]
            x2 = s_ref[1, rows, cols]
            r = jnp.full((rows2, LANES), float(k), F32)
            c = jnp.full((rows2, LANES), float(k), F32)
            for b in reversed(range(k)):
                r = jnp.where(svb[b] <= x2, float(b), r)
                c = jnp.where(ub[b] > x1, float(b), c)
            r2_ref[0, rows, cols] = r.astype(BF16)
            cnt_ref[0, rows, cols] = _dup_bf16(c)
            e1_ref[0, rows, cols] = _dup_bf16(jnp.exp(x1 - c1))
            e2_ref[0, rows, cols] = (jnp.exp(x2 - c2) * zinv).astype(BF16)
            n1 = n1 + jnp.where(x1 >= last1, 1.0, 0.0)
            n2 = n2 + jnp.where(r < float(k), 1.0, 0.0)
        bad.append(jnp.abs(jnp.sum(n1, axis=0, keepdims=True) - k)
                   + jnp.abs(jnp.sum(n2, axis=0, keepdims=True) - k))

    any_tied = jnp.max(sum(bad)) + jnp.max(misc_ref[3]) > 0.0

    @pl.when(any_tied)
    def _():
        for tt in range(SUBLANES):
            tied = jnp.max(bad[tt]) + jnp.max(misc_ref[3, tt:tt + 1, :]) > 0.0

            @pl.when(tied)
            def _():
                cols = slice(tt * LANES, (tt + 1) * LANES)
                r2, cnt, e1, e2 = _topk_exact_tile(s_ref[0, :, cols], s_ref[1, :, cols], sv1_ref, sv2_ref)
                r2_ref[0, :, cols] = r2.astype(BF16)
                cnt_ref[0, :, cols] = _dup_bf16(cnt)
                e1_ref[0, :, cols] = _dup_bf16(e1)
                e2_ref[0, :, cols] = e2.astype(BF16)


def _peer_topk(s):
    ng, nk, t = s.shape
    fast = t % _TILES == 0
    lanes = _TILES if fast else LANES
    spec = pl.BlockSpec((1, nk, lanes), lambda i, hd: (hd, 0, i))
    shp = lambda dt: jax.ShapeDtypeStruct((PEER_HEADS, nk, t), dt)
    slab = lambda n: pltpu.VMEM((n, SUBLANES, LANES), F32)
    row = pltpu.VMEM((PEER_TOPK, LANES), F32)
    scratch = [slab(2 * N_KEYS), slab(2 * PEER_TOPK), slab(PEER_TOPK), slab(4), row, row] if fast else [row, row]
    return pl.pallas_call(
        _peer_topk_kernel if fast else _peer_topk_small_kernel,
        grid=(t // lanes, PEER_HEADS),
        in_specs=[pl.BlockSpec((2, nk, lanes), lambda i, hd: (hd, 0, i))],
        out_specs=[spec, spec, spec, spec],
        out_shape=[shp(BF16), shp(jnp.uint32), shp(jnp.uint32), shp(BF16)],
        scratch_shapes=scratch,
        compiler_params=_cparams("parallel", "parallel"),
        name="peer_topk",
    )(s)


def _gelu_tanh(x):
    c = -2.0 * math.sqrt(2.0 / math.pi) * math.log2(math.e)
    return x / (1.0 + jnp.exp2(x * (c + (c * 0.044715) * (x * x))))


_DENSE_CHUNK_ROWS = 8
_DENSE_KW = 512


def _peer_dense_kernel(*refs, rows_per_step, n_u, n_v):
    h_ref, xt_ref = refs[:2]
    u_refs = refs[2:2 + n_u]
    vt_refs = refs[2 + n_u:2 + n_u + n_v]
    r2_ref, cnt_ref, e1_ref, e2_ref, o_ref, acc_ref = refs[2 + n_u + n_v:]
    e = pl.program_id(1)
    lanes = xt_ref.shape[-1]
    bf16_rows = 2 * SUBLANES
    kw = _DENSE_KW

    @pl.when(e == 0)
    def _():
        acc_ref[...] = jnp.zeros_like(acc_ref)

    ce = _DENSE_CHUNK_ROWS * N_KEYS
    n_chunks = rows_per_step // _DENSE_CHUNK_ROWS
    zero = jnp.zeros((), BF16)

    def scores(c):
        out = None
        for k, u_ref in enumerate(u_refs):
            part = jnp.dot(u_ref[c * ce:(c + 1) * ce, :], xt_ref[k * kw:(k + 1) * kw, :],
                           preferred_element_type=F32)
            out = part if out is None else part + out
        return out

    def packed_row(ref, hd, i):
        word = jnp.broadcast_to(ref[hd, pl.ds(i, 1), :], (SUBLANES, lanes))
        return jnp.tile(pltpu.bitcast(word, BF16), (N_KEYS // bf16_rows, 1))

    a_next = scores(0)
    total = None
    for c in range(n_chunks):
        a = a_next
        ws = []
        for ii in range(_DENSE_CHUNK_ROWS):
            i = e * rows_per_step + c * _DENSE_CHUNK_ROWS + ii
            w = None
            for hd in range(PEER_HEADS):
                term = packed_row(e1_ref, hd, i) * jnp.where(r2_ref[hd] < packed_row(cnt_ref, hd, i),
                                                             e2_ref[hd], zero)
                w = term if w is None else w + term
            ws.append(w)
        if c + 1 < n_chunks:
            a_next = scores(c + 1)
        act = _gelu_tanh(a.astype(BF16)) * jnp.concatenate(ws, axis=0)
        for k in range(ce // kw):
            part = jnp.dot(vt_refs[c * (ce // kw) + k][...], act[k * kw:(k + 1) * kw, :],
                           preferred_element_type=F32)
            total = part if total is None else part + total
    acc_ref[...] += total

    @pl.when(e == pl.num_programs(1) - 1)
    def _():
        o_ref[...] = h_ref[...] + acc_ref[...].T


def _peer_dense(h, xt, u, vt, layer, r2, cnt, e1, e2, lanes, rows_per_step):
    t, d = h.shape
    ne = u.shape[1]
    eb = rows_per_step * N_KEYS
    kw = _DENSE_KW
    n_u, n_v = d // kw, eb // kw
    tok = pl.BlockSpec((PEER_HEADS, N_KEYS, lanes), lambda i, e: (0, 0, i))
    u_specs = [pl.BlockSpec((None, eb, kw), functools.partial(lambda i, e, k: (layer, e, k), k=k))
               for k in range(n_u)]
    v_specs = [pl.BlockSpec((None, d, kw), functools.partial(lambda i, e, k: (layer, 0, e * n_v + k), k=k))
               for k in range(n_v)]
    return pl.pallas_call(
        functools.partial(_peer_dense_kernel, rows_per_step=rows_per_step, n_u=n_u, n_v=n_v),
        grid=(t // lanes, ne // eb),
        in_specs=[pl.BlockSpec((lanes, d), lambda i, e: (i, 0)),
                  pl.BlockSpec((d, lanes), lambda i, e: (0, i))] + u_specs + v_specs + [tok, tok, tok, tok],
        out_specs=pl.BlockSpec((lanes, d), lambda i, e: (i, 0)),
        out_shape=jax.ShapeDtypeStruct((t, d), F32),
        scratch_shapes=[pltpu.VMEM((d, lanes), F32)],
        compiler_params=_cparams("parallel", "arbitrary"),
        name="peer_dense",
    )(h, xt, *([u] * n_u), *([vt] * n_v), r2, cnt, e1, e2)


def _pad_groups(w, real, n_groups):
    k = w.shape[0]
    w = w.reshape(k, n_groups, real)
    return jnp.pad(w, ((0, 0), (0, 0), (0, HEAD_PAD - real))).reshape(k, n_groups * HEAD_PAD)


def _with_rope_copy(w):
    first = w[..., ROPE_LO:ROPE_LO + ROPE_HALF]
    pad = jnp.zeros(w.shape[:-1] + (HEAD_PAD - QK_DIM - ROPE_HALF,), w.dtype)
    return jnp.concatenate([w, first, pad], axis=-1)


def _layer_params(l, w_in, w_conv_out, w_uq, w_ukv, q_norm_g, k_norm_g, w_mla_out, w_out, peer_wq,
                  peer_keys):
    d = w_in.shape[1]
    wi = w_in[l]
    o0 = 2 * D_CONV
    o1 = o0 + Q_LORA
    o2 = o1 + KV_LORA
    o3 = o2 + ROPE_DIM
    rope_cols = _with_rope_copy(jnp.pad(wi[:, o2:o3], ((0, 0), (ROPE_LO, 0))))
    w_in_p = jnp.concatenate([wi[:, :o2], rope_cols, wi[:, o3:]], axis=1).astype(BF16)
    wkv = w_ukv[l].reshape(KV_LORA, N_HEADS, NOPE_DIM + V_DIM)
    wk = _pad_groups(wkv[:, :, :NOPE_DIM].reshape(KV_LORA, N_HEADS * NOPE_DIM), NOPE_DIM, N_HEADS)
    wv = wkv[:, :, NOPE_DIM:].reshape(KV_LORA, N_HEADS * V_DIM)
    padg = lambda g: _with_rope_copy(g).reshape(1, HEAD_PAD)
    return dict(
        w_in=w_in_p,
        w_conv_out=w_conv_out[l].astype(BF16),
        wq=_with_rope_copy(w_uq[l].reshape(Q_LORA, N_HEADS, QK_DIM)).reshape(Q_LORA, N_HEADS * HEAD_PAD).astype(BF16),
        wk=wk.astype(BF16),
        wv=wv.astype(BF16),
        qg=padg(q_norm_g[l]),
        kg=padg(k_norm_g[l]),
        w_mla_out=w_mla_out[l].astype(BF16),
        w_out=w_out[l].astype(BF16),
        peer_wqt=peer_wq[l],
        peer_keys=peer_keys[l].reshape(2 * PEER_HEADS, N_KEYS, PEER_HALF).astype(BF16),
    )


def _rope_tables(length):
    pos = jnp.arange(length, dtype=F32)
    inv = 1.0 / (ROPE_THETA ** (jnp.arange(0, ROPE_DIM, 2, dtype=F32) / ROPE_DIM))
    ang = pos[:, None] * inv[None, :]
    cos = jnp.concatenate([jnp.cos(ang), jnp.cos(ang)], axis=-1)
    sin = jnp.concatenate([-jnp.sin(ang), jnp.sin(ang)], axis=-1)
    cos = jnp.pad(cos, ((0, 0), (ROPE_LO, 0)), constant_values=1.0)
    cos = jnp.pad(cos, ((0, 0), (0, HEAD_PAD - ROPE_LO - ROPE_DIM)))
    sin = jnp.pad(sin, ((0, 0), (ROPE_LO, HEAD_PAD - ROPE_LO - ROPE_DIM)))
    return cos, sin


def _row(v):
    return v.reshape(1, -1)


def kernel(x, meta_tokens, mix_norm_g, w_in, conv_w, conv_b, conv_ln_g, conv_ln_b, w_conv_out,
           q_a_norm_g, w_uq, kv_a_norm_g, w_ukv, q_norm_g, k_norm_g, w_mla_out, w_out, ffn_norm_g,
           peer_wq, peer_keys, peer_u, peer_v):
    bsz, seq, d = x.shape
    depth = w_in.shape[0]
    t = bsz * seq
    assert meta_tokens.shape[0] == N_META and seq % 256 == 0 and d % LANES == 0
    tb = min(512, seq)
    tq = min(512, seq)
    dense_lanes = min(512, t)
    rows_per_step = 16

    cos, sin = _rope_tables(N_META + seq)
    h = x.reshape(t, d)
    hm = meta_tokens.astype(x.dtype)
    zero_pre = jnp.zeros((N_META, D_CONV), F32)

    peer_wqt = jnp.swapaxes(peer_wq, 1, 2).astype(BF16)
    peer_ub = peer_u.astype(BF16)
    peer_vt = jnp.swapaxes(peer_v, 1, 2).astype(BF16)

    for l in range(depth):
        p = _layer_params(l, w_in, w_conv_out, w_uq, w_ukv, q_norm_g, k_norm_g, w_mla_out, w_out,
                          peer_wqt, peer_keys)
        last = l == depth - 1
        g_mix, qag, kvag = _row(mix_norm_g[l]), _row(q_a_norm_g[l]), _row(kv_a_norm_g[l])
        conv_args = (conv_w[l], _row(conv_b[l]), _row(conv_ln_g[l]), _row(conv_ln_b[l]), p["w_conv_out"])

        um, cqm, ckvm, krm, gatem = _inproj(hm, g_mix, p["w_in"], qag, kvag, N_META)
        qm, km, vm = _qkv(cqm, ckvm, krm, p["wq"], p["wk"], p["wv"], p["qg"], p["kg"],
                          cos[:N_META], sin[:N_META], N_META)
        km_p = jnp.pad(km, ((0, META_PAD - N_META), (0, 0)))
        vm_p = jnp.pad(vm, ((0, META_PAD - N_META), (0, 0)))

        u, cq, ckv, kr, gates = _inproj(h, g_mix, p["w_in"], qag, kvag, tb)
        gc = _conv_branch(u.reshape(bsz, seq, D_CONV), um, *conv_args, gates.reshape(bsz, seq, 2 * d))
        q, k, v = _qkv(cq, ckv, kr, p["wq"], p["wk"], p["wv"], p["qg"], p["kg"],
                       cos[N_META:], sin[N_META:], tb)
        hp = N_HEADS * HEAD_PAD
        o = _attention(q.reshape(bsz, seq, hp), k.reshape(bsz, seq, hp),
                       v.reshape(bsz, seq, N_HEADS * V_DIM), km_p, vm_p, tq)
        h1 = _merge(h, o.reshape(t, N_HEADS * V_DIM), gc.reshape(t, d), gates, p["w_mla_out"],
                    p["w_out"], tb)
        xt, s = _peer_scores(h1, _row(ffn_norm_g[l]), p["peer_wqt"], p["peer_keys"], tb)
        r2, cnt, e1, e2 = _peer_topk(s)
        h = _peer_dense(h1, xt, peer_ub, peer_vt, l, r2, cnt, e1, e2, dense_lanes, rows_per_step)

        if not last:
            gcm = _conv_branch(um.reshape(1, N_META, D_CONV), zero_pre, *conv_args,
                               gatem.reshape(1, N_META, 2 * d))
            om = _attention(qm.reshape(1, N_META, hp), None, None, km_p, vm_p, N_META)
            hm1 = _merge(hm, om.reshape(N_META, N_HEADS * V_DIM), gcm.reshape(N_META, d), gatem,
                         p["w_mla_out"], p["w_out"], N_META)
            hm1_p = jnp.pad(hm1, ((0, LANES - N_META), (0, 0)))
            xtm, sm = _peer_scores(hm1_p, _row(ffn_norm_g[l]), p["peer_wqt"], p["peer_keys"], LANES)
            r2m, cntm, e1m, e2m = _peer_topk(sm)
            hm = _peer_dense(hm1_p, xtm, peer_ub, peer_vt, l, r2m, cntm, e1m, e2m, LANES,
                             rows_per_step)[:N_META]

    return h.reshape(bsz, seq, d)
```

```python
import functools
import math

import jax
import jax.numpy as jnp
from jax import lax
from jax.experimental import pallas as pl
from jax.experimental.pallas import tpu as pltpu

F32 = jnp.float32
BF16 = jnp.bfloat16

CHUNK = 64
N_META = 16
D_CONV = 512
CONV_WIDTH = 31
N_HEADS = 8
Q_LORA = 256
KV_LORA = 128
NOPE_DIM = 64
ROPE_DIM = 32
QK_DIM = NOPE_DIM + ROPE_DIM
V_DIM = 64
ROPE_THETA = 10000.0
PEER_HEADS = 8
PEER_HALF = 128
N_KEYS = 128
PEER_TOPK = 16
EPS = 1e-6
MASK_VALUE = -1e30

LANES = 128
SUBLANES = 8
VMEM_LIMIT = 48 * 1024 * 1024

HEAD_PAD = LANES
ROPE_LO = NOPE_DIM
ROPE_HALF = ROPE_DIM // 2
META_PAD = LANES
NEG_INF = float("-inf")


def _cparams(*sem):
    return pltpu.CompilerParams(dimension_semantics=sem, vmem_limit_bytes=VMEM_LIMIT)


def _rms(x, g, n, valid=None):
    xs = x if valid is None else jnp.where(valid, x, 0.0)
    ms = jnp.sum(xs * xs, axis=-1, keepdims=True) * (1.0 / n)
    return x * lax.rsqrt(ms + EPS) * g


_C_CONV = 0
_C_Q = 2 * D_CONV
_C_KV = _C_Q + Q_LORA
_C_ROPE = _C_KV + KV_LORA
_C_GATE = _C_ROPE + HEAD_PAD


def _inproj_kernel(h_ref, g_ref, w_ref, qg_ref, kvg_ref, u_ref, cq_ref, ckv_ref, kr_ref, gate_ref):
    d = h_ref.shape[-1]
    xn = _rms(h_ref[...], g_ref[...], d).astype(BF16)

    def proj(lo, hi):
        return jnp.dot(xn, w_ref[:, lo:hi], preferred_element_type=F32)

    a = proj(_C_CONV, _C_CONV + D_CONV)
    b = proj(_C_CONV + D_CONV, _C_Q)
    u_ref[...] = a * jax.nn.sigmoid(b)
    cq_ref[...] = _rms(proj(_C_Q, _C_KV), qg_ref[...], Q_LORA).astype(BF16)
    ckv_ref[...] = _rms(proj(_C_KV, _C_ROPE), kvg_ref[...], KV_LORA).astype(BF16)
    kr_ref[...] = proj(_C_ROPE, _C_GATE)
    gate_ref[...] = jax.nn.sigmoid(proj(_C_GATE, _C_GATE + 2 * d)).astype(BF16)


def _inproj(h, g, w, qg, kvg, tb):
    t, d = h.shape
    n = w.shape[1]
    row = lambda c: pl.BlockSpec((tb, c), lambda i: (i, 0))
    full = lambda a: pl.BlockSpec(a.shape, lambda i: (0,) * a.ndim)
    return pl.pallas_call(
        _inproj_kernel,
        grid=(t // tb,),
        in_specs=[row(d), full(g), full(w), full(qg), full(kvg)],
        out_specs=[row(D_CONV), row(Q_LORA), row(KV_LORA), row(HEAD_PAD), row(2 * d)],
        out_shape=[jax.ShapeDtypeStruct((t, D_CONV), F32),
                   jax.ShapeDtypeStruct((t, Q_LORA), BF16),
                   jax.ShapeDtypeStruct((t, KV_LORA), BF16),
                   jax.ShapeDtypeStruct((t, HEAD_PAD), F32),
                   jax.ShapeDtypeStruct((t, 2 * d), BF16)],
        compiler_params=_cparams("parallel"),
        name="inproj",
    )(h, g, w, qg, kvg)


_CONV_PAD = 32
_CONV_TILE = 64


def _conv_kernel(u_ref, pre_ref, cw_ref, cb_ref, lg_ref, lb_ref, wo_ref, gate_ref, o_ref,
                 buf_ref, act_ref, sh_ref, *, rows):
    s = u_ref.shape[1]
    npre = pre_ref.shape[0]
    buf_ref[0:_CONV_PAD - npre, :] = jnp.zeros((_CONV_PAD - npre, D_CONV), F32)
    buf_ref[_CONV_PAD - npre:_CONV_PAD, :] = pre_ref[...]
    buf_ref[_CONV_PAD:, :] = u_ref[0]
    shift = _CONV_PAD - (CONV_WIDTH - 1)

    def tile(i, carry):
        r0 = pl.multiple_of(i * rows, rows)
        acc = jnp.zeros((rows, D_CONV), F32) + cb_ref[...]
        win = buf_ref[pl.ds(r0, rows + _CONV_PAD), :]
        for res in range(1, SUBLANES):
            sh_ref[res - 1] = win[res:res + rows + _CONV_PAD - SUBLANES, :]
        for k in range(CONV_WIDTH):
            res, off = (shift + k) % SUBLANES, (shift + k) // SUBLANES * SUBLANES
            x = win[off:off + rows, :] if res == 0 else sh_ref[res - 1, off:off + rows, :]
            acc = acc + cw_ref[k:k + 1, :] * x
        mu = jnp.mean(acc, axis=-1, keepdims=True)
        xc = acc - mu
        var = jnp.mean(xc * xc, axis=-1, keepdims=True)
        y = xc * lax.rsqrt(var + EPS) * lg_ref[...] + lb_ref[...]
        act_ref[pl.ds(r0, rows), :] = (y * jax.nn.sigmoid(y)).astype(BF16)
        return carry

    lax.fori_loop(0, s // rows, tile, 0)
    y = jnp.dot(act_ref[...], wo_ref[...], preferred_element_type=F32)
    o_ref[0] = (gate_ref[0].astype(F32) * y).astype(BF16)


def _conv_branch(u, pre, cw, cb, lg, lb, wo, gates):
    b, s, _ = u.shape
    d = wo.shape[1]
    rows = min(_CONV_TILE, s)
    full = lambda a: pl.BlockSpec(a.shape, lambda i: (0,) * a.ndim)
    return pl.pallas_call(
        functools.partial(_conv_kernel, rows=rows),
        grid=(b,),
        in_specs=[pl.BlockSpec((1, s, D_CONV), lambda i: (i, 0, 0)), full(pre), full(cw), full(cb),
                  full(lg), full(lb), full(wo), pl.BlockSpec((1, s, d), lambda i: (i, 0, 0))],
        out_specs=pl.BlockSpec((1, s, d), lambda i: (i, 0, 0)),
        out_shape=jax.ShapeDtypeStruct((b, s, d), BF16),
        scratch_shapes=[pltpu.VMEM((_CONV_PAD + s, D_CONV), F32), pltpu.VMEM((s, D_CONV), BF16),
                        pltpu.VMEM((SUBLANES - 1, rows + _CONV_PAD - SUBLANES, D_CONV), F32)],
        compiler_params=_cparams("parallel"),
        name="conv_branch",
    )(u, pre, cw, cb, lg, lb, wo, gates)


def _rope_group(x, cos, sin_signed):
    return x * cos + pltpu.roll(x, LANES - ROPE_HALF, 1) * sin_signed


def _qkv_kernel(cq_ref, ckv_ref, kr_ref, wq_ref, wk_ref, wv_ref, qg_ref, kg_ref, cos_ref, sin_ref,
                q_ref, k_ref, v_ref):
    cos = cos_ref[...]
    sin = sin_ref[...]
    qf = jnp.dot(cq_ref[...], wq_ref[...], preferred_element_type=F32)
    kf = jnp.dot(ckv_ref[...], wk_ref[...], preferred_element_type=F32)
    kr = kr_ref[...]
    scale = QK_DIM ** -0.5 * math.log2(math.e)
    real = lax.broadcasted_iota(jnp.int32, kr.shape, 1) < QK_DIM
    for hd in range(N_HEADS):
        grp = slice(hd * HEAD_PAD, (hd + 1) * HEAD_PAD)
        qn = _rms(qf[:, grp], qg_ref[...], QK_DIM, real)
        q_ref[:, grp] = (_rope_group(qn, cos, sin) * scale).astype(BF16)
        kn = _rms(kf[:, grp] + kr, kg_ref[...], QK_DIM, real)
        k_ref[:, grp] = _rope_group(kn, cos, sin).astype(BF16)
    v_ref[...] = jnp.dot(ckv_ref[...], wv_ref[...], preferred_element_type=F32).astype(BF16)


def _qkv(cq, ckv, kr, wq, wk, wv, qg, kg, cos, sin, tb):
    t = cq.shape[0]
    nrope = cos.shape[0] // tb
    row = lambda c: pl.BlockSpec((tb, c), lambda i: (i, 0))
    full = lambda a: pl.BlockSpec(a.shape, lambda i: (0,) * a.ndim)
    rope = pl.BlockSpec((tb, HEAD_PAD), lambda i: (i % nrope, 0))
    hp = N_HEADS * HEAD_PAD
    return pl.pallas_call(
        _qkv_kernel,
        grid=(t // tb,),
        in_specs=[row(Q_LORA), row(KV_LORA), row(HEAD_PAD), full(wq), full(wk), full(wv),
                  full(qg), full(kg), rope, rope],
        out_specs=[row(hp), row(hp), row(N_HEADS * V_DIM)],
        out_shape=[jax.ShapeDtypeStruct((t, hp), BF16), jax.ShapeDtypeStruct((t, hp), BF16),
                   jax.ShapeDtypeStruct((t, N_HEADS * V_DIM), BF16)],
        compiler_params=_cparams("parallel"),
        name="qkv",
    )(cq, ckv, kr, wq, wk, wv, qg, kg, cos, sin)


_ATTN_PAIRS = 2


def _attn_kernel(*refs, tq, has_real):
    if has_real:
        q_ref, k_ref, v_ref, km_ref, vm_ref, vis_ref, o_ref = refs
    else:
        q_ref, km_ref, vm_ref, o_ref = refs
    qi = pl.program_id(2)
    nt = (((1,), (1,)), ((), ()))
    heads = range(2 * _ATTN_PAIRS)
    groups = [slice(hh * HEAD_PAD, (hh + 1) * HEAD_PAD) for hh in heads]
    vcols = [slice((hh // 2) * 2 * V_DIM, (hh // 2 + 1) * 2 * V_DIM) for hh in heads]
    qs = [q_ref[0, :, grp] for grp in groups]

    def step(s, v, carry):
        m, l, acc = carry
        m_new = jnp.maximum(m, jnp.max(s, axis=-1, keepdims=True))
        alpha = jnp.exp2(m - m_new)
        p = jnp.exp2(s - m_new)
        l = alpha * l + jnp.sum(p, axis=-1, keepdims=True)
        acc = alpha * acc + jnp.dot(p.astype(BF16), v, preferred_element_type=F32)
        return m_new, l, acc

    state = []
    for q, grp, vc in zip(qs, groups, vcols):
        s = lax.dot_general(q, km_ref[:, grp], nt, preferred_element_type=F32)
        col = lax.broadcasted_iota(jnp.int32, s.shape, 1)
        s = jnp.where(col < N_META, s, MASK_VALUE)
        m = jnp.max(s, axis=-1, keepdims=True)
        p = jnp.exp2(s - m)
        l = jnp.sum(p, axis=-1, keepdims=True)
        state.append((m, l, jnp.dot(p.astype(BF16), vm_ref[:, vc], preferred_element_type=F32)))
    state = tuple(state)

    if has_real:
        def block(r0, carry, masked):
            out = []
            for q, grp, vc, c in zip(qs, groups, vcols, carry):
                s = lax.dot_general(q, k_ref[0, pl.ds(r0, tq), grp], nt, preferred_element_type=F32)
                if masked:
                    s = jnp.where(vis_ref[...] > 0.0, s, MASK_VALUE)
                out.append(step(s, v_ref[0, pl.ds(r0, tq), vc], c))
            return tuple(out)

        state = lax.fori_loop(0, qi, lambda kb, c: block(pl.multiple_of(kb * tq, tq), c, False), state)
        state = block(pl.multiple_of(qi * tq, tq), state, True)
    outs = [acc / l for (_, l, acc) in state]
    lane = lax.broadcasted_iota(jnp.int32, outs[0].shape, 1)
    pairs = [jnp.where(lane < V_DIM, outs[2 * p], outs[2 * p + 1]) for p in range(_ATTN_PAIRS)]
    o_ref[0] = jnp.concatenate(pairs, axis=-1).astype(BF16)


def _attention(q, k, v, km, vm, tq):
    b, sq, _ = q.shape
    has_real = k is not None
    qw = 2 * _ATTN_PAIRS * HEAD_PAD
    vw = 2 * _ATTN_PAIRS * V_DIM
    qspec = pl.BlockSpec((1, tq, qw), lambda bi, hp, qi: (bi, qi, hp))
    mk = pl.BlockSpec((META_PAD, qw), lambda bi, hp, qi: (0, hp))
    mv = pl.BlockSpec((META_PAD, vw), lambda bi, hp, qi: (0, hp))
    if has_real:
        s = k.shape[1]
        chunk = jnp.arange(tq, dtype=jnp.int32) // CHUNK
        vis = (chunk[None, :] <= chunk[:, None]).astype(F32)
        in_specs = [qspec, pl.BlockSpec((1, s, qw), lambda bi, hp, qi: (bi, 0, hp)),
                    pl.BlockSpec((1, s, vw), lambda bi, hp, qi: (bi, 0, hp)), mk, mv,
                    pl.BlockSpec((tq, tq), lambda bi, hp, qi: (0, 0))]
        args = (q, k, v, km, vm, vis)
    else:
        in_specs = [qspec, mk, mv]
        args = (q, km, vm)
    return pl.pallas_call(
        functools.partial(_attn_kernel, tq=tq, has_real=has_real),
        grid=(b, N_HEADS // (2 * _ATTN_PAIRS), sq // tq),
        in_specs=in_specs,
        out_specs=pl.BlockSpec((1, tq, vw), lambda bi, hp, qi: (bi, qi, hp)),
        out_shape=jax.ShapeDtypeStruct((b, sq, N_HEADS * V_DIM), BF16),
        compiler_params=_cparams("parallel", "parallel", "arbitrary"),
        name="attention",
    )(*args)


def _merge_kernel(h_ref, o_ref, gc_ref, g2_ref, wm_ref, wo_ref, out_ref):
    ymla = jnp.dot(o_ref[...], wm_ref[...], preferred_element_type=F32)
    merged = gc_ref[...].astype(F32) + g2_ref[...].astype(F32) * ymla
    out_ref[...] = h_ref[...] + jnp.dot(merged.astype(BF16), wo_ref[...], preferred_element_type=F32)


def _merge(h, o, gc, gates, wm, wo, tb):
    t, d = h.shape
    row = lambda c: pl.BlockSpec((tb, c), lambda i: (i, 0))
    full = lambda a: pl.BlockSpec(a.shape, lambda i: (0,) * a.ndim)
    return pl.pallas_call(
        _merge_kernel,
        grid=(t // tb,),
        in_specs=[row(d), row(N_HEADS * V_DIM), row(d), pl.BlockSpec((tb, d), lambda i: (i, 1)),
                  full(wm), full(wo)],
        out_specs=row(d),
        out_shape=jax.ShapeDtypeStruct((t, d), F32),
        compiler_params=_cparams("parallel"),
        name="merge_out",
    )(h, o, gc, gates, wm, wo)


def _peer_score_kernel(h_ref, g_ref, wq_ref, keys_ref, xt_ref, s_ref):
    d = h_ref.shape[-1]
    xn = _rms(h_ref[...], g_ref[...], d)
    xt = xn.T.astype(BF16)
    xt_ref[...] = xt
    qt = jnp.dot(wq_ref[...], xt, preferred_element_type=F32)
    for g in range(2 * PEER_HEADS):
        qg = qt[g * PEER_HALF:(g + 1) * PEER_HALF, :].astype(BF16)
        s_ref[g] = jnp.dot(keys_ref[g], qg, preferred_element_type=F32)


def _peer_scores(h, g, wqt, keys, tb):
    t, d = h.shape
    full = lambda a: pl.BlockSpec(a.shape, lambda i: (0,) * a.ndim)
    ng = 2 * PEER_HEADS
    return pl.pallas_call(
        _peer_score_kernel,
        grid=(t // tb,),
        in_specs=[pl.BlockSpec((tb, d), lambda i: (i, 0)), full(g), full(wqt), full(keys)],
        out_specs=[pl.BlockSpec((d, tb), lambda i: (0, i)),
                   pl.BlockSpec((ng, N_KEYS, tb), lambda i: (0, 0, i))],
        out_shape=[jax.ShapeDtypeStruct((d, t), BF16), jax.ShapeDtypeStruct((ng, N_KEYS, t), F32)],
        compiler_params=_cparams("parallel"),
        name="peer_scores",
    )(h, g, wqt, keys)


def _extract_topk(vals, pos, sv_ref):
    big = float(vals.shape[0] * vals.shape[0])

    def body(a, carry):
        cur, rank = carry
        m = jnp.max(cur, axis=0, keepdims=True)
        first = jnp.min(jnp.where(cur == m, pos, big), axis=0, keepdims=True)
        hit = pos == first
        if sv_ref is not None:
            sv_ref[pl.ds(a, 1), :] = m
        return jnp.where(hit, NEG_INF, cur), jnp.where(hit, lax.convert_element_type(a, F32), rank)

    init = (vals, jnp.full(vals.shape, float(PEER_TOPK), F32))
    return lax.fori_loop(0, PEER_TOPK, body, init)[1]


def _topk_exact_tile(s1, s2, sv1_ref, sv2_ref):
    lanes = s1.shape[-1]
    k = PEER_TOPK
    key_pos = lax.broadcasted_iota(jnp.int32, (N_KEYS, lanes), 0).astype(F32)
    crow = lax.broadcasted_iota(jnp.int32, (k * k, lanes), 0)
    cand_pos = ((crow % k) * k + crow // k).astype(F32)
    r1 = _extract_topk(s1, key_pos, sv1_ref)
    r2 = _extract_topk(s2, key_pos, sv2_ref)
    sv1 = sv1_ref[...]
    sv2 = sv2_ref[...]
    cand = jnp.concatenate([sv1 + sv2[b:b + 1, :] for b in range(k)], axis=0)
    sel = _extract_topk(cand, cand_pos, None) < float(k)
    t1 = jnp.exp(sv1 - sv1[0:1, :])
    t2 = jnp.exp(sv2 - sv2[0:1, :])
    z = jnp.zeros((1, lanes), F32)
    n_a = jnp.zeros((k, lanes), F32)
    for b in range(k):
        sb = sel[b * k:(b + 1) * k, :]
        z = z + jnp.sum(jnp.where(sb, t1 * t2[b:b + 1, :], 0.0), axis=0, keepdims=True)
        n_a = n_a + jnp.where(sb, 1.0, 0.0)
    cnt = jnp.zeros((N_KEYS, lanes), F32)
    for a in range(k):
        cnt = jnp.where(r1 == float(a), n_a[a:a + 1, :], cnt)
    return r2, cnt, jnp.exp(s1 - sv1[0:1, :]), jnp.exp(s2 - sv2[0:1, :]) / z


def _peer_topk_small_kernel(s_ref, r2_ref, cnt_ref, e1_ref, e2_ref, sv1_ref, sv2_ref):
    r2, cnt, e1, e2 = _topk_exact_tile(s_ref[0], s_ref[1], sv1_ref, sv2_ref)
    r2_ref[0] = r2.astype(BF16)
    cnt_ref[0] = cnt
    e1_ref[0] = e1
    e2_ref[0] = e2.astype(BF16)


def _batcher_pairs(n):
    pairs = []
    p = 1
    while p < n:
        k = p
        while k >= 1:
            for j in range(k % p, n - k, 2 * k):
                for i in range(min(k, n - j - k)):
                    if (i + j) // (2 * p) == (i + j + k) // (2 * p):
                        pairs.append((i + j, i + j + k))
            k //= 2
        p *= 2
    return pairs


_SORT16 = _batcher_pairs(PEER_TOPK)
_STAIR = [(a, b) for a in range(PEER_TOPK) for b in range(PEER_TOPK) if (a + 1) * (b + 1) <= PEER_TOPK]
_TILES = LANES * SUBLANES


def _top16_desc(load, lo, n):
    k = PEER_TOPK
    if n == k:
        v = [load(lo + i) for i in range(k)]
        for i, j in _SORT16:
            v[i], v[j] = jnp.maximum(v[i], v[j]), jnp.minimum(v[i], v[j])
        return v
    x = _top16_desc(load, lo, n // 2)
    y = _top16_desc(load, lo + n // 2, n // 2)
    c = [jnp.maximum(x[i], y[k - 1 - i]) for i in range(k)]
    d = k // 2
    while d >= 1:
        for i in range(k):
            if i & d == 0:
                c[i], c[i + d] = jnp.maximum(c[i], c[i + d]), jnp.minimum(c[i], c[i + d])
        d //= 2
    return c


def _sublane_transpose(vs):
    sub = lax.broadcasted_iota(jnp.int32, vs[0].shape, 0)
    d = SUBLANES // 2
    while d >= 1:
        low = (sub & d) == 0
        nxt = list(vs)
        for j in range(SUBLANES):
            if j & d == 0:
                x, y = vs[j], vs[j + d]
                nxt[j] = jnp.where(low, x, pltpu.roll(y, d, 0))
                nxt[j + d] = jnp.where(low, pltpu.roll(x, SUBLANES - d, 0), y)
        vs = nxt
        d //= 2
    return vs


def _peer_topk_kernel(s_ref, r2_ref, cnt_ref, e1_ref, e2_ref, slab_ref, sv_ref, u_ref, misc_ref,
                      sv1_ref, sv2_ref):
    k = PEER_TOPK
    inf = float("inf")
    for half in range(2):
        for kt in range(N_KEYS // SUBLANES):
            rows = slice(kt * SUBLANES, (kt + 1) * SUBLANES)
            tiles = [s_ref[half, rows, j * LANES:(j + 1) * LANES] for j in range(SUBLANES)]
            for r, slab in enumerate(_sublane_transpose(tiles)):
                slab_ref[half * N_KEYS + kt * SUBLANES + r] = slab
    sv1 = _top16_desc(lambda i: slab_ref[i], 0, N_KEYS)
    sv2 = _top16_desc(lambda i: slab_ref[i], N_KEYS, N_KEYS)
    for a in range(k):
        sv_ref[a] = sv1[a]
        sv_ref[k + a] = sv2[a]
    tie = jnp.zeros(sv1[0].shape, F32)
    for v in (sv1, sv2):
        for a in range(k - 1):
            tie = jnp.where(v[a] == v[a + 1], 1.0, tie)
    cand = {c: sv1[c[0]] + sv2[c[1]] for c in _STAIR}
    beaten = {c: float((c[0] + 1) * (c[1] + 1) - 1) for c in _STAIR}
    dyn = {c: None for c in _STAIR}
    for x_i, x in enumerate(_STAIR):
        for y in _STAIR[x_i + 1:]:
            if (x[0] < y[0]) == (x[1] < y[1]) or x[0] == y[0] or x[1] == y[1]:
                continue
            g = jnp.where(cand[x] >= cand[y], 1.0, 0.0)
            dyn[y] = g if dyn[y] is None else dyn[y] + g
            beaten[x] += 1.0
            dyn[x] = -g if dyn[x] is None else dyn[x] - g
    sel = {c: (dyn[c] + beaten[c] if dyn[c] is not None else jnp.full(tie.shape, beaten[c])) < float(k)
           for c in _STAIR}
    t1 = [jnp.exp(sv1[a] - sv1[0]) for a in range(k)]
    t2 = [jnp.exp(sv2[b] - sv2[0]) for b in range(k)]
    z = jnp.zeros(tie.shape, F32)
    for c in _STAIR:
        z = z + jnp.where(sel[c], t1[c[0]] * t2[c[1]], 0.0)
    for b in range(k):
        u = jnp.full(tie.shape, inf, F32)
        for a in range(k):
            if (a, b) in sel:
                u = jnp.minimum(u, jnp.where(sel[(a, b)], sv1[a], inf))
        u_ref[b] = u
    misc_ref[0] = sv1[0]
    misc_ref[1] = sv2[0]
    misc_ref[2] = 1.0 / z
    misc_ref[3] = tie

    rows2 = 2 * SUBLANES
    bad = []
    for tt in range(SUBLANES):
        cols = slice(tt * LANES, (tt + 1) * LANES)
        bc = lambda ref, i: jnp.broadcast_to(ref[i, tt:tt + 1, :], (rows2, LANES))
        svb = [bc(sv_ref, k + b) for b in range(k)]
        ub = [bc(u_ref, b) for b in range(k)]
        c1, c2, zinv, last1 = bc(misc_ref, 0), bc(misc_ref, 1), bc(misc_ref, 2), bc(sv_ref, k - 1)
        n1 = jnp.zeros((rows2, LANES), F32)
        n2 = jnp.zeros((rows2, LANES), F32)
        for kt in range(N_KEYS // rows2):
            rows = slice(kt * rows2, (kt + 1) * rows2)
            x1 = s_ref[0, rows, cols]
            x2 = s_ref[1, rows, cols]
            r = jnp.full((rows2, LANES), float(k), F32)
            c = jnp.full((rows2, LANES), float(k), F32)
            for b in reversed(range(k)):
                r = jnp.where(svb[b] <= x2, float(b), r)
                c = jnp.where(ub[b] > x1, float(b), c)
            r2_ref[0, rows, cols] = r.astype(BF16)
            cnt_ref[0, rows, cols] = c
            e1_ref[0, rows, cols] = jnp.exp(x1 - c1)
            e2_ref[0, rows, cols] = (jnp.exp(x2 - c2) * zinv).astype(BF16)
            n1 = n1 + jnp.where(x1 >= last1, 1.0, 0.0)
            n2 = n2 + jnp.where(r < float(k), 1.0, 0.0)
        bad.append(jnp.abs(jnp.sum(n1, axis=0, keepdims=True) - k)
                   + jnp.abs(jnp.sum(n2, axis=0, keepdims=True) - k))

    any_tied = jnp.max(sum(bad)) + jnp.max(misc_ref[3]) > 0.0

    @pl.when(any_tied)
    def _():
        for tt in range(SUBLANES):
            tied = jnp.max(bad[tt]) + jnp.max(misc_ref[3, tt:tt + 1, :]) > 0.0

            @pl.when(tied)
            def _():
                cols = slice(tt * LANES, (tt + 1) * LANES)
                r2, cnt, e1, e2 = _topk_exact_tile(s_ref[0, :, cols], s_ref[1, :, cols], sv1_ref, sv2_ref)
                r2_ref[0, :, cols] = r2.astype(BF16)
                cnt_ref[0, :, cols] = cnt
                e1_ref[0, :, cols] = e1
                e2_ref[0, :, cols] = e2.astype(BF16)


def _peer_topk(s):
    ng, nk, t = s.shape
    fast = t % _TILES == 0
    lanes = _TILES if fast else LANES
    spec = pl.BlockSpec((1, nk, lanes), lambda i, hd: (hd, 0, i))
    shp = lambda dt: jax.ShapeDtypeStruct((PEER_HEADS, nk, t), dt)
    slab = lambda n: pltpu.VMEM((n, SUBLANES, LANES), F32)
    row = pltpu.VMEM((PEER_TOPK, LANES), F32)
    scratch = [slab(2 * N_KEYS), slab(2 * PEER_TOPK), slab(PEER_TOPK), slab(4), row, row] if fast else [row, row]
    return pl.pallas_call(
        _peer_topk_kernel if fast else _peer_topk_small_kernel,
        grid=(t // lanes, PEER_HEADS),
        in_specs=[pl.BlockSpec((2, nk, lanes), lambda i, hd: (hd, 0, i))],
        out_specs=[spec, spec, spec, spec],
        out_shape=[shp(BF16), shp(F32), shp(F32), shp(BF16)],
        scratch_shapes=scratch,
        compiler_params=_cparams("parallel", "parallel"),
        name="peer_topk",
    )(s)


def _gelu_tanh(x):
    c = -2.0 * math.sqrt(2.0 / math.pi) * math.log2(math.e)
    return x / (1.0 + jnp.exp2(x * (c + (c * 0.044715) * (x * x))))


_DENSE_CHUNK_ROWS = 8
_DENSE_KW = 512


def _peer_dense_kernel(*refs, rows_per_step, n_u, n_v):
    h_ref, xt_ref = refs[:2]
    u_refs = refs[2:2 + n_u]
    vt_refs = refs[2 + n_u:2 + n_u + n_v]
    r2_ref, cnt_ref, e1_ref, e2_ref, o_ref, acc_ref = refs[2 + n_u + n_v:]
    e = pl.program_id(1)
    lanes = xt_ref.shape[-1]
    bf16_rows = 2 * SUBLANES
    kw = _DENSE_KW

    @pl.when(e == 0)
    def _():
        acc_ref[...] = jnp.zeros_like(acc_ref)

    ce = _DENSE_CHUNK_ROWS * N_KEYS
    n_chunks = rows_per_step // _DENSE_CHUNK_ROWS
    zero = jnp.zeros((), BF16)

    def scores(c):
        out = None
        for k, u_ref in enumerate(u_refs):
            part = jnp.dot(u_ref[c * ce:(c + 1) * ce, :], xt_ref[k * kw:(k + 1) * kw, :],
                           preferred_element_type=F32)
            out = part if out is None else part + out
        return out

    def packed_row(ref, hd, i):
        row = jnp.broadcast_to(ref[hd, pl.ds(i, 1), :], (bf16_rows, lanes))
        return jnp.tile(row.astype(BF16), (N_KEYS // bf16_rows, 1))

    a_next = scores(0)
    total = None
    for c in range(n_chunks):
        a = a_next
        ws = []
        for ii in range(_DENSE_CHUNK_ROWS):
            i = e * rows_per_step + c * _DENSE_CHUNK_ROWS + ii
            w = None
            for hd in range(PEER_HEADS):
                term = packed_row(e1_ref, hd, i) * jnp.where(r2_ref[hd] < packed_row(cnt_ref, hd, i),
                                                             e2_ref[hd], zero)
                w = term if w is None else w + term
            ws.append(w)
        if c + 1 < n_chunks:
            a_next = scores(c + 1)
        act = _gelu_tanh(a.astype(BF16)) * jnp.concatenate(ws, axis=0)
        for k in range(ce // kw):
            part = jnp.dot(vt_refs[c * (ce // kw) + k][...], act[k * kw:(k + 1) * kw, :],
                           preferred_element_type=F32)
            total = part if total is None else part + total
    acc_ref[...] += total

    @pl.when(e == pl.num_programs(1) - 1)
    def _():
        o_ref[...] = h_ref[...] + acc_ref[...].T


def _peer_dense(h, xt, u, vt, layer, r2, cnt, e1, e2, lanes, rows_per_step):
    t, d = h.shape
    ne = u.shape[1]
    eb = rows_per_step * N_KEYS
    kw = _DENSE_KW
    n_u, n_v = d // kw, eb // kw
    tok = pl.BlockSpec((PEER_HEADS, N_KEYS, lanes), lambda i, e: (0, 0, i))
    u_specs = [pl.BlockSpec((None, eb, kw), functools.partial(lambda i, e, k: (layer, e, k), k=k))
               for k in range(n_u)]
    v_specs = [pl.BlockSpec((None, d, kw), functools.partial(lambda i, e, k: (layer, 0, e * n_v + k), k=k))
               for k in range(n_v)]
    return pl.pallas_call(
        functools.partial(_peer_dense_kernel, rows_per_step=rows_per_step, n_u=n_u, n_v=n_v),
        grid=(t // lanes, ne // eb),
        in_specs=[pl.BlockSpec((lanes, d), lambda i, e: (i, 0)),
                  pl.BlockSpec((d, lanes), lambda i, e: (0, i))] + u_specs + v_specs + [tok, tok, tok, tok],
        out_specs=pl.BlockSpec((lanes, d), lambda i, e: (i, 0)),
        out_shape=jax.ShapeDtypeStruct((t, d), F32),
        scratch_shapes=[pltpu.VMEM((d, lanes), F32)],
        compiler_params=_cparams("parallel", "arbitrary"),
        name="peer_dense",
    )(h, xt, *([u] * n_u), *([vt] * n_v), r2, cnt, e1, e2)


def _pad_groups(w, real, n_groups):
    k = w.shape[0]
    w = w.reshape(k, n_groups, real)
    return jnp.pad(w, ((0, 0), (0, 0), (0, HEAD_PAD - real))).reshape(k, n_groups * HEAD_PAD)


def _with_rope_copy(w):
    first = w[..., ROPE_LO:ROPE_LO + ROPE_HALF]
    pad = jnp.zeros(w.shape[:-1] + (HEAD_PAD - QK_DIM - ROPE_HALF,), w.dtype)
    return jnp.concatenate([w, first, pad], axis=-1)


def _layer_params(l, w_in, w_conv_out, w_uq, w_ukv, q_norm_g, k_norm_g, w_mla_out, w_out, peer_wq,
                  peer_keys):
    d = w_in.shape[1]
    wi = w_in[l]
    o0 = 2 * D_CONV
    o1 = o0 + Q_LORA
    o2 = o1 + KV_LORA
    o3 = o2 + ROPE_DIM
    rope_cols = _with_rope_copy(jnp.pad(wi[:, o2:o3], ((0, 0), (ROPE_LO, 0))))
    w_in_p = jnp.concatenate([wi[:, :o2], rope_cols, wi[:, o3:]], axis=1).astype(BF16)
    wkv = w_ukv[l].reshape(KV_LORA, N_HEADS, NOPE_DIM + V_DIM)
    wk = _pad_groups(wkv[:, :, :NOPE_DIM].reshape(KV_LORA, N_HEADS * NOPE_DIM), NOPE_DIM, N_HEADS)
    wv = wkv[:, :, NOPE_DIM:].reshape(KV_LORA, N_HEADS * V_DIM)
    padg = lambda g: _with_rope_copy(g).reshape(1, HEAD_PAD)
    return dict(
        w_in=w_in_p,
        w_conv_out=w_conv_out[l].astype(BF16),
        wq=_with_rope_copy(w_uq[l].reshape(Q_LORA, N_HEADS, QK_DIM)).reshape(Q_LORA, N_HEADS * HEAD_PAD).astype(BF16),
        wk=wk.astype(BF16),
        wv=wv.astype(BF16),
        qg=padg(q_norm_g[l]),
        kg=padg(k_norm_g[l]),
        w_mla_out=w_mla_out[l].astype(BF16),
        w_out=w_out[l].astype(BF16),
        peer_wqt=peer_wq[l],
        peer_keys=peer_keys[l].reshape(2 * PEER_HEADS, N_KEYS, PEER_HALF).astype(BF16),
    )


def _rope_tables(length):
    pos = jnp.arange(length, dtype=F32)
    inv = 1.0 / (ROPE_THETA ** (jnp.arange(0, ROPE_DIM, 2, dtype=F32) / ROPE_DIM))
    ang = pos[:, None] * inv[None, :]
    cos = jnp.concatenate([jnp.cos(ang), jnp.cos(ang)], axis=-1)
    sin = jnp.concatenate([-jnp.sin(ang), jnp.sin(ang)], axis=-1)
    cos = jnp.pad(cos, ((0, 0), (ROPE_LO, 0)), constant_values=1.0)
    cos = jnp.pad(cos, ((0, 0), (0, HEAD_PAD - ROPE_LO - ROPE_DIM)))
    sin = jnp.pad(sin, ((0, 0), (ROPE_LO, HEAD_PAD - ROPE_LO - ROPE_DIM)))
    return cos, sin


def _row(v):
    return v.reshape(1, -1)


def kernel(x, meta_tokens, mix_norm_g, w_in, conv_w, conv_b, conv_ln_g, conv_ln_b, w_conv_out,
           q_a_norm_g, w_uq, kv_a_norm_g, w_ukv, q_norm_g, k_norm_g, w_mla_out, w_out, ffn_norm_g,
           peer_wq, peer_keys, peer_u, peer_v):
    bsz, seq, d = x.shape
    depth = w_in.shape[0]
    t = bsz * seq
    assert meta_tokens.shape[0] == N_META and seq % 256 == 0 and d % LANES == 0
    tb = min(512, seq)
    tq = min(512, seq)
    dense_lanes = min(512, t)
    rows_per_step = 16

    cos, sin = _rope_tables(N_META + seq)
    h = x.reshape(t, d)
    hm = meta_tokens.astype(x.dtype)
    zero_pre = jnp.zeros((N_META, D_CONV), F32)

    peer_wqt = jnp.swapaxes(peer_wq, 1, 2).astype(BF16)
    peer_ub = peer_u.astype(BF16)
    peer_vt = jnp.swapaxes(peer_v, 1, 2).astype(BF16)

    for l in range(depth):
        p = _layer_params(l, w_in, w_conv_out, w_uq, w_ukv, q_norm_g, k_norm_g, w_mla_out, w_out,
                          peer_wqt, peer_keys)
        last = l == depth - 1
        g_mix, qag, kvag = _row(mix_norm_g[l]), _row(q_a_norm_g[l]), _row(kv_a_norm_g[l])
        conv_args = (conv_w[l], _row(conv_b[l]), _row(conv_ln_g[l]), _row(conv_ln_b[l]), p["w_conv_out"])

        um, cqm, ckvm, krm, gatem = _inproj(hm, g_mix, p["w_in"], qag, kvag, N_META)
        qm, km, vm = _qkv(cqm, ckvm, krm, p["wq"], p["wk"], p["wv"], p["qg"], p["kg"],
                          cos[:N_META], sin[:N_META], N_META)
        km_p = jnp.pad(km, ((0, META_PAD - N_META), (0, 0)))
        vm_p = jnp.pad(vm, ((0, META_PAD - N_META), (0, 0)))

        u, cq, ckv, kr, gates = _inproj(h, g_mix, p["w_in"], qag, kvag, tb)
        gc = _conv_branch(u.reshape(bsz, seq, D_CONV), um, *conv_args, gates.reshape(bsz, seq, 2 * d))
        q, k, v = _qkv(cq, ckv, kr, p["wq"], p["wk"], p["wv"], p["qg"], p["kg"],
                       cos[N_META:], sin[N_META:], tb)
        hp = N_HEADS * HEAD_PAD
        o = _attention(q.reshape(bsz, seq, hp), k.reshape(bsz, seq, hp),
                       v.reshape(bsz, seq, N_HEADS * V_DIM), km_p, vm_p, tq)
        h1 = _merge(h, o.reshape(t, N_HEADS * V_DIM), gc.reshape(t, d), gates, p["w_mla_out"],
                    p["w_out"], tb)
        xt, s = _peer_scores(h1, _row(ffn_norm_g[l]), p["peer_wqt"], p["peer_keys"], tb)
        r2, cnt, e1, e2 = _peer_topk(s)
        h = _peer_dense(h1, xt, peer_ub, peer_vt, l, r2, cnt, e1, e2, dense_lanes, rows_per_step)

        if not last:
            gcm = _conv_branch(um.reshape(1, N_META, D_CONV), zero_pre, *conv_args,
                               gatem.reshape(1, N_META, 2 * d))
            om = _attention(qm.reshape(1, N_META, hp), None, None, km_p, vm_p, N_META)
            hm1 = _merge(hm, om.reshape(N_META, N_HEADS * V_DIM), gcm.reshape(N_META, d), gatem,
                         p["w_mla_out"], p["w_out"], N_META)
            hm1_p = jnp.pad(hm1, ((0, LANES - N_META), (0, 0)))
            xtm, sm = _peer_scores(hm1_p, _row(ffn_norm_g[l]), p["peer_wqt"], p["peer_keys"], LANES)
            r2m, cntm, e1m, e2m = _peer_topk(sm)
            hm = _peer_dense(hm1_p, xtm, peer_ub, peer_vt, l, r2m, cntm, e1m, e2m, LANES,
                             rows_per_step)[:N_META]

    return h.reshape(bsz, seq, d)
```

```python
import functools
import math

import jax
import jax.numpy as jnp
from jax import lax
from jax.experimental import pallas as pl
from jax.experimental.pallas import tpu as pltpu

F32 = jnp.float32
BF16 = jnp.bfloat16

CHUNK = 64
N_META = 16
D_CONV = 512
CONV_WIDTH = 31
N_HEADS = 8
Q_LORA = 256
KV_LORA = 128
NOPE_DIM = 64
ROPE_DIM = 32
QK_DIM = NOPE_DIM + ROPE_DIM
V_DIM = 64
ROPE_THETA = 10000.0
PEER_HEADS = 8
PEER_HALF = 128
N_KEYS = 128
PEER_TOPK = 16
EPS = 1e-6
MASK_VALUE = -1e30

LANES = 128
SUBLANES = 8
VMEM_LIMIT = 48 * 1024 * 1024

HEAD_PAD = LANES
ROPE_LO = NOPE_DIM
ROPE_HALF = ROPE_DIM // 2
META_PAD = LANES
NEG_INF = float("-inf")


def _cparams(*sem):
    return pltpu.CompilerParams(dimension_semantics=sem, vmem_limit_bytes=VMEM_LIMIT)


def _rms(x, g, n, valid=None):
    xs = x if valid is None else jnp.where(valid, x, 0.0)
    ms = jnp.sum(xs * xs, axis=-1, keepdims=True) * (1.0 / n)
    return x * lax.rsqrt(ms + EPS) * g


_C_CONV = 0
_C_Q = 2 * D_CONV
_C_KV = _C_Q + Q_LORA
_C_ROPE = _C_KV + KV_LORA
_C_GATE = _C_ROPE + HEAD_PAD


def _inproj_kernel(h_ref, g_ref, w_ref, qg_ref, kvg_ref, u_ref, cq_ref, ckv_ref, kr_ref, gate_ref):
    d = h_ref.shape[-1]
    xn = _rms(h_ref[...], g_ref[...], d).astype(BF16)

    def proj(lo, hi):
        return jnp.dot(xn, w_ref[:, lo:hi], preferred_element_type=F32)

    a = proj(_C_CONV, _C_CONV + D_CONV)
    b = proj(_C_CONV + D_CONV, _C_Q)
    u_ref[...] = a * jax.nn.sigmoid(b)
    cq_ref[...] = _rms(proj(_C_Q, _C_KV), qg_ref[...], Q_LORA).astype(BF16)
    ckv_ref[...] = _rms(proj(_C_KV, _C_ROPE), kvg_ref[...], KV_LORA).astype(BF16)
    kr_ref[...] = proj(_C_ROPE, _C_GATE)
    gate_ref[...] = jax.nn.sigmoid(proj(_C_GATE, _C_GATE + 2 * d)).astype(BF16)


def _inproj(h, g, w, qg, kvg, tb):
    t, d = h.shape
    n = w.shape[1]
    row = lambda c: pl.BlockSpec((tb, c), lambda i: (i, 0))
    full = lambda a: pl.BlockSpec(a.shape, lambda i: (0,) * a.ndim)
    return pl.pallas_call(
        _inproj_kernel,
        grid=(t // tb,),
        in_specs=[row(d), full(g), full(w), full(qg), full(kvg)],
        out_specs=[row(D_CONV), row(Q_LORA), row(KV_LORA), row(HEAD_PAD), row(2 * d)],
        out_shape=[jax.ShapeDtypeStruct((t, D_CONV), F32),
                   jax.ShapeDtypeStruct((t, Q_LORA), BF16),
                   jax.ShapeDtypeStruct((t, KV_LORA), BF16),
                   jax.ShapeDtypeStruct((t, HEAD_PAD), F32),
                   jax.ShapeDtypeStruct((t, 2 * d), BF16)],
        compiler_params=_cparams("parallel"),
        name="inproj",
    )(h, g, w, qg, kvg)


_CONV_PAD = 32
_CONV_TILE = 256


def _conv_kernel(u_ref, pre_ref, cw_ref, cb_ref, lg_ref, lb_ref, wo_ref, gate_ref, o_ref,
                 buf_ref, act_ref, sh_ref, *, rows):
    s = u_ref.shape[1]
    npre = pre_ref.shape[0]
    buf_ref[0:_CONV_PAD - npre, :] = jnp.zeros((_CONV_PAD - npre, D_CONV), F32)
    buf_ref[_CONV_PAD - npre:_CONV_PAD, :] = pre_ref[...]
    buf_ref[_CONV_PAD:, :] = u_ref[0]
    shift = _CONV_PAD - (CONV_WIDTH - 1)

    def tile(i, carry):
        r0 = pl.multiple_of(i * rows, rows)
        acc = jnp.zeros((rows, D_CONV), F32) + cb_ref[...]
        win = buf_ref[pl.ds(r0, rows + _CONV_PAD), :]
        for res in range(1, SUBLANES):
            sh_ref[res - 1] = win[res:res + rows + _CONV_PAD - SUBLANES, :]
        for k in range(CONV_WIDTH):
            res, off = (shift + k) % SUBLANES, (shift + k) // SUBLANES * SUBLANES
            x = win[off:off + rows, :] if res == 0 else sh_ref[res - 1, off:off + rows, :]
            acc = acc + cw_ref[k:k + 1, :] * x
        mu = jnp.mean(acc, axis=-1, keepdims=True)
        xc = acc - mu
        var = jnp.mean(xc * xc, axis=-1, keepdims=True)
        y = xc * lax.rsqrt(var + EPS) * lg_ref[...] + lb_ref[...]
        act_ref[pl.ds(r0, rows), :] = (y * jax.nn.sigmoid(y)).astype(BF16)
        return carry

    lax.fori_loop(0, s // rows, tile, 0)
    y = jnp.dot(act_ref[...], wo_ref[...], preferred_element_type=F32)
    o_ref[0] = (gate_ref[0].astype(F32) * y).astype(BF16)


def _conv_branch(u, pre, cw, cb, lg, lb, wo, gates):
    b, s, _ = u.shape
    d = wo.shape[1]
    rows = min(_CONV_TILE, s)
    full = lambda a: pl.BlockSpec(a.shape, lambda i: (0,) * a.ndim)
    return pl.pallas_call(
        functools.partial(_conv_kernel, rows=rows),
        grid=(b,),
        in_specs=[pl.BlockSpec((1, s, D_CONV), lambda i: (i, 0, 0)), full(pre), full(cw), full(cb),
                  full(lg), full(lb), full(wo), pl.BlockSpec((1, s, d), lambda i: (i, 0, 0))],
        out_specs=pl.BlockSpec((1, s, d), lambda i: (i, 0, 0)),
        out_shape=jax.ShapeDtypeStruct((b, s, d), BF16),
        scratch_shapes=[pltpu.VMEM((_CONV_PAD + s, D_CONV), F32), pltpu.VMEM((s, D_CONV), BF16),
                        pltpu.VMEM((SUBLANES - 1, rows + _CONV_PAD - SUBLANES, D_CONV), F32)],
        compiler_params=_cparams("parallel"),
        name="conv_branch",
    )(u, pre, cw, cb, lg, lb, wo, gates)


def _rope_group(x, cos, sin_signed):
    return x * cos + pltpu.roll(x, LANES - ROPE_HALF, 1) * sin_signed


def _qkv_kernel(cq_ref, ckv_ref, kr_ref, wq_ref, wk_ref, wv_ref, qg_ref, kg_ref, cos_ref, sin_ref,
                q_ref, k_ref, v_ref):
    cos = cos_ref[...]
    sin = sin_ref[...]
    qf = jnp.dot(cq_ref[...], wq_ref[...], preferred_element_type=F32)
    kf = jnp.dot(ckv_ref[...], wk_ref[...], preferred_element_type=F32)
    kr = kr_ref[...]
    scale = QK_DIM ** -0.5 * math.log2(math.e)
    real = lax.broadcasted_iota(jnp.int32, kr.shape, 1) < QK_DIM
    for hd in range(N_HEADS):
        grp = slice(hd * HEAD_PAD, (hd + 1) * HEAD_PAD)
        qn = _rms(qf[:, grp], qg_ref[...], QK_DIM, real)
        q_ref[:, grp] = (_rope_group(qn, cos, sin) * scale).astype(BF16)
        kn = _rms(kf[:, grp] + kr, kg_ref[...], QK_DIM, real)
        k_ref[:, grp] = _rope_group(kn, cos, sin).astype(BF16)
    v_ref[...] = jnp.dot(ckv_ref[...], wv_ref[...], preferred_element_type=F32).astype(BF16)


def _qkv(cq, ckv, kr, wq, wk, wv, qg, kg, cos, sin, tb):
    t = cq.shape[0]
    nrope = cos.shape[0] // tb
    row = lambda c: pl.BlockSpec((tb, c), lambda i: (i, 0))
    full = lambda a: pl.BlockSpec(a.shape, lambda i: (0,) * a.ndim)
    rope = pl.BlockSpec((tb, HEAD_PAD), lambda i: (i % nrope, 0))
    hp = N_HEADS * HEAD_PAD
    return pl.pallas_call(
        _qkv_kernel,
        grid=(t // tb,),
        in_specs=[row(Q_LORA), row(KV_LORA), row(HEAD_PAD), full(wq), full(wk), full(wv),
                  full(qg), full(kg), rope, rope],
        out_specs=[row(hp), row(hp), row(N_HEADS * V_DIM)],
        out_shape=[jax.ShapeDtypeStruct((t, hp), BF16), jax.ShapeDtypeStruct((t, hp), BF16),
                   jax.ShapeDtypeStruct((t, N_HEADS * V_DIM), BF16)],
        compiler_params=_cparams("parallel"),
        name="qkv",
    )(cq, ckv, kr, wq, wk, wv, qg, kg, cos, sin)


_ATTN_PAIRS = 2


def _attn_kernel(*refs, tq, has_real):
    if has_real:
        q_ref, k_ref, v_ref, km_ref, vm_ref, vis_ref, o_ref = refs
    else:
        q_ref, km_ref, vm_ref, o_ref = refs
    qi = pl.program_id(2)
    nt = (((1,), (1,)), ((), ()))
    heads = range(2 * _ATTN_PAIRS)
    groups = [slice(hh * HEAD_PAD, (hh + 1) * HEAD_PAD) for hh in heads]
    vcols = [slice((hh // 2) * 2 * V_DIM, (hh // 2 + 1) * 2 * V_DIM) for hh in heads]
    qs = [q_ref[0, :, grp] for grp in groups]

    def step(s, v, carry):
        m, l, acc = carry
        m_new = jnp.maximum(m, jnp.max(s, axis=-1, keepdims=True))
        alpha = jnp.exp2(m - m_new)
        p = jnp.exp2(s - m_new)
        l = alpha * l + jnp.sum(p, axis=-1, keepdims=True)
        acc = alpha * acc + jnp.dot(p.astype(BF16), v, preferred_element_type=F32)
        return m_new, l, acc

    state = []
    for q, grp, vc in zip(qs, groups, vcols):
        s = lax.dot_general(q, km_ref[:, grp], nt, preferred_element_type=F32)
        col = lax.broadcasted_iota(jnp.int32, s.shape, 1)
        s = jnp.where(col < N_META, s, MASK_VALUE)
        m = jnp.max(s, axis=-1, keepdims=True)
        p = jnp.exp2(s - m)
        l = jnp.sum(p, axis=-1, keepdims=True)
        state.append((m, l, jnp.dot(p.astype(BF16), vm_ref[:, vc], preferred_element_type=F32)))
    state = tuple(state)

    if has_real:
        def block(r0, carry, masked):
            out = []
            for q, grp, vc, c in zip(qs, groups, vcols, carry):
                s = lax.dot_general(q, k_ref[0, pl.ds(r0, tq), grp], nt, preferred_element_type=F32)
                if masked:
                    s = jnp.where(vis_ref[...] > 0.0, s, MASK_VALUE)
                out.append(step(s, v_ref[0, pl.ds(r0, tq), vc], c))
            return tuple(out)

        state = lax.fori_loop(0, qi, lambda kb, c: block(pl.multiple_of(kb * tq, tq), c, False), state)
        state = block(pl.multiple_of(qi * tq, tq), state, True)
    outs = [acc / l for (_, l, acc) in state]
    lane = lax.broadcasted_iota(jnp.int32, outs[0].shape, 1)
    pairs = [jnp.where(lane < V_DIM, outs[2 * p], outs[2 * p + 1]) for p in range(_ATTN_PAIRS)]
    o_ref[0] = jnp.concatenate(pairs, axis=-1).astype(BF16)


def _attention(q, k, v, km, vm, tq):
    b, sq, _ = q.shape
    has_real = k is not None
    qw = 2 * _ATTN_PAIRS * HEAD_PAD
    vw = 2 * _ATTN_PAIRS * V_DIM
    qspec = pl.BlockSpec((1, tq, qw), lambda bi, hp, qi: (bi, qi, hp))
    mk = pl.BlockSpec((META_PAD, qw), lambda bi, hp, qi: (0, hp))
    mv = pl.BlockSpec((META_PAD, vw), lambda bi, hp, qi: (0, hp))
    if has_real:
        s = k.shape[1]
        chunk = jnp.arange(tq, dtype=jnp.int32) // CHUNK
        vis = (chunk[None, :] <= chunk[:, None]).astype(F32)
        in_specs = [qspec, pl.BlockSpec((1, s, qw), lambda bi, hp, qi: (bi, 0, hp)),
                    pl.BlockSpec((1, s, vw), lambda bi, hp, qi: (bi, 0, hp)), mk, mv,
                    pl.BlockSpec((tq, tq), lambda bi, hp, qi: (0, 0))]
        args = (q, k, v, km, vm, vis)
    else:
        in_specs = [qspec, mk, mv]
        args = (q, km, vm)
    return pl.pallas_call(
        functools.partial(_attn_kernel, tq=tq, has_real=has_real),
        grid=(b, N_HEADS // (2 * _ATTN_PAIRS), sq // tq),
        in_specs=in_specs,
        out_specs=pl.BlockSpec((1, tq, vw), lambda bi, hp, qi: (bi, qi, hp)),
        out_shape=jax.ShapeDtypeStruct((b, sq, N_HEADS * V_DIM), BF16),
        compiler_params=_cparams("parallel", "parallel", "arbitrary"),
        name="attention",
    )(*args)


def _merge_kernel(h_ref, o_ref, gc_ref, g2_ref, wm_ref, wo_ref, out_ref):
    ymla = jnp.dot(o_ref[...], wm_ref[...], preferred_element_type=F32)
    merged = gc_ref[...].astype(F32) + g2_ref[...].astype(F32) * ymla
    out_ref[...] = h_ref[...] + jnp.dot(merged.astype(BF16), wo_ref[...], preferred_element_type=F32)


def _merge(h, o, gc, gates, wm, wo, tb):
    t, d = h.shape
    row = lambda c: pl.BlockSpec((tb, c), lambda i: (i, 0))
    full = lambda a: pl.BlockSpec(a.shape, lambda i: (0,) * a.ndim)
    return pl.pallas_call(
        _merge_kernel,
        grid=(t // tb,),
        in_specs=[row(d), row(N_HEADS * V_DIM), row(d), pl.BlockSpec((tb, d), lambda i: (i, 1)),
                  full(wm), full(wo)],
        out_specs=row(d),
        out_shape=jax.ShapeDtypeStruct((t, d), F32),
        compiler_params=_cparams("parallel"),
        name="merge_out",
    )(h, o, gc, gates, wm, wo)


def _peer_score_kernel(h_ref, g_ref, wq_ref, keys_ref, xt_ref, s_ref):
    d = h_ref.shape[-1]
    xn = _rms(h_ref[...], g_ref[...], d)
    xt = xn.T.astype(BF16)
    xt_ref[...] = xt
    qt = jnp.dot(wq_ref[...], xt, preferred_element_type=F32)
    for g in range(2 * PEER_HEADS):
        qg = qt[g * PEER_HALF:(g + 1) * PEER_HALF, :].astype(BF16)
        s_ref[g] = jnp.dot(keys_ref[g], qg, preferred_element_type=F32)


def _peer_scores(h, g, wqt, keys, tb):
    t, d = h.shape
    full = lambda a: pl.BlockSpec(a.shape, lambda i: (0,) * a.ndim)
    ng = 2 * PEER_HEADS
    return pl.pallas_call(
        _peer_score_kernel,
        grid=(t // tb,),
        in_specs=[pl.BlockSpec((tb, d), lambda i: (i, 0)), full(g), full(wqt), full(keys)],
        out_specs=[pl.BlockSpec((d, tb), lambda i: (0, i)),
                   pl.BlockSpec((ng, N_KEYS, tb), lambda i: (0, 0, i))],
        out_shape=[jax.ShapeDtypeStruct((d, t), BF16), jax.ShapeDtypeStruct((ng, N_KEYS, t), F32)],
        compiler_params=_cparams("parallel"),
        name="peer_scores",
    )(h, g, wqt, keys)


def _extract_topk(vals, pos, sv_ref):
    big = float(vals.shape[0] * vals.shape[0])

    def body(a, carry):
        cur, rank = carry
        m = jnp.max(cur, axis=0, keepdims=True)
        first = jnp.min(jnp.where(cur == m, pos, big), axis=0, keepdims=True)
        hit = pos == first
        if sv_ref is not None:
            sv_ref[pl.ds(a, 1), :] = m
        return jnp.where(hit, NEG_INF, cur), jnp.where(hit, lax.convert_element_type(a, F32), rank)

    init = (vals, jnp.full(vals.shape, float(PEER_TOPK), F32))
    return lax.fori_loop(0, PEER_TOPK, body, init)[1]


def _topk_exact_tile(s1, s2, sv1_ref, sv2_ref):
    lanes = s1.shape[-1]
    k = PEER_TOPK
    key_pos = lax.broadcasted_iota(jnp.int32, (N_KEYS, lanes), 0).astype(F32)
    crow = lax.broadcasted_iota(jnp.int32, (k * k, lanes), 0)
    cand_pos = ((crow % k) * k + crow // k).astype(F32)
    r1 = _extract_topk(s1, key_pos, sv1_ref)
    r2 = _extract_topk(s2, key_pos, sv2_ref)
    sv1 = sv1_ref[...]
    sv2 = sv2_ref[...]
    cand = jnp.concatenate([sv1 + sv2[b:b + 1, :] for b in range(k)], axis=0)
    sel = _extract_topk(cand, cand_pos, None) < float(k)
    t1 = jnp.exp(sv1 - sv1[0:1, :])
    t2 = jnp.exp(sv2 - sv2[0:1, :])
    z = jnp.zeros((1, lanes), F32)
    n_a = jnp.zeros((k, lanes), F32)
    for b in range(k):
        sb = sel[b * k:(b + 1) * k, :]
        z = z + jnp.sum(jnp.where(sb, t1 * t2[b:b + 1, :], 0.0), axis=0, keepdims=True)
        n_a = n_a + jnp.where(sb, 1.0, 0.0)
    cnt = jnp.zeros((N_KEYS, lanes), F32)
    for a in range(k):
        cnt = jnp.where(r1 == float(a), n_a[a:a + 1, :], cnt)
    return r2, cnt, jnp.exp(s1 - sv1[0:1, :]), jnp.exp(s2 - sv2[0:1, :]) / z


def _peer_topk_small_kernel(s_ref, r2_ref, cnt_ref, e1_ref, e2_ref, sv1_ref, sv2_ref):
    r2, cnt, e1, e2 = _topk_exact_tile(s_ref[0], s_ref[1], sv1_ref, sv2_ref)
    r2_ref[0] = r2.astype(BF16)
    cnt_ref[0] = cnt
    e1_ref[0] = e1
    e2_ref[0] = e2.astype(BF16)


def _batcher_pairs(n):
    pairs = []
    p = 1
    while p < n:
        k = p
        while k >= 1:
            for j in range(k % p, n - k, 2 * k):
                for i in range(min(k, n - j - k)):
                    if (i + j) // (2 * p) == (i + j + k) // (2 * p):
                        pairs.append((i + j, i + j + k))
            k //= 2
        p *= 2
    return pairs


_SORT16 = _batcher_pairs(PEER_TOPK)
_STAIR = [(a, b) for a in range(PEER_TOPK) for b in range(PEER_TOPK) if (a + 1) * (b + 1) <= PEER_TOPK]
_TILES = LANES * SUBLANES


def _top16_desc(load, lo, n):
    k = PEER_TOPK
    if n == k:
        v = [load(lo + i) for i in range(k)]
        for i, j in _SORT16:
            v[i], v[j] = jnp.maximum(v[i], v[j]), jnp.minimum(v[i], v[j])
        return v
    x = _top16_desc(load, lo, n // 2)
    y = _top16_desc(load, lo + n // 2, n // 2)
    c = [jnp.maximum(x[i], y[k - 1 - i]) for i in range(k)]
    d = k // 2
    while d >= 1:
        for i in range(k):
            if i & d == 0:
                c[i], c[i + d] = jnp.maximum(c[i], c[i + d]), jnp.minimum(c[i], c[i + d])
        d //= 2
    return c


def _sublane_transpose(vs):
    sub = lax.broadcasted_iota(jnp.int32, vs[0].shape, 0)
    d = SUBLANES // 2
    while d >= 1:
        low = (sub & d) == 0
        nxt = list(vs)
        for j in range(SUBLANES):
            if j & d == 0:
                x, y = vs[j], vs[j + d]
                nxt[j] = jnp.where(low, x, pltpu.roll(y, d, 0))
                nxt[j + d] = jnp.where(low, pltpu.roll(x, SUBLANES - d, 0), y)
        vs = nxt
        d //= 2
    return vs


def _peer_topk_kernel(s_ref, r2_ref, cnt_ref, e1_ref, e2_ref, slab_ref, sv_ref, u_ref, misc_ref,
                      sv1_ref, sv2_ref):
    k = PEER_TOPK
    inf = float("inf")
    for half in range(2):
        for kt in range(N_KEYS // SUBLANES):
            rows = slice(kt * SUBLANES, (kt + 1) * SUBLANES)
            tiles = [s_ref[half, rows, j * LANES:(j + 1) * LANES] for j in range(SUBLANES)]
            for r, slab in enumerate(_sublane_transpose(tiles)):
                slab_ref[half * N_KEYS + kt * SUBLANES + r] = slab
    sv1 = _top16_desc(lambda i: slab_ref[i], 0, N_KEYS)
    sv2 = _top16_desc(lambda i: slab_ref[i], N_KEYS, N_KEYS)
    for a in range(k):
        sv_ref[a] = sv1[a]
        sv_ref[k + a] = sv2[a]
    tie = jnp.zeros(sv1[0].shape, F32)
    for v in (sv1, sv2):
        for a in range(k - 1):
            tie = jnp.where(v[a] == v[a + 1], 1.0, tie)
    cand = {c: sv1[c[0]] + sv2[c[1]] for c in _STAIR}
    beaten = {c: float((c[0] + 1) * (c[1] + 1) - 1) for c in _STAIR}
    dyn = {c: None for c in _STAIR}
    for x_i, x in enumerate(_STAIR):
        for y in _STAIR[x_i + 1:]:
            if (x[0] < y[0]) == (x[1] < y[1]) or x[0] == y[0] or x[1] == y[1]:
                continue
            g = jnp.where(cand[x] >= cand[y], 1.0, 0.0)
            dyn[y] = g if dyn[y] is None else dyn[y] + g
            beaten[x] += 1.0
            dyn[x] = -g if dyn[x] is None else dyn[x] - g
    sel = {c: (dyn[c] + beaten[c] if dyn[c] is not None else jnp.full(tie.shape, beaten[c])) < float(k)
           for c in _STAIR}
    t1 = [jnp.exp(sv1[a] - sv1[0]) for a in range(k)]
    t2 = [jnp.exp(sv2[b] - sv2[0]) for b in range(k)]
    z = jnp.zeros(tie.shape, F32)
    for c in _STAIR:
        z = z + jnp.where(sel[c], t1[c[0]] * t2[c[1]], 0.0)
    for b in range(k):
        u = jnp.full(tie.shape, inf, F32)
        for a in range(k):
            if (a, b) in sel:
                u = jnp.minimum(u, jnp.where(sel[(a, b)], sv1[a], inf))
        u_ref[b] = u
    misc_ref[0] = sv1[0]
    misc_ref[1] = sv2[0]
    misc_ref[2] = 1.0 / z
    misc_ref[3] = tie

    rows2 = 2 * SUBLANES
    bad = []
    for tt in range(SUBLANES):
        cols = slice(tt * LANES, (tt + 1) * LANES)
        bc = lambda ref, i: jnp.broadcast_to(ref[i, tt:tt + 1, :], (rows2, LANES))
        svb = [bc(sv_ref, k + b) for b in range(k)]
        ub = [bc(u_ref, b) for b in range(k)]
        c1, c2, zinv, last1 = bc(misc_ref, 0), bc(misc_ref, 1), bc(misc_ref, 2), bc(sv_ref, k - 1)
        n1 = jnp.zeros((rows2, LANES), F32)
        n2 = jnp.zeros((rows2, LANES), F32)
        for kt in range(N_KEYS // rows2):
            rows = slice(kt * rows2, (kt + 1) * rows2)
            x1 = s_ref[0, rows, cols]
            x2 = s_ref[1, rows, cols]
            r = jnp.full((rows2, LANES), float(k), F32)
            c = jnp.full((rows2, LANES), float(k), F32)
            for b in reversed(range(k)):
                r = jnp.where(svb[b] <= x2, float(b), r)
                c = jnp.where(ub[b] > x1, float(b), c)
            r2_ref[0, rows, cols] = r.astype(BF16)
            cnt_ref[0, rows, cols] = c
            e1_ref[0, rows, cols] = jnp.exp(x1 - c1)
            e2_ref[0, rows, cols] = (jnp.exp(x2 - c2) * zinv).astype(BF16)
            n1 = n1 + jnp.where(x1 >= last1, 1.0, 0.0)
            n2 = n2 + jnp.where(r < float(k), 1.0, 0.0)
        bad.append(jnp.abs(jnp.sum(n1, axis=0, keepdims=True) - k)
                   + jnp.abs(jnp.sum(n2, axis=0, keepdims=True) - k))

    any_tied = jnp.max(sum(bad)) + jnp.max(misc_ref[3]) > 0.0

    @pl.when(any_tied)
    def _():
        for tt in range(SUBLANES):
            tied = jnp.max(bad[tt]) + jnp.max(misc_ref[3, tt:tt + 1, :]) > 0.0

            @pl.when(tied)
            def _():
                cols = slice(tt * LANES, (tt + 1) * LANES)
                r2, cnt, e1, e2 = _topk_exact_tile(s_ref[0, :, cols], s_ref[1, :, cols], sv1_ref, sv2_ref)
                r2_ref[0, :, cols] = r2.astype(BF16)
                cnt_ref[0, :, cols] = cnt
                e1_ref[0, :, cols] = e1
                e2_ref[0, :, cols] = e2.astype(BF16)


def _peer_topk(s):
    ng, nk, t = s.shape
    fast = t % _TILES == 0
    lanes = _TILES if fast else LANES
    spec = pl.BlockSpec((1, nk, lanes), lambda i, hd: (hd, 0, i))
    shp = lambda dt: jax.ShapeDtypeStruct((PEER_HEADS, nk, t), dt)
    slab = lambda n: pltpu.VMEM((n, SUBLANES, LANES), F32)
    row = pltpu.VMEM((PEER_TOPK, LANES), F32)
    scratch = [slab(2 * N_KEYS), slab(2 * PEER_TOPK), slab(PEER_TOPK), slab(4), row, row] if fast else [row, row]
    return pl.pallas_call(
        _peer_topk_kernel if fast else _peer_topk_small_kernel,
        grid=(t // lanes, PEER_HEADS),
        in_specs=[pl.BlockSpec((2, nk, lanes), lambda i, hd: (hd, 0, i))],
        out_specs=[spec, spec, spec, spec],
        out_shape=[shp(BF16), shp(F32), shp(F32), shp(BF16)],
        scratch_shapes=scratch,
        compiler_params=_cparams("parallel", "parallel"),
        name="peer_topk",
    )(s)


def _gelu_tanh(x):
    c = -2.0 * math.sqrt(2.0 / math.pi) * math.log2(math.e)
    return x / (1.0 + jnp.exp2(x * (c + (c * 0.044715) * (x * x))))


_DENSE_CHUNK_ROWS = 8
_DENSE_KW = 512


def _peer_dense_kernel(*refs, rows_per_step, n_u, n_v):
    h_ref, xt_ref = refs[:2]
    u_refs = refs[2:2 + n_u]
    vt_refs = refs[2 + n_u:2 + n_u + n_v]
    r2_ref, cnt_ref, e1_ref, e2_ref, o_ref, acc_ref = refs[2 + n_u + n_v:]
    e = pl.program_id(1)
    lanes = xt_ref.shape[-1]
    bf16_rows = 2 * SUBLANES
    kw = _DENSE_KW

    @pl.when(e == 0)
    def _():
        acc_ref[...] = jnp.zeros_like(acc_ref)

    ce = _DENSE_CHUNK_ROWS * N_KEYS
    n_chunks = rows_per_step // _DENSE_CHUNK_ROWS
    zero = jnp.zeros((), BF16)

    def scores(c):
        out = None
        for k, u_ref in enumerate(u_refs):
            part = jnp.dot(u_ref[c * ce:(c + 1) * ce, :], xt_ref[k * kw:(k + 1) * kw, :],
                           preferred_element_type=F32)
            out = part if out is None else part + out
        return out

    def packed_row(ref, hd, i):
        row = jnp.broadcast_to(ref[hd, pl.ds(i, 1), :], (bf16_rows, lanes))
        return jnp.tile(row.astype(BF16), (N_KEYS // bf16_rows, 1))

    a_next = scores(0)
    total = None
    for c in range(n_chunks):
        a = a_next
        ws = []
        for ii in range(_DENSE_CHUNK_ROWS):
            i = e * rows_per_step + c * _DENSE_CHUNK_ROWS + ii
            w = None
            for hd in range(PEER_HEADS):
                term = packed_row(e1_ref, hd, i) * jnp.where(r2_ref[hd] < packed_row(cnt_ref, hd, i),
                                                             e2_ref[hd], zero)
                w = term if w is None else w + term
            ws.append(w)
        if c + 1 < n_chunks:
            a_next = scores(c + 1)
        act = _gelu_tanh(a.astype(BF16)) * jnp.concatenate(ws, axis=0)
        for k in range(ce // kw):
            part = jnp.dot(vt_refs[c * (ce // kw) + k][...], act[k * kw:(k + 1) * kw, :],
                           preferred_element_type=F32)
            total = part if total is None else part + total
    acc_ref[...] += total

    @pl.when(e == pl.num_programs(1) - 1)
    def _():
        o_ref[...] = h_ref[...] + acc_ref[...].T


def _peer_dense(h, xt, u, vt, layer, r2, cnt, e1, e2, lanes, rows_per_step):
    t, d = h.shape
    ne = u.shape[1]
    eb = rows_per_step * N_KEYS
    kw = _DENSE_KW
    n_u, n_v = d // kw, eb // kw
    tok = pl.BlockSpec((PEER_HEADS, N_KEYS, lanes), lambda i, e: (0, 0, i))
    u_specs = [pl.BlockSpec((None, eb, kw), functools.partial(lambda i, e, k: (layer, e, k), k=k))
               for k in range(n_u)]
    v_specs = [pl.BlockSpec((None, d, kw), functools.partial(lambda i, e, k: (layer, 0, e * n_v + k), k=k))
               for k in range(n_v)]
    return pl.pallas_call(
        functools.partial(_peer_dense_kernel, rows_per_step=rows_per_step, n_u=n_u, n_v=n_v),
        grid=(t // lanes, ne // eb),
        in_specs=[pl.BlockSpec((lanes, d), lambda i, e: (i, 0)),
                  pl.BlockSpec((d, lanes), lambda i, e: (0, i))] + u_specs + v_specs + [tok, tok, tok, tok],
        out_specs=pl.BlockSpec((lanes, d), lambda i, e: (i, 0)),
        out_shape=jax.ShapeDtypeStruct((t, d), F32),
        scratch_shapes=[pltpu.VMEM((d, lanes), F32)],
        compiler_params=_cparams("parallel", "arbitrary"),
        name="peer_dense",
    )(h, xt, *([u] * n_u), *([vt] * n_v), r2, cnt, e1, e2)


def _pad_groups(w, real, n_groups):
    k = w.shape[0]
    w = w.reshape(k, n_groups, real)
    return jnp.pad(w, ((0, 0), (0, 0), (0, HEAD_PAD - real))).reshape(k, n_groups * HEAD_PAD)


def _with_rope_copy(w):
    first = w[..., ROPE_LO:ROPE_LO + ROPE_HALF]
    pad = jnp.zeros(w.shape[:-1] + (HEAD_PAD - QK_DIM - ROPE_HALF,), w.dtype)
    return jnp.concatenate([w, first, pad], axis=-1)


def _layer_params(l, w_in, w_conv_out, w_uq, w_ukv, q_norm_g, k_norm_g, w_mla_out, w_out, peer_wq,
                  peer_keys):
    d = w_in.shape[1]
    wi = w_in[l]
    o0 = 2 * D_CONV
    o1 = o0 + Q_LORA
    o2 = o1 + KV_LORA
    o3 = o2 + ROPE_DIM
    rope_cols = _with_rope_copy(jnp.pad(wi[:, o2:o3], ((0, 0), (ROPE_LO, 0))))
    w_in_p = jnp.concatenate([wi[:, :o2], rope_cols, wi[:, o3:]], axis=1).astype(BF16)
    wkv = w_ukv[l].reshape(KV_LORA, N_HEADS, NOPE_DIM + V_DIM)
    wk = _pad_groups(wkv[:, :, :NOPE_DIM].reshape(KV_LORA, N_HEADS * NOPE_DIM), NOPE_DIM, N_HEADS)
    wv = wkv[:, :, NOPE_DIM:].reshape(KV_LORA, N_HEADS * V_DIM)
    padg = lambda g: _with_rope_copy(g).reshape(1, HEAD_PAD)
    return dict(
        w_in=w_in_p,
        w_conv_out=w_conv_out[l].astype(BF16),
        wq=_with_rope_copy(w_uq[l].reshape(Q_LORA, N_HEADS, QK_DIM)).reshape(Q_LORA, N_HEADS * HEAD_PAD).astype(BF16),
        wk=wk.astype(BF16),
        wv=wv.astype(BF16),
        qg=padg(q_norm_g[l]),
        kg=padg(k_norm_g[l]),
        w_mla_out=w_mla_out[l].astype(BF16),
        w_out=w_out[l].astype(BF16),
        peer_wqt=peer_wq[l],
        peer_keys=peer_keys[l].reshape(2 * PEER_HEADS, N_KEYS, PEER_HALF).astype(BF16),
    )


def _rope_tables(length):
    pos = jnp.arange(length, dtype=F32)
    inv = 1.0 / (ROPE_THETA ** (jnp.arange(0, ROPE_DIM, 2, dtype=F32) / ROPE_DIM))
    ang = pos[:, None] * inv[None, :]
    cos = jnp.concatenate([jnp.cos(ang), jnp.cos(ang)], axis=-1)
    sin = jnp.concatenate([-jnp.sin(ang), jnp.sin(ang)], axis=-1)
    cos = jnp.pad(cos, ((0, 0), (ROPE_LO, 0)), constant_values=1.0)
    cos = jnp.pad(cos, ((0, 0), (0, HEAD_PAD - ROPE_LO - ROPE_DIM)))
    sin = jnp.pad(sin, ((0, 0), (ROPE_LO, HEAD_PAD - ROPE_LO - ROPE_DIM)))
    return cos, sin


def _row(v):
    return v.reshape(1, -1)


def kernel(x, meta_tokens, mix_norm_g, w_in, conv_w, conv_b, conv_ln_g, conv_ln_b, w_conv_out,
           q_a_norm_g, w_uq, kv_a_norm_g, w_ukv, q_norm_g, k_norm_g, w_mla_out, w_out, ffn_norm_g,
           peer_wq, peer_keys, peer_u, peer_v):
    bsz, seq, d = x.shape
    depth = w_in.shape[0]
    t = bsz * seq
    assert meta_tokens.shape[0] == N_META and seq % 256 == 0 and d % LANES == 0
    tb = min(1024, seq)
    tq = min(512, seq)
    dense_lanes = min(512, t)
    rows_per_step = 16

    cos, sin = _rope_tables(N_META + seq)
    h = x.reshape(t, d)
    hm = meta_tokens.astype(x.dtype)
    zero_pre = jnp.zeros((N_META, D_CONV), F32)

    peer_wqt = jnp.swapaxes(peer_wq, 1, 2).astype(BF16)
    peer_ub = peer_u.astype(BF16)
    peer_vt = jnp.swapaxes(peer_v, 1, 2).astype(BF16)

    for l in range(depth):
        p = _layer_params(l, w_in, w_conv_out, w_uq, w_ukv, q_norm_g, k_norm_g, w_mla_out, w_out,
                          peer_wqt, peer_keys)
        last = l == depth - 1
        g_mix, qag, kvag = _row(mix_norm_g[l]), _row(q_a_norm_g[l]), _row(kv_a_norm_g[l])
        conv_args = (conv_w[l], _row(conv_b[l]), _row(conv_ln_g[l]), _row(conv_ln_b[l]), p["w_conv_out"])

        um, cqm, ckvm, krm, gatem = _inproj(hm, g_mix, p["w_in"], qag, kvag, N_META)
        qm, km, vm = _qkv(cqm, ckvm, krm, p["wq"], p["wk"], p["wv"], p["qg"], p["kg"],
                          cos[:N_META], sin[:N_META], N_META)
        km_p = jnp.pad(km, ((0, META_PAD - N_META), (0, 0)))
        vm_p = jnp.pad(vm, ((0, META_PAD - N_META), (0, 0)))

        u, cq, ckv, kr, gates = _inproj(h, g_mix, p["w_in"], qag, kvag, tb)
        gc = _conv_branch(u.reshape(bsz, seq, D_CONV), um, *conv_args, gates.reshape(bsz, seq, 2 * d))
        q, k, v = _qkv(cq, ckv, kr, p["wq"], p["wk"], p["wv"], p["qg"], p["kg"],
                       cos[N_META:], sin[N_META:], tb)
        hp = N_HEADS * HEAD_PAD
        o = _attention(q.reshape(bsz, seq, hp), k.reshape(bsz, seq, hp),
                       v.reshape(bsz, seq, N_HEADS * V_DIM), km_p, vm_p, tq)
        h1 = _merge(h, o.reshape(t, N_HEADS * V_DIM), gc.reshape(t, d), gates, p["w_mla_out"],
                    p["w_out"], tb)
        xt, s = _peer_scores(h1, _row(ffn_norm_g[l]), p["peer_wqt"], p["peer_keys"], tb)
        r2, cnt, e1, e2 = _peer_topk(s)
        h = _peer_dense(h1, xt, peer_ub, peer_vt, l, r2, cnt, e1, e2, dense_lanes, rows_per_step)

        if not last:
            gcm = _conv_branch(um.reshape(1, N_META, D_CONV), zero_pre, *conv_args,
                               gatem.reshape(1, N_META, 2 * d))
            om = _attention(qm.reshape(1, N_META, hp), None, None, km_p, vm_p, N_META)
            hm1 = _merge(hm, om.reshape(N_META, N_HEADS * V_DIM), gcm.reshape(N_META, d), gatem,
                         p["w_mla_out"], p["w_out"], N_META)
            hm1_p = jnp.pad(hm1, ((0, LANES - N_META), (0, 0)))
            xtm, sm = _peer_scores(hm1_p, _row(ffn_norm_g[l]), p["peer_wqt"], p["peer_keys"], LANES)
            r2m, cntm, e1m, e2m = _peer_topk(sm)
            hm = _peer_dense(hm1_p, xtm, peer_ub, peer_vt, l, r2m, cntm, e1m, e2m, LANES,
                             rows_per_step)[:N_META]

    return h.reshape(bsz, seq, d)
```

```python
import functools
import math

import jax
import jax.numpy as jnp
from jax import lax
from jax.experimental import pallas as pl
from jax.experimental.pallas import tpu as pltpu

F32 = jnp.float32
BF16 = jnp.bfloat16

CHUNK = 64
N_META = 16
D_CONV = 512
CONV_WIDTH = 31
N_HEADS = 8
Q_LORA = 256
KV_LORA = 128
NOPE_DIM = 64
ROPE_DIM = 32
QK_DIM = NOPE_DIM + ROPE_DIM
V_DIM = 64
ROPE_THETA = 10000.0
PEER_HEADS = 8
PEER_HALF = 128
N_KEYS = 128
PEER_TOPK = 16
EPS = 1e-6
MASK_VALUE = -1e30

LANES = 128
SUBLANES = 8
VMEM_LIMIT = 48 * 1024 * 1024

HEAD_PAD = LANES
ROPE_LO = NOPE_DIM
ROPE_HALF = ROPE_DIM // 2
META_PAD = LANES
NEG_INF = float("-inf")


def _cparams(*sem):
    return pltpu.CompilerParams(dimension_semantics=sem, vmem_limit_bytes=VMEM_LIMIT)


def _rms(x, g, n, valid=None):
    xs = x if valid is None else jnp.where(valid, x, 0.0)
    ms = jnp.sum(xs * xs, axis=-1, keepdims=True) * (1.0 / n)
    return x * lax.rsqrt(ms + EPS) * g


_C_CONV = 0
_C_Q = 2 * D_CONV
_C_KV = _C_Q + Q_LORA
_C_ROPE = _C_KV + KV_LORA
_C_GATE = _C_ROPE + HEAD_PAD


def _inproj_kernel(h_ref, g_ref, w_ref, qg_ref, kvg_ref, u_ref, cq_ref, ckv_ref, kr_ref, gate_ref):
    d = h_ref.shape[-1]
    xn = _rms(h_ref[...], g_ref[...], d).astype(BF16)

    def proj(lo, hi):
        return jnp.dot(xn, w_ref[:, lo:hi], preferred_element_type=F32)

    a = proj(_C_CONV, _C_CONV + D_CONV)
    b = proj(_C_CONV + D_CONV, _C_Q)
    u_ref[...] = a * jax.nn.sigmoid(b)
    cq_ref[...] = _rms(proj(_C_Q, _C_KV), qg_ref[...], Q_LORA).astype(BF16)
    ckv_ref[...] = _rms(proj(_C_KV, _C_ROPE), kvg_ref[...], KV_LORA).astype(BF16)
    kr_ref[...] = proj(_C_ROPE, _C_GATE)
    gate_ref[...] = jax.nn.sigmoid(proj(_C_GATE, _C_GATE + 2 * d)).astype(BF16)


def _inproj(h, g, w, qg, kvg, tb):
    t, d = h.shape
    n = w.shape[1]
    row = lambda c: pl.BlockSpec((tb, c), lambda i: (i, 0))
    full = lambda a: pl.BlockSpec(a.shape, lambda i: (0,) * a.ndim)
    return pl.pallas_call(
        _inproj_kernel,
        grid=(t // tb,),
        in_specs=[row(d), full(g), full(w), full(qg), full(kvg)],
        out_specs=[row(D_CONV), row(Q_LORA), row(KV_LORA), row(HEAD_PAD), row(2 * d)],
        out_shape=[jax.ShapeDtypeStruct((t, D_CONV), F32),
                   jax.ShapeDtypeStruct((t, Q_LORA), BF16),
                   jax.ShapeDtypeStruct((t, KV_LORA), BF16),
                   jax.ShapeDtypeStruct((t, HEAD_PAD), F32),
                   jax.ShapeDtypeStruct((t, 2 * d), BF16)],
        compiler_params=_cparams("parallel"),
        name="inproj",
    )(h, g, w, qg, kvg)


_CONV_PAD = 32
_CONV_TILE = 256


def _conv_kernel(u_ref, pre_ref, cw_ref, cb_ref, lg_ref, lb_ref, wo_ref, gate_ref, o_ref,
                 buf_ref, act_ref, sh_ref, *, rows):
    s = u_ref.shape[1]
    npre = pre_ref.shape[0]
    buf_ref[0:_CONV_PAD - npre, :] = jnp.zeros((_CONV_PAD - npre, D_CONV), F32)
    buf_ref[_CONV_PAD - npre:_CONV_PAD, :] = pre_ref[...]
    buf_ref[_CONV_PAD:, :] = u_ref[0]
    shift = _CONV_PAD - (CONV_WIDTH - 1)

    def tile(i, carry):
        r0 = pl.multiple_of(i * rows, rows)
        acc = jnp.zeros((rows, D_CONV), F32) + cb_ref[...]
        win = buf_ref[pl.ds(r0, rows + _CONV_PAD), :]
        for res in range(1, SUBLANES):
            sh_ref[res - 1] = win[res:res + rows + _CONV_PAD - SUBLANES, :]
        for k in range(CONV_WIDTH):
            res, off = (shift + k) % SUBLANES, (shift + k) // SUBLANES * SUBLANES
            x = win[off:off + rows, :] if res == 0 else sh_ref[res - 1, off:off + rows, :]
            acc = acc + cw_ref[k:k + 1, :] * x
        mu = jnp.mean(acc, axis=-1, keepdims=True)
        xc = acc - mu
        var = jnp.mean(xc * xc, axis=-1, keepdims=True)
        y = xc * lax.rsqrt(var + EPS) * lg_ref[...] + lb_ref[...]
        act_ref[pl.ds(r0, rows), :] = (y * jax.nn.sigmoid(y)).astype(BF16)
        return carry

    lax.fori_loop(0, s // rows, tile, 0)
    y = jnp.dot(act_ref[...], wo_ref[...], preferred_element_type=F32)
    o_ref[0] = (gate_ref[0].astype(F32) * y).astype(BF16)


def _conv_branch(u, pre, cw, cb, lg, lb, wo, gates):
    b, s, _ = u.shape
    d = wo.shape[1]
    rows = min(_CONV_TILE, s)
    full = lambda a: pl.BlockSpec(a.shape, lambda i: (0,) * a.ndim)
    return pl.pallas_call(
        functools.partial(_conv_kernel, rows=rows),
        grid=(b,),
        in_specs=[pl.BlockSpec((1, s, D_CONV), lambda i: (i, 0, 0)), full(pre), full(cw), full(cb),
                  full(lg), full(lb), full(wo), pl.BlockSpec((1, s, d), lambda i: (i, 0, 0))],
        out_specs=pl.BlockSpec((1, s, d), lambda i: (i, 0, 0)),
        out_shape=jax.ShapeDtypeStruct((b, s, d), BF16),
        scratch_shapes=[pltpu.VMEM((_CONV_PAD + s, D_CONV), F32), pltpu.VMEM((s, D_CONV), BF16),
                        pltpu.VMEM((SUBLANES - 1, rows + _CONV_PAD - SUBLANES, D_CONV), F32)],
        compiler_params=_cparams("parallel"),
        name="conv_branch",
    )(u, pre, cw, cb, lg, lb, wo, gates)


def _rope_group(x, cos, sin_signed):
    return x * cos + pltpu.roll(x, LANES - ROPE_HALF, 1) * sin_signed


def _qkv_kernel(cq_ref, ckv_ref, kr_ref, wq_ref, wk_ref, wv_ref, qg_ref, kg_ref, cos_ref, sin_ref,
                q_ref, k_ref, v_ref):
    cos = cos_ref[...]
    sin = sin_ref[...]
    qf = jnp.dot(cq_ref[...], wq_ref[...], preferred_element_type=F32)
    kf = jnp.dot(ckv_ref[...], wk_ref[...], preferred_element_type=F32)
    kr = kr_ref[...]
    scale = QK_DIM ** -0.5 * math.log2(math.e)
    real = lax.broadcasted_iota(jnp.int32, kr.shape, 1) < QK_DIM
    for hd in range(N_HEADS):
        grp = slice(hd * HEAD_PAD, (hd + 1) * HEAD_PAD)
        qn = _rms(qf[:, grp], qg_ref[...], QK_DIM, real)
        q_ref[:, grp] = (_rope_group(qn, cos, sin) * scale).astype(BF16)
        kn = _rms(kf[:, grp] + kr, kg_ref[...], QK_DIM, real)
        k_ref[:, grp] = _rope_group(kn, cos, sin).astype(BF16)
    v_ref[...] = jnp.dot(ckv_ref[...], wv_ref[...], preferred_element_type=F32).astype(BF16)


def _qkv(cq, ckv, kr, wq, wk, wv, qg, kg, cos, sin, tb):
    t = cq.shape[0]
    nrope = cos.shape[0] // tb
    row = lambda c: pl.BlockSpec((tb, c), lambda i: (i, 0))
    full = lambda a: pl.BlockSpec(a.shape, lambda i: (0,) * a.ndim)
    rope = pl.BlockSpec((tb, HEAD_PAD), lambda i: (i % nrope, 0))
    hp = N_HEADS * HEAD_PAD
    return pl.pallas_call(
        _qkv_kernel,
        grid=(t // tb,),
        in_specs=[row(Q_LORA), row(KV_LORA), row(HEAD_PAD), full(wq), full(wk), full(wv),
                  full(qg), full(kg), rope, rope],
        out_specs=[row(hp), row(hp), row(N_HEADS * V_DIM)],
        out_shape=[jax.ShapeDtypeStruct((t, hp), BF16), jax.ShapeDtypeStruct((t, hp), BF16),
                   jax.ShapeDtypeStruct((t, N_HEADS * V_DIM), BF16)],
        compiler_params=_cparams("parallel"),
        name="qkv",
    )(cq, ckv, kr, wq, wk, wv, qg, kg, cos, sin)


_ATTN_PAIRS = 2


def _attn_kernel(*refs, tq, has_real):
    if has_real:
        q_ref, k_ref, v_ref, km_ref, vm_ref, vis_ref, o_ref = refs
    else:
        q_ref, km_ref, vm_ref, o_ref = refs
    qi = pl.program_id(2)
    nt = (((1,), (1,)), ((), ()))
    heads = range(2 * _ATTN_PAIRS)
    groups = [slice(hh * HEAD_PAD, (hh + 1) * HEAD_PAD) for hh in heads]
    vcols = [slice((hh // 2) * 2 * V_DIM, (hh // 2 + 1) * 2 * V_DIM) for hh in heads]
    qs = [q_ref[0, :, grp] for grp in groups]

    def step(s, v, carry):
        m, l, acc = carry
        m_new = jnp.maximum(m, jnp.max(s, axis=-1, keepdims=True))
        alpha = jnp.exp2(m - m_new)
        p = jnp.exp2(s - m_new)
        l = alpha * l + jnp.sum(p, axis=-1, keepdims=True)
        acc = alpha * acc + jnp.dot(p.astype(BF16), v, preferred_element_type=F32)
        return m_new, l, acc

    state = []
    for q, grp, vc in zip(qs, groups, vcols):
        s = lax.dot_general(q, km_ref[:, grp], nt, preferred_element_type=F32)
        col = lax.broadcasted_iota(jnp.int32, s.shape, 1)
        s = jnp.where(col < N_META, s, MASK_VALUE)
        m = jnp.max(s, axis=-1, keepdims=True)
        p = jnp.exp2(s - m)
        l = jnp.sum(p, axis=-1, keepdims=True)
        state.append((m, l, jnp.dot(p.astype(BF16), vm_ref[:, vc], preferred_element_type=F32)))
    state = tuple(state)

    if has_real:
        def block(r0, carry, masked):
            out = []
            for q, grp, vc, c in zip(qs, groups, vcols, carry):
                s = lax.dot_general(q, k_ref[0, pl.ds(r0, tq), grp], nt, preferred_element_type=F32)
                if masked:
                    s = jnp.where(vis_ref[...] > 0.0, s, MASK_VALUE)
                out.append(step(s, v_ref[0, pl.ds(r0, tq), vc], c))
            return tuple(out)

        state = lax.fori_loop(0, qi, lambda kb, c: block(pl.multiple_of(kb * tq, tq), c, False), state)
        state = block(pl.multiple_of(qi * tq, tq), state, True)
    outs = [acc / l for (_, l, acc) in state]
    lane = lax.broadcasted_iota(jnp.int32, outs[0].shape, 1)
    pairs = [jnp.where(lane < V_DIM, outs[2 * p], outs[2 * p + 1]) for p in range(_ATTN_PAIRS)]
    o_ref[0] = jnp.concatenate(pairs, axis=-1).astype(BF16)


def _attention(q, k, v, km, vm, tq):
    b, sq, _ = q.shape
    has_real = k is not None
    qw = 2 * _ATTN_PAIRS * HEAD_PAD
    vw = 2 * _ATTN_PAIRS * V_DIM
    qspec = pl.BlockSpec((1, tq, qw), lambda bi, hp, qi: (bi, qi, hp))
    mk = pl.BlockSpec((META_PAD, qw), lambda bi, hp, qi: (0, hp))
    mv = pl.BlockSpec((META_PAD, vw), lambda bi, hp, qi: (0, hp))
    if has_real:
        s = k.shape[1]
        chunk = jnp.arange(tq, dtype=jnp.int32) // CHUNK
        vis = (chunk[None, :] <= chunk[:, None]).astype(F32)
        in_specs = [qspec, pl.BlockSpec((1, s, qw), lambda bi, hp, qi: (bi, 0, hp)),
                    pl.BlockSpec((1, s, vw), lambda bi, hp, qi: (bi, 0, hp)), mk, mv,
                    pl.BlockSpec((tq, tq), lambda bi, hp, qi: (0, 0))]
        args = (q, k, v, km, vm, vis)
    else:
        in_specs = [qspec, mk, mv]
        args = (q, km, vm)
    return pl.pallas_call(
        functools.partial(_attn_kernel, tq=tq, has_real=has_real),
        grid=(b, N_HEADS // (2 * _ATTN_PAIRS), sq // tq),
        in_specs=in_specs,
        out_specs=pl.BlockSpec((1, tq, vw), lambda bi, hp, qi: (bi, qi, hp)),
        out_shape=jax.ShapeDtypeStruct((b, sq, N_HEADS * V_DIM), BF16),
        compiler_params=_cparams("parallel", "parallel", "arbitrary"),
        name="attention",
    )(*args)


def _merge_kernel(h_ref, o_ref, gc_ref, g2_ref, wm_ref, wo_ref, out_ref):
    ymla = jnp.dot(o_ref[...], wm_ref[...], preferred_element_type=F32)
    merged = gc_ref[...].astype(F32) + g2_ref[...].astype(F32) * ymla
    out_ref[...] = h_ref[...] + jnp.dot(merged.astype(BF16), wo_ref[...], preferred_element_type=F32)


def _merge(h, o, gc, gates, wm, wo, tb):
    t, d = h.shape
    row = lambda c: pl.BlockSpec((tb, c), lambda i: (i, 0))
    full = lambda a: pl.BlockSpec(a.shape, lambda i: (0,) * a.ndim)
    return pl.pallas_call(
        _merge_kernel,
        grid=(t // tb,),
        in_specs=[row(d), row(N_HEADS * V_DIM), row(d), pl.BlockSpec((tb, d), lambda i: (i, 1)),
                  full(wm), full(wo)],
        out_specs=row(d),
        out_shape=jax.ShapeDtypeStruct((t, d), F32),
        compiler_params=_cparams("parallel"),
        name="merge_out",
    )(h, o, gc, gates, wm, wo)


def _peer_score_kernel(h_ref, g_ref, wq_ref, keys_ref, xt_ref, s_ref):
    d = h_ref.shape[-1]
    xn = _rms(h_ref[...], g_ref[...], d)
    xt = xn.T.astype(BF16)
    xt_ref[...] = xt
    qt = jnp.dot(wq_ref[...], xt, preferred_element_type=F32)
    for g in range(2 * PEER_HEADS):
        qg = qt[g * PEER_HALF:(g + 1) * PEER_HALF, :].astype(BF16)
        s_ref[g] = jnp.dot(keys_ref[g], qg, preferred_element_type=F32)


def _peer_scores(h, g, wqt, keys, tb):
    t, d = h.shape
    full = lambda a: pl.BlockSpec(a.shape, lambda i: (0,) * a.ndim)
    ng = 2 * PEER_HEADS
    return pl.pallas_call(
        _peer_score_kernel,
        grid=(t // tb,),
        in_specs=[pl.BlockSpec((tb, d), lambda i: (i, 0)), full(g), full(wqt), full(keys)],
        out_specs=[pl.BlockSpec((d, tb), lambda i: (0, i)),
                   pl.BlockSpec((ng, N_KEYS, tb), lambda i: (0, 0, i))],
        out_shape=[jax.ShapeDtypeStruct((d, t), BF16), jax.ShapeDtypeStruct((ng, N_KEYS, t), F32)],
        compiler_params=_cparams("parallel"),
        name="peer_scores",
    )(h, g, wqt, keys)


def _merge_scores_kernel(h_ref, o_ref, gc_ref, g2_ref, wm_ref, wo_ref, g_ref, wq_ref, keys_ref,
                         out_ref, xt_ref, s_ref):
    _merge_kernel(h_ref, o_ref, gc_ref, g2_ref, wm_ref, wo_ref, out_ref)
    _peer_score_kernel(out_ref, g_ref, wq_ref, keys_ref, xt_ref, s_ref)


def _merge_scores(h, o, gc, gates, wm, wo, g, wqt, keys, tb):
    t, d = h.shape
    ng = 2 * PEER_HEADS
    row = lambda c: pl.BlockSpec((tb, c), lambda i: (i, 0))
    full = lambda a: pl.BlockSpec(a.shape, lambda i: (0,) * a.ndim)
    return pl.pallas_call(
        _merge_scores_kernel,
        grid=(t // tb,),
        in_specs=[row(d), row(N_HEADS * V_DIM), row(d), pl.BlockSpec((tb, d), lambda i: (i, 1)),
                  full(wm), full(wo), full(g), full(wqt), full(keys)],
        out_specs=[row(d), pl.BlockSpec((d, tb), lambda i: (0, i)),
                   pl.BlockSpec((ng, N_KEYS, tb), lambda i: (0, 0, i))],
        out_shape=[jax.ShapeDtypeStruct((t, d), F32), jax.ShapeDtypeStruct((d, t), BF16),
                   jax.ShapeDtypeStruct((ng, N_KEYS, t), F32)],
        compiler_params=_cparams("parallel"),
        name="merge_scores",
    )(h, o, gc, gates, wm, wo, g, wqt, keys)


def _extract_topk(vals, pos, sv_ref):
    big = float(vals.shape[0] * vals.shape[0])

    def body(a, carry):
        cur, rank = carry
        m = jnp.max(cur, axis=0, keepdims=True)
        first = jnp.min(jnp.where(cur == m, pos, big), axis=0, keepdims=True)
        hit = pos == first
        if sv_ref is not None:
            sv_ref[pl.ds(a, 1), :] = m
        return jnp.where(hit, NEG_INF, cur), jnp.where(hit, lax.convert_element_type(a, F32), rank)

    init = (vals, jnp.full(vals.shape, float(PEER_TOPK), F32))
    return lax.fori_loop(0, PEER_TOPK, body, init)[1]


def _topk_exact_tile(s1, s2, sv1_ref, sv2_ref):
    lanes = s1.shape[-1]
    k = PEER_TOPK
    key_pos = lax.broadcasted_iota(jnp.int32, (N_KEYS, lanes), 0).astype(F32)
    crow = lax.broadcasted_iota(jnp.int32, (k * k, lanes), 0)
    cand_pos = ((crow % k) * k + crow // k).astype(F32)
    r1 = _extract_topk(s1, key_pos, sv1_ref)
    r2 = _extract_topk(s2, key_pos, sv2_ref)
    sv1 = sv1_ref[...]
    sv2 = sv2_ref[...]
    cand = jnp.concatenate([sv1 + sv2[b:b + 1, :] for b in range(k)], axis=0)
    sel = _extract_topk(cand, cand_pos, None) < float(k)
    t1 = jnp.exp(sv1 - sv1[0:1, :])
    t2 = jnp.exp(sv2 - sv2[0:1, :])
    z = jnp.zeros((1, lanes), F32)
    n_a = jnp.zeros((k, lanes), F32)
    for b in range(k):
        sb = sel[b * k:(b + 1) * k, :]
        z = z + jnp.sum(jnp.where(sb, t1 * t2[b:b + 1, :], 0.0), axis=0, keepdims=True)
        n_a = n_a + jnp.where(sb, 1.0, 0.0)
    cnt = jnp.zeros((N_KEYS, lanes), F32)
    for a in range(k):
        cnt = jnp.where(r1 == float(a), n_a[a:a + 1, :], cnt)
    return r2, cnt, jnp.exp(s1 - sv1[0:1, :]), jnp.exp(s2 - sv2[0:1, :]) / z


def _peer_topk_small_kernel(s_ref, r2_ref, cnt_ref, e1_ref, e2_ref, sv1_ref, sv2_ref):
    r2, cnt, e1, e2 = _topk_exact_tile(s_ref[0], s_ref[1], sv1_ref, sv2_ref)
    r2_ref[0] = r2.astype(BF16)
    cnt_ref[0] = cnt
    e1_ref[0] = e1
    e2_ref[0] = e2.astype(BF16)


def _batcher_pairs(n):
    pairs = []
    p = 1
    while p < n:
        k = p
        while k >= 1:
            for j in range(k % p, n - k, 2 * k):
                for i in range(min(k, n - j - k)):
                    if (i + j) // (2 * p) == (i + j + k) // (2 * p):
                        pairs.append((i + j, i + j + k))
            k //= 2
        p *= 2
    return pairs


_SORT16 = _batcher_pairs(PEER_TOPK)
_STAIR = [(a, b) for a in range(PEER_TOPK) for b in range(PEER_TOPK) if (a + 1) * (b + 1) <= PEER_TOPK]
_TILES = LANES * SUBLANES


def _top16_desc(load, lo, n):
    k = PEER_TOPK
    if n == k:
        v = [load(lo + i) for i in range(k)]
        for i, j in _SORT16:
            v[i], v[j] = jnp.maximum(v[i], v[j]), jnp.minimum(v[i], v[j])
        return v
    x = _top16_desc(load, lo, n // 2)
    y = _top16_desc(load, lo + n // 2, n // 2)
    c = [jnp.maximum(x[i], y[k - 1 - i]) for i in range(k)]
    d = k // 2
    while d >= 1:
        for i in range(k):
            if i & d == 0:
                c[i], c[i + d] = jnp.maximum(c[i], c[i + d]), jnp.minimum(c[i], c[i + d])
        d //= 2
    return c


def _sublane_transpose(vs):
    sub = lax.broadcasted_iota(jnp.int32, vs[0].shape, 0)
    d = SUBLANES // 2
    while d >= 1:
        low = (sub & d) == 0
        nxt = list(vs)
        for j in range(SUBLANES):
            if j & d == 0:
                x, y = vs[j], vs[j + d]
                nxt[j] = jnp.where(low, x, pltpu.roll(y, d, 0))
                nxt[j + d] = jnp.where(low, pltpu.roll(x, SUBLANES - d, 0), y)
        vs = nxt
        d //= 2
    return vs


def _peer_topk_kernel(s_ref, r2_ref, cnt_ref, e1_ref, e2_ref, slab_ref, sv_ref, u_ref, misc_ref,
                      sv1_ref, sv2_ref):
    k = PEER_TOPK
    inf = float("inf")
    for half in range(2):
        for kt in range(N_KEYS // SUBLANES):
            rows = slice(kt * SUBLANES, (kt + 1) * SUBLANES)
            tiles = [s_ref[half, rows, j * LANES:(j + 1) * LANES] for j in range(SUBLANES)]
            for r, slab in enumerate(_sublane_transpose(tiles)):
                slab_ref[half * N_KEYS + kt * SUBLANES + r] = slab
    sv1 = _top16_desc(lambda i: slab_ref[i], 0, N_KEYS)
    sv2 = _top16_desc(lambda i: slab_ref[i], N_KEYS, N_KEYS)
    for a in range(k):
        sv_ref[a] = sv1[a]
        sv_ref[k + a] = sv2[a]
    tie = jnp.zeros(sv1[0].shape, F32)
    for v in (sv1, sv2):
        for a in range(k - 1):
            tie = jnp.where(v[a] == v[a + 1], 1.0, tie)
    cand = {c: sv1[c[0]] + sv2[c[1]] for c in _STAIR}
    beaten = {c: float((c[0] + 1) * (c[1] + 1) - 1) for c in _STAIR}
    dyn = {c: None for c in _STAIR}
    for x_i, x in enumerate(_STAIR):
        for y in _STAIR[x_i + 1:]:
            if (x[0] < y[0]) == (x[1] < y[1]) or x[0] == y[0] or x[1] == y[1]:
                continue
            g = jnp.where(cand[x] >= cand[y], 1.0, 0.0)
            dyn[y] = g if dyn[y] is None else dyn[y] + g
            beaten[x] += 1.0
            dyn[x] = -g if dyn[x] is None else dyn[x] - g
    sel = {c: (dyn[c] + beaten[c] if dyn[c] is not None else jnp.full(tie.shape, beaten[c])) < float(k)
           for c in _STAIR}
    t1 = [jnp.exp(sv1[a] - sv1[0]) for a in range(k)]
    t2 = [jnp.exp(sv2[b] - sv2[0]) for b in range(k)]
    z = jnp.zeros(tie.shape, F32)
    for c in _STAIR:
        z = z + jnp.where(sel[c], t1[c[0]] * t2[c[1]], 0.0)
    for b in range(k):
        u = jnp.full(tie.shape, inf, F32)
        for a in range(k):
            if (a, b) in sel:
                u = jnp.minimum(u, jnp.where(sel[(a, b)], sv1[a], inf))
        u_ref[b] = u
    misc_ref[0] = sv1[0]
    misc_ref[1] = sv2[0]
    misc_ref[2] = 1.0 / z
    misc_ref[3] = tie

    rows2 = 2 * SUBLANES
    bad = []
    for tt in range(SUBLANES):
        cols = slice(tt * LANES, (tt + 1) * LANES)
        bc = lambda ref, i: jnp.broadcast_to(ref[i, tt:tt + 1, :], (rows2, LANES))
        svb = [bc(sv_ref, k + b) for b in range(k)]
        ub = [bc(u_ref, b) for b in range(k)]
        c1, c2, zinv, last1 = bc(misc_ref, 0), bc(misc_ref, 1), bc(misc_ref, 2), bc(sv_ref, k - 1)
        n1 = jnp.zeros((rows2, LANES), F32)
        n2 = jnp.zeros((rows2, LANES), F32)
        for kt in range(N_KEYS // rows2):
            rows = slice(kt * rows2, (kt + 1) * rows2)
            x1 = s_ref[0, rows, cols]
            x2 = s_ref[1, rows, cols]
            r = jnp.full((rows2, LANES), float(k), F32)
            c = jnp.full((rows2, LANES), float(k), F32)
            for b in reversed(range(k)):
                r = jnp.where(svb[b] <= x2, float(b), r)
                c = jnp.where(ub[b] > x1, float(b), c)
            r2_ref[0, rows, cols] = r.astype(BF16)
            cnt_ref[0, rows, cols] = c
            e1_ref[0, rows, cols] = jnp.exp(x1 - c1)
            e2_ref[0, rows, cols] = (jnp.exp(x2 - c2) * zinv).astype(BF16)
            n1 = n1 + jnp.where(x1 >= last1, 1.0, 0.0)
            n2 = n2 + jnp.where(r < float(k), 1.0, 0.0)
        bad.append(jnp.abs(jnp.sum(n1, axis=0, keepdims=True) - k)
                   + jnp.abs(jnp.sum(n2, axis=0, keepdims=True) - k))

    any_tied = jnp.max(sum(bad)) + jnp.max(misc_ref[3]) > 0.0

    @pl.when(any_tied)
    def _():
        for tt in range(SUBLANES):
            tied = jnp.max(bad[tt]) + jnp.max(misc_ref[3, tt:tt + 1, :]) > 0.0

            @pl.when(tied)
            def _():
                cols = slice(tt * LANES, (tt + 1) * LANES)
                r2, cnt, e1, e2 = _topk_exact_tile(s_ref[0, :, cols], s_ref[1, :, cols], sv1_ref, sv2_ref)
                r2_ref[0, :, cols] = r2.astype(BF16)
                cnt_ref[0, :, cols] = cnt
                e1_ref[0, :, cols] = e1
                e2_ref[0, :, cols] = e2.astype(BF16)


def _peer_topk(s):
    ng, nk, t = s.shape
    fast = t % _TILES == 0
    lanes = _TILES if fast else LANES
    spec = pl.BlockSpec((1, nk, lanes), lambda i, hd: (hd, 0, i))
    shp = lambda dt: jax.ShapeDtypeStruct((PEER_HEADS, nk, t), dt)
    slab = lambda n: pltpu.VMEM((n, SUBLANES, LANES), F32)
    row = pltpu.VMEM((PEER_TOPK, LANES), F32)
    scratch = [slab(2 * N_KEYS), slab(2 * PEER_TOPK), slab(PEER_TOPK), slab(4), row, row] if fast else [row, row]
    return pl.pallas_call(
        _peer_topk_kernel if fast else _peer_topk_small_kernel,
        grid=(t // lanes, PEER_HEADS),
        in_specs=[pl.BlockSpec((2, nk, lanes), lambda i, hd: (hd, 0, i))],
        out_specs=[spec, spec, spec, spec],
        out_shape=[shp(BF16), shp(F32), shp(F32), shp(BF16)],
        scratch_shapes=scratch,
        compiler_params=_cparams("parallel", "parallel"),
        name="peer_topk",
    )(s)


def _gelu_tanh(x):
    c = -2.0 * math.sqrt(2.0 / math.pi) * math.log2(math.e)
    return x / (1.0 + jnp.exp2(x * (c + (c * 0.044715) * (x * x))))


_DENSE_CHUNK_ROWS = 8
_DENSE_KW = 512


def _peer_dense_kernel(*refs, rows_per_step, n_u, n_v):
    h_ref, xt_ref = refs[:2]
    u_refs = refs[2:2 + n_u]
    vt_refs = refs[2 + n_u:2 + n_u + n_v]
    r2_ref, cnt_ref, e1_ref, e2_ref, o_ref, acc_ref = refs[2 + n_u + n_v:]
    e = pl.program_id(1)
    lanes = xt_ref.shape[-1]
    bf16_rows = 2 * SUBLANES
    kw = _DENSE_KW

    @pl.when(e == 0)
    def _():
        acc_ref[...] = jnp.zeros_like(acc_ref)

    ce = _DENSE_CHUNK_ROWS * N_KEYS
    n_chunks = rows_per_step // _DENSE_CHUNK_ROWS
    zero = jnp.zeros((), BF16)

    def scores(c):
        out = None
        for k, u_ref in enumerate(u_refs):
            part = jnp.dot(u_ref[c * ce:(c + 1) * ce, :], xt_ref[k * kw:(k + 1) * kw, :],
                           preferred_element_type=F32)
            out = part if out is None else part + out
        return out

    def packed_row(ref, hd, i):
        row = jnp.broadcast_to(ref[hd, pl.ds(i, 1), :], (bf16_rows, lanes))
        return jnp.tile(row.astype(BF16), (N_KEYS // bf16_rows, 1))

    a_next = scores(0)
    total = None
    for c in range(n_chunks):
        a = a_next
        ws = []
        for ii in range(_DENSE_CHUNK_ROWS):
            i = e * rows_per_step + c * _DENSE_CHUNK_ROWS + ii
            w = None
            for hd in range(PEER_HEADS):
                term = packed_row(e1_ref, hd, i) * jnp.where(r2_ref[hd] < packed_row(cnt_ref, hd, i),
                                                             e2_ref[hd], zero)
                w = term if w is None else w + term
            ws.append(w)
        if c + 1 < n_chunks:
            a_next = scores(c + 1)
        act = _gelu_tanh(a.astype(BF16)) * jnp.concatenate(ws, axis=0)
        for k in range(ce // kw):
            part = jnp.dot(vt_refs[c * (ce // kw) + k][...], act[k * kw:(k + 1) * kw, :],
                           preferred_element_type=F32)
            total = part if total is None else part + total
    acc_ref[...] += total

    @pl.when(e == pl.num_programs(1) - 1)
    def _():
        o_ref[...] = h_ref[...] + acc_ref[...].T


def _peer_dense(h, xt, u, vt, layer, r2, cnt, e1, e2, lanes, rows_per_step):
    t, d = h.shape
    ne = u.shape[1]
    eb = rows_per_step * N_KEYS
    kw = _DENSE_KW
    n_u, n_v = d // kw, eb // kw
    tok = pl.BlockSpec((PEER_HEADS, N_KEYS, lanes), lambda i, e: (0, 0, i))
    u_specs = [pl.BlockSpec((None, eb, kw), functools.partial(lambda i, e, k: (layer, e, k), k=k))
               for k in range(n_u)]
    v_specs = [pl.BlockSpec((None, d, kw), functools.partial(lambda i, e, k: (layer, 0, e * n_v + k), k=k))
               for k in range(n_v)]
    return pl.pallas_call(
        functools.partial(_peer_dense_kernel, rows_per_step=rows_per_step, n_u=n_u, n_v=n_v),
        grid=(t // lanes, ne // eb),
        in_specs=[pl.BlockSpec((lanes, d), lambda i, e: (i, 0)),
                  pl.BlockSpec((d, lanes), lambda i, e: (0, i))] + u_specs + v_specs + [tok, tok, tok, tok],
        out_specs=pl.BlockSpec((lanes, d), lambda i, e: (i, 0)),
        out_shape=jax.ShapeDtypeStruct((t, d), F32),
        scratch_shapes=[pltpu.VMEM((d, lanes), F32)],
        compiler_params=_cparams("parallel", "arbitrary"),
        name="peer_dense",
    )(h, xt, *([u] * n_u), *([vt] * n_v), r2, cnt, e1, e2)


def _pad_groups(w, real, n_groups):
    k = w.shape[0]
    w = w.reshape(k, n_groups, real)
    return jnp.pad(w, ((0, 0), (0, 0), (0, HEAD_PAD - real))).reshape(k, n_groups * HEAD_PAD)


def _with_rope_copy(w):
    first = w[..., ROPE_LO:ROPE_LO + ROPE_HALF]
    pad = jnp.zeros(w.shape[:-1] + (HEAD_PAD - QK_DIM - ROPE_HALF,), w.dtype)
    return jnp.concatenate([w, first, pad], axis=-1)


def _layer_params(l, w_in, w_conv_out, w_uq, w_ukv, q_norm_g, k_norm_g, w_mla_out, w_out, peer_wq,
                  peer_keys):
    d = w_in.shape[1]
    wi = w_in[l]
    o0 = 2 * D_CONV
    o1 = o0 + Q_LORA
    o2 = o1 + KV_LORA
    o3 = o2 + ROPE_DIM
    rope_cols = _with_rope_copy(jnp.pad(wi[:, o2:o3], ((0, 0), (ROPE_LO, 0))))
    w_in_p = jnp.concatenate([wi[:, :o2], rope_cols, wi[:, o3:]], axis=1).astype(BF16)
    wkv = w_ukv[l].reshape(KV_LORA, N_HEADS, NOPE_DIM + V_DIM)
    wk = _pad_groups(wkv[:, :, :NOPE_DIM].reshape(KV_LORA, N_HEADS * NOPE_DIM), NOPE_DIM, N_HEADS)
    wv = wkv[:, :, NOPE_DIM:].reshape(KV_LORA, N_HEADS * V_DIM)
    padg = lambda g: _with_rope_copy(g).reshape(1, HEAD_PAD)
    return dict(
        w_in=w_in_p,
        w_conv_out=w_conv_out[l].astype(BF16),
        wq=_with_rope_copy(w_uq[l].reshape(Q_LORA, N_HEADS, QK_DIM)).reshape(Q_LORA, N_HEADS * HEAD_PAD).astype(BF16),
        wk=wk.astype(BF16),
        wv=wv.astype(BF16),
        qg=padg(q_norm_g[l]),
        kg=padg(k_norm_g[l]),
        w_mla_out=w_mla_out[l].astype(BF16),
        w_out=w_out[l].astype(BF16),
        peer_wqt=peer_wq[l],
        peer_keys=peer_keys[l].reshape(2 * PEER_HEADS, N_KEYS, PEER_HALF).astype(BF16),
    )


def _rope_tables(length):
    pos = jnp.arange(length, dtype=F32)
    inv = 1.0 / (ROPE_THETA ** (jnp.arange(0, ROPE_DIM, 2, dtype=F32) / ROPE_DIM))
    ang = pos[:, None] * inv[None, :]
    cos = jnp.concatenate([jnp.cos(ang), jnp.cos(ang)], axis=-1)
    sin = jnp.concatenate([-jnp.sin(ang), jnp.sin(ang)], axis=-1)
    cos = jnp.pad(cos, ((0, 0), (ROPE_LO, 0)), constant_values=1.0)
    cos = jnp.pad(cos, ((0, 0), (0, HEAD_PAD - ROPE_LO - ROPE_DIM)))
    sin = jnp.pad(sin, ((0, 0), (ROPE_LO, HEAD_PAD - ROPE_LO - ROPE_DIM)))
    return cos, sin


def _row(v):
    return v.reshape(1, -1)


def kernel(x, meta_tokens, mix_norm_g, w_in, conv_w, conv_b, conv_ln_g, conv_ln_b, w_conv_out,
           q_a_norm_g, w_uq, kv_a_norm_g, w_ukv, q_norm_g, k_norm_g, w_mla_out, w_out, ffn_norm_g,
           peer_wq, peer_keys, peer_u, peer_v):
    bsz, seq, d = x.shape
    depth = w_in.shape[0]
    t = bsz * seq
    assert meta_tokens.shape[0] == N_META and seq % 256 == 0 and d % LANES == 0
    tb = min(1024, seq)
    tq = min(512, seq)
    dense_lanes = min(512, t)
    rows_per_step = 16

    cos, sin = _rope_tables(N_META + seq)
    h = x.reshape(t, d)
    hm = meta_tokens.astype(x.dtype)
    zero_pre = jnp.zeros((N_META, D_CONV), F32)

    peer_wqt = jnp.swapaxes(peer_wq, 1, 2).astype(BF16)
    peer_ub = peer_u.astype(BF16)
    peer_vt = jnp.swapaxes(peer_v, 1, 2).astype(BF16)

    for l in range(depth):
        p = _layer_params(l, w_in, w_conv_out, w_uq, w_ukv, q_norm_g, k_norm_g, w_mla_out, w_out,
                          peer_wqt, peer_keys)
        last = l == depth - 1
        g_mix, qag, kvag = _row(mix_norm_g[l]), _row(q_a_norm_g[l]), _row(kv_a_norm_g[l])
        conv_args = (conv_w[l], _row(conv_b[l]), _row(conv_ln_g[l]), _row(conv_ln_b[l]), p["w_conv_out"])

        um, cqm, ckvm, krm, gatem = _inproj(hm, g_mix, p["w_in"], qag, kvag, N_META)
        qm, km, vm = _qkv(cqm, ckvm, krm, p["wq"], p["wk"], p["wv"], p["qg"], p["kg"],
                          cos[:N_META], sin[:N_META], N_META)
        km_p = jnp.pad(km, ((0, META_PAD - N_META), (0, 0)))
        vm_p = jnp.pad(vm, ((0, META_PAD - N_META), (0, 0)))

        u, cq, ckv, kr, gates = _inproj(h, g_mix, p["w_in"], qag, kvag, tb)
        gc = _conv_branch(u.reshape(bsz, seq, D_CONV), um, *conv_args, gates.reshape(bsz, seq, 2 * d))
        q, k, v = _qkv(cq, ckv, kr, p["wq"], p["wk"], p["wv"], p["qg"], p["kg"],
                       cos[N_META:], sin[N_META:], tb)
        hp = N_HEADS * HEAD_PAD
        o = _attention(q.reshape(bsz, seq, hp), k.reshape(bsz, seq, hp),
                       v.reshape(bsz, seq, N_HEADS * V_DIM), km_p, vm_p, tq)
        h1, xt, s = _merge_scores(h, o.reshape(t, N_HEADS * V_DIM), gc.reshape(t, d), gates,
                                  p["w_mla_out"], p["w_out"], _row(ffn_norm_g[l]), p["peer_wqt"],
                                  p["peer_keys"], tb // 2)
        r2, cnt, e1, e2 = _peer_topk(s)
        h = _peer_dense(h1, xt, peer_ub, peer_vt, l, r2, cnt, e1, e2, dense_lanes, rows_per_step)

        if not last:
            gcm = _conv_branch(um.reshape(1, N_META, D_CONV), zero_pre, *conv_args,
                               gatem.reshape(1, N_META, 2 * d))
            om = _attention(qm.reshape(1, N_META, hp), None, None, km_p, vm_p, N_META)
            hm1 = _merge(hm, om.reshape(N_META, N_HEADS * V_DIM), gcm.reshape(N_META, d), gatem,
                         p["w_mla_out"], p["w_out"], N_META)
            hm1_p = jnp.pad(hm1, ((0, LANES - N_META), (0, 0)))
            xtm, sm = _peer_scores(hm1_p, _row(ffn_norm_g[l]), p["peer_wqt"], p["peer_keys"], LANES)
            r2m, cntm, e1m, e2m = _peer_topk(sm)
            hm = _peer_dense(hm1_p, xtm, peer_ub, peer_vt, l, r2m, cntm, e1m, e2m, LANES,
                             rows_per_step)[:N_META]

    return h.reshape(bsz, seq, d)
```
